```python
import math
import jax, jax.numpy as jnp
from jax import lax
import numpy as np

D_MODEL = 1024
BATCH = 4
SEQ = 4096
DEPTH = 2

N_A_LAYERS = (DEPTH + 1) // 2
N_B_LAYERS = DEPTH // 2

CHUNK = 64
Q_BLOCK = 128
MEM_LEN = 256
MIX_W = D_MODEL
MEM_W = D_MODEL // 4
MEM_HEADS = 4
MEM_HD = MEM_W // MEM_HEADS
TOK_W = MIX_W - MEM_W
DA_DH = 64
DA_HEADS = TOK_W // (2 * DA_DH)
DA_IN = 3 * TOK_W + MEM_W
SSM_HD = 64
SSM_HEADS = TOK_W // SSM_HD
SSM_GROUPS = 2
SSM_STATE = 128
SSM_CONV_K = 4
SSM_CONV_DIM = TOK_W + 2 * SSM_GROUPS * SSM_STATE
SSM_IN = TOK_W + SSM_CONV_DIM + SSM_HEADS + MEM_W
FFN_DIM = 128 * ((8 * D_MODEL // 3 + 127) // 128)
N_EXPERTS = 8
TOP_K = 2
EXPERT_DIM = 7 * D_MODEL // 2
EPS = 1e-6

kernel_name = "hybrid_diffattn_ssd_moe_streaming"


def rms_norm(x, g):
    xf = x.astype(jnp.float32)
    y = xf * lax.rsqrt(jnp.mean(xf * xf, axis=-1, keepdims=True) + EPS)
    return (y * g.astype(jnp.float32)).astype(x.dtype)


def swiglu(h, w_gate, w_up, w_down):
    return (jax.nn.silu(h @ w_gate) * (h @ w_up)) @ w_down


def diff_attention(q, k, v, qn_g, kn_g, lam, sub_g, lambda_init):
    b, s = q.shape[:2]
    q = rms_norm(q, qn_g)
    k = rms_norm(k, kn_g)
    scale = DA_DH ** -0.5
    nqb = s // Q_BLOCK
    q_blocks = jnp.swapaxes(q.reshape(b, nqb, Q_BLOCK, DA_HEADS, 2, DA_DH), 0, 1)
    pos_chunk = jnp.arange(s) // CHUNK
    q_chunk = pos_chunk.reshape(nqb, Q_BLOCK)

    def one_block(args):
        q_blk, qc = args
        sc = jnp.einsum('bqhid,bkhid->bhiqk', q_blk, k).astype(jnp.float32) * scale
        mask = pos_chunk[None, :] <= qc[:, None]
        sc = jnp.where(mask, sc, -jnp.inf)
        p = jax.nn.softmax(sc, axis=-1)
        a = p[:, :, 0] - lam * p[:, :, 1]
        return jnp.einsum('bhqk,bkhe->bqhe', a.astype(v.dtype), v)

    o = lax.map(one_block, (q_blocks, q_chunk))
    o = jnp.swapaxes(o, 0, 1).reshape(b, s, DA_HEADS, 2 * DA_DH)
    o = rms_norm(o, sub_g) * (1.0 - lambda_init)
    return o.reshape(b, s, TOK_W)


def memory_attention(mq, mem_n, w_kv, qn_g, kn_g):
    b, s = mq.shape[:2]
    m = mem_n.shape[1]
    q = rms_norm(mq.reshape(b, s, MEM_HEADS, MEM_HD), qn_g)
    kv = mem_n @ w_kv
    k = rms_norm(kv[..., :MEM_W].reshape(b, m, MEM_HEADS, MEM_HD), kn_g)
    v = kv[..., MEM_W:].reshape(b, m, MEM_HEADS, MEM_HD)
    sc = jnp.einsum('bqhd,bkhd->bhqk', q, k).astype(jnp.float32) * (MEM_HD ** -0.5)
    p = jax.nn.softmax(sc, axis=-1)
    o = jnp.einsum('bhqk,bkhd->bqhd', p.astype(v.dtype), v)
    return o.reshape(b, s, MEM_W)


def causal_depthwise_conv(x, w):
    kw, c = w.shape
    return lax.conv_general_dilated(x, w[:, None, :].astype(x.dtype), window_strides=(1,),
                                    padding=[(kw - 1, 0)], dimension_numbers=('NWC', 'WIO', 'NWC'),
                                    feature_group_count=c)


def ssd(x, dt, a, bm, cm):
    b, s, h, p = x.shape
    g, n = bm.shape[2], bm.shape[3]
    hg = h // g
    nc = s // CHUNK
    f32 = jnp.float32
    x = x.astype(f32).reshape(b, nc, CHUNK, g, hg, p)
    dt = dt.reshape(b, nc, CHUNK, g, hg)
    bm = bm.astype(f32).reshape(b, nc, CHUNK, g, n)
    cm = cm.astype(f32).reshape(b, nc, CHUNK, g, n)
    cum = jnp.cumsum(dt * a.reshape(g, hg), axis=2)
    tri = jnp.tril(jnp.ones((CHUNK, CHUNK), dtype=bool))
    seg = cum[:, :, :, None] - cum[:, :, None, :]
    decay = jnp.exp(jnp.where(tri[:, :, None, None], seg, -jnp.inf))
    cb = jnp.einsum('bctgn,bcsgn->bctsg', cm, bm)
    w = cb[..., None] * decay * dt[:, :, None]
    y_diag = jnp.einsum('bctsgh,bcsghp->bctghp', w, x)
    to_end = jnp.exp(cum[:, :, -1:] - cum) * dt
    states = jnp.einsum('bclgn,bclgh,bclghp->bcghpn', bm, to_end, x)
    chunk_decay = jnp.exp(cum[:, :, -1])

    def step(carry, inp):
        st, dec = inp
        return carry * dec[..., None, None] + st, carry

    h0 = jnp.zeros((b, g, hg, p, n), f32)
    _, h_in = lax.scan(step, h0, (jnp.moveaxis(states, 1, 0), jnp.moveaxis(chunk_decay, 1, 0)))
    h_in = jnp.moveaxis(h_in, 0, 1)
    y_off = jnp.einsum('bclgn,bcghpn,bclgh->bclghp', cm, h_in, jnp.exp(cum))
    return (y_diag + y_off).reshape(b, s, h, p)


def mamba2_mixer(z, xbc, dt_raw, conv_w, conv_b, dt_bias, a_log, d_skip, norm_g):
    b, s = z.shape[:2]
    xbc = jax.nn.silu(causal_depthwise_conv(xbc, conv_w) + conv_b)
    xs = xbc[..., :TOK_W].reshape(b, s, SSM_HEADS, SSM_HD)
    bm = xbc[..., TOK_W:TOK_W + SSM_GROUPS * SSM_STATE].reshape(b, s, SSM_GROUPS, SSM_STATE)
    cm = xbc[..., TOK_W + SSM_GROUPS * SSM_STATE:].reshape(b, s, SSM_GROUPS, SSM_STATE)
    dt = jax.nn.softplus(dt_raw.astype(jnp.float32) + dt_bias.astype(jnp.float32))
    a = -jnp.exp(a_log.astype(jnp.float32))
    y = ssd(xs, dt, a, bm, cm)
    y = y + d_skip.astype(jnp.float32)[:, None] * xs.astype(jnp.float32)
    y = y.reshape(b, s, TOK_W) * jax.nn.silu(z.astype(jnp.float32))
    y = rms_norm(y.reshape(b, s, SSM_GROUPS, TOK_W // SSM_GROUPS),
                 norm_g.reshape(SSM_GROUPS, TOK_W // SSM_GROUPS)).reshape(b, s, TOK_W)
    return y.astype(z.dtype)


def moe_swiglu(h, w_router, w_gate, w_up, w_down):
    logits = (h @ w_router).astype(jnp.float32)
    top_val, top_idx = lax.top_k(logits, TOP_K)
    top_w = jax.nn.softmax(top_val, axis=-1)
    gates = jnp.sum(jax.nn.one_hot(top_idx, N_EXPERTS, dtype=jnp.float32) * top_w[..., None], axis=-2)
    out = jnp.zeros(h.shape, jnp.float32)
    for e in range(N_EXPERTS):
        out = out + gates[..., e:e + 1] * swiglu(h, w_gate[e], w_up[e], w_down[e])
    return out.astype(h.dtype)


def setup_inputs(seed: int = 0) -> dict:
    key = jax.random.key(seed)
    ks = iter(jax.random.split(key, 48))
    f32 = jnp.float32

    def nrm(shape, scale):
        return jax.random.normal(next(ks), shape, f32) * scale

    def gain(shape):
        return 1.0 + nrm(shape, 0.02)

    NA, NB = N_A_LAYERS, N_B_LAYERS
    dt0 = jnp.exp(jax.random.uniform(next(ks), (NB, SSM_HEADS), f32,
                                     minval=math.log(1e-3), maxval=math.log(1e-1)))
    dt_bias = dt0 + jnp.log(-jnp.expm1(-dt0))
    a_log = jnp.log(jax.random.uniform(next(ks), (NB, SSM_HEADS), f32, minval=1.0, maxval=16.0))
    return {
        "x": nrm((BATCH, SEQ, D_MODEL), 1.0),
        "mem": nrm((BATCH, MEM_LEN, D_MODEL), 1.0),
        "ln1_g": gain((DEPTH, D_MODEL)),
        "ln2_g": gain((DEPTH, D_MODEL)),
        "mem_norm_g": gain((D_MODEL,)),
        "w_out": nrm((DEPTH, MIX_W, D_MODEL), MIX_W ** -0.5),
        "mem_w_kv": nrm((DEPTH, D_MODEL, 2 * MEM_W), D_MODEL ** -0.5),
        "mem_qn_g": gain((DEPTH, MEM_HD)),
        "mem_kn_g": gain((DEPTH, MEM_HD)),
        "da_w_in": nrm((NA, D_MODEL, DA_IN), D_MODEL ** -0.5),
        "da_qn_g": gain((NA, DA_DH)),
        "da_kn_g": gain((NA, DA_DH)),
        "da_lq1": nrm((NA, DA_DH), 0.1),
        "da_lk1": nrm((NA, DA_DH), 0.1),
        "da_lq2": nrm((NA, DA_DH), 0.1),
        "da_lk2": nrm((NA, DA_DH), 0.1),
        "da_sub_g": gain((NA, 2 * DA_DH)),
        "ssm_w_in": nrm((NB, D_MODEL, SSM_IN), D_MODEL ** -0.5),
        "ssm_conv_w": nrm((NB, SSM_CONV_K, SSM_CONV_DIM), SSM_CONV_K ** -0.5),
        "ssm_conv_b": nrm((NB, SSM_CONV_DIM), 0.02),
        "ssm_dt_bias": dt_bias,
        "ssm_a_log": a_log,
        "ssm_d": gain((NB, SSM_HEADS)),
        "ssm_norm_g": gain((NB, TOK_W)),
        "ffn_w_gate": nrm((NA, D_MODEL, FFN_DIM), D_MODEL ** -0.5),
        "ffn_w_up": nrm((NA, D_MODEL, FFN_DIM), D_MODEL ** -0.5),
        "ffn_w_down": nrm((NA, FFN_DIM, D_MODEL), FFN_DIM ** -0.5),
        "moe_w_router": nrm((NB, D_MODEL, N_EXPERTS), D_MODEL ** -0.5),
        "moe_w_gate": nrm((NB, N_EXPERTS, D_MODEL, EXPERT_DIM), D_MODEL ** -0.5),
        "moe_w_up": nrm((NB, N_EXPERTS, D_MODEL, EXPERT_DIM), D_MODEL ** -0.5),
        "moe_w_down": nrm((NB, N_EXPERTS, EXPERT_DIM, D_MODEL), EXPERT_DIM ** -0.5),
    }


def reference(x, mem, ln1_g, ln2_g, mem_norm_g, w_out, mem_w_kv, mem_qn_g, mem_kn_g,
              da_w_in, da_qn_g, da_kn_g, da_lq1, da_lk1, da_lq2, da_lk2, da_sub_g,
              ssm_w_in, ssm_conv_w, ssm_conv_b, ssm_dt_bias, ssm_a_log, ssm_d, ssm_norm_g,
              ffn_w_gate, ffn_w_up, ffn_w_down,
              moe_w_router, moe_w_gate, moe_w_up, moe_w_down):
    b, s = x.shape[:2]
    mem_n = rms_norm(mem, mem_norm_g)
    for i in range(DEPTH):
        j = i // 2
        h = rms_norm(x, ln1_g[i])
        if i % 2 == 0:
            u = h @ da_w_in[j]
            q = u[..., :TOK_W].reshape(b, s, DA_HEADS, 2, DA_DH)
            k = u[..., TOK_W:2 * TOK_W].reshape(b, s, DA_HEADS, 2, DA_DH)
            v = u[..., 2 * TOK_W:3 * TOK_W].reshape(b, s, DA_HEADS, 2 * DA_DH)
            mq = u[..., 3 * TOK_W:]
            lambda_init = 0.8 - 0.6 * math.exp(-0.3 * i)
            lam = (jnp.exp(jnp.sum(da_lq1[j].astype(jnp.float32) * da_lk1[j].astype(jnp.float32)))
                   - jnp.exp(jnp.sum(da_lq2[j].astype(jnp.float32) * da_lk2[j].astype(jnp.float32)))
                   + lambda_init)
            tok = diff_attention(q, k, v, da_qn_g[j], da_kn_g[j], lam, da_sub_g[j], lambda_init)
        else:
            u = h @ ssm_w_in[j]
            z = u[..., :TOK_W]
            xbc = u[..., TOK_W:TOK_W + SSM_CONV_DIM]
            dt_raw = u[..., TOK_W + SSM_CONV_DIM:TOK_W + SSM_CONV_DIM + SSM_HEADS]
            mq = u[..., TOK_W + SSM_CONV_DIM + SSM_HEADS:]
            tok = mamba2_mixer(z, xbc, dt_raw, ssm_conv_w[j], ssm_conv_b[j], ssm_dt_bias[j],
                               ssm_a_log[j], ssm_d[j], ssm_norm_g[j])
        mo = memory_attention(mq, mem_n, mem_w_kv[i], mem_qn_g[i], mem_kn_g[i])
        mixed = jnp.concatenate([tok.astype(h.dtype), mo.astype(h.dtype)], axis=-1)
        x = x + (mixed @ w_out[i]).astype(x.dtype)
        h = rms_norm(x, ln2_g[i])
        if i % 2 == 0:
            x = x + swiglu(h, ffn_w_gate[j], ffn_w_up[j], ffn_w_down[j]).astype(x.dtype)
        else:
            x = x + moe_swiglu(h, moe_w_router[j], moe_w_gate[j], moe_w_up[j], moe_w_down[j]).astype(x.dtype)
    return x
```

```python
import functools
import math

import jax
import jax.numpy as jnp
from jax import lax
from jax.experimental import pallas as pl
from jax.experimental.pallas import tpu as pltpu

F32 = jnp.float32
BF16 = jnp.bfloat16

D_MODEL = 1024
CHUNK = 64
MEM_LEN = 256
MEM_W = 256
MEM_HEADS = 4
MEM_HD = 64
TOK_W = 768
DA_DH = 64
DA_HEADS = 6
SSM_HD = 64
SSM_HEADS = 12
SSM_GROUPS = 2
SSM_STATE = 128
SSM_CONV_K = 4
SSM_CONV_DIM = 1280
N_EXPERTS = 8
EPS = 1e-6
LANES = 128
NEG = -1e30
VMEM_LIMIT = 56 * 1024 * 1024


def _cparams(sem):
    return pltpu.CompilerParams(dimension_semantics=sem, vmem_limit_bytes=VMEM_LIMIT)


def _rms(xf, g):
    ms = jnp.mean(xf * xf, axis=-1, keepdims=True)
    return xf * lax.rsqrt(ms + EPS) * g


def _seg_mean_matrix(n, seg_shift):
    r = lax.broadcasted_iota(jnp.int32, (n, n), 0) >> seg_shift
    c = lax.broadcasted_iota(jnp.int32, (n, n), 1) >> seg_shift
    return jnp.where(r == c, 1.0 / (1 << seg_shift), 0.0).astype(BF16)


def _seg_mean_sq(y, seg_shift=6):
    bd = _seg_mean_matrix(256, seg_shift)
    sq = (y * y).astype(BF16)
    parts = [jnp.dot(sq[:, c:c + 256], bd, preferred_element_type=F32)
             for c in range(0, y.shape[1], 256)]
    return parts[0] if len(parts) == 1 else jnp.concatenate(parts, axis=1)


def _inproj0_kernel(x_ref, g_ref, w_ref, qkg_ref, u_ref):
    h = _rms(x_ref[...], g_ref[...]).astype(BF16)
    u = jnp.dot(h, w_ref[...], preferred_element_type=F32)
    nqk = 2 * TOK_W
    qk = u[:, :nqk]
    qkn = qk * lax.rsqrt(_seg_mean_sq(qk) + EPS) * qkg_ref[...]
    u_ref[:, :nqk] = qkn.astype(BF16)
    u_ref[:, nqk:] = u[:, nqk:].astype(BF16)


def _inproj0(x, g, w, qkg, tm=512):
    t = x.shape[0]
    n = w.shape[1]
    return pl.pallas_call(
        _inproj0_kernel,
        out_shape=jax.ShapeDtypeStruct((t, n), BF16),
        grid=(t // tm,),
        in_specs=[pl.BlockSpec((tm, D_MODEL), lambda i: (i, 0)),
                  pl.BlockSpec((1, D_MODEL), lambda i: (0, 0)),
                  pl.BlockSpec((D_MODEL, n), lambda i: (0, 0)),
                  pl.BlockSpec((1, 2 * TOK_W), lambda i: (0, 0))],
        out_specs=pl.BlockSpec((tm, n), lambda i: (i, 0)),
        compiler_params=_cparams(("parallel",)),
        name="inproj0",
    )(x, g, w, qkg)


def _attn_kernel(lq1_ref, lk1_ref, lq2_ref, lk2_ref, subg_ref, q_ref, k_ref, v_ref, o_ref,
                 acc_ref, m_ref, l_ref, *, tq, tk, lambda_init):
    qi = pl.program_id(2)
    q = q_ref[0]
    lane = lax.broadcasted_iota(jnp.int32, q.shape, 1)
    zero = jnp.zeros_like(q)
    qs = (jnp.where(lane < DA_DH, q, zero), jnp.where(lane >= DA_DH, q, zero))
    acc_ref[...] = jnp.zeros_like(acc_ref)
    m_ref[...] = jnp.full_like(m_ref, NEG)
    l_ref[...] = jnp.zeros_like(l_ref)

    def step(ki, masked):
        start = pl.multiple_of(ki * tk, tk)
        k = k_ref[0, pl.ds(start, tk), :]
        v = v_ref[0, pl.ds(start, tk), :]
        if masked:
            rc = lax.broadcasted_iota(jnp.int32, (tq, tk), 0) >> 6
            cc = lax.broadcasted_iota(jnp.int32, (tq, tk), 1) >> 6
            vis = cc <= rc
        for i in range(2):
            s = lax.dot_general(qs[i], k, (((1,), (1,)), ((), ())), preferred_element_type=F32)
            if masked:
                s = jnp.where(vis, s, NEG)
            m_old = m_ref[i]
            m_new = jnp.maximum(m_old, jnp.max(s, axis=1, keepdims=True))
            alpha = jnp.exp(m_old - m_new)
            p = jnp.exp(s - m_new)
            l_ref[i] = alpha * l_ref[i] + jnp.sum(p, axis=1, keepdims=True)
            acc_ref[i] = alpha * acc_ref[i] + jnp.dot(p.astype(BF16), v, preferred_element_type=F32)
            m_ref[i] = m_new

    def body(ki, carry):
        step(ki, False)
        return carry

    lax.fori_loop(0, qi, body, 0)
    step(qi, True)

    lam = (jnp.exp(jnp.sum(lq1_ref[...] * lk1_ref[...])) - jnp.exp(jnp.sum(lq2_ref[...] * lk2_ref[...]))
           + lambda_init)
    o = acc_ref[0] / l_ref[0] - lam * (acc_ref[1] / l_ref[1])
    o = _rms(o, subg_ref[...]) * (1.0 - lambda_init)
    o_ref[0] = o.astype(o_ref.dtype)


def _diff_attention(u3, lq1, lk1, lq2, lk2, subg, lambda_init, tq=512):
    b, s, _ = u3.shape
    tk = tq
    nq = s // tq
    hd = 2 * DA_DH
    vec = lambda n: pl.BlockSpec((1, n), lambda bi, h, qi: (0, 0))
    kern = functools.partial(_attn_kernel, tq=tq, tk=tk, lambda_init=lambda_init)
    return pl.pallas_call(
        kern,
        out_shape=jax.ShapeDtypeStruct((b, s, TOK_W), BF16),
        grid=(b, DA_HEADS, nq),
        in_specs=[vec(DA_DH), vec(DA_DH), vec(DA_DH), vec(DA_DH), vec(hd),
                  pl.BlockSpec((1, tq, hd), lambda bi, h, qi: (bi, qi, h)),
                  pl.BlockSpec((1, s, hd), lambda bi, h, qi: (bi, 0, DA_HEADS + h)),
                  pl.BlockSpec((1, s, hd), lambda bi, h, qi: (bi, 0, 2 * DA_HEADS + h))],
        out_specs=pl.BlockSpec((1, tq, hd), lambda bi, h, qi: (bi, qi, h)),
        scratch_shapes=[pltpu.VMEM((2, tq, hd), F32), pltpu.VMEM((2, tq, 1), F32),
                        pltpu.VMEM((2, tq, 1), F32)],
        compiler_params=_cparams(("parallel", "parallel", "parallel")),
        name="diff_attn",
    )(lq1, lk1, lq2, lk2, subg, u3, u3, u3)


def _memkv_kernel(mem_ref, g_ref, wkv_ref, kng_ref, kv_ref):
    mn = _rms(mem_ref[0], g_ref[...]).astype(BF16)
    kv = jnp.dot(mn, wkv_ref[0], preferred_element_type=F32)
    k = kv[:, :MEM_W]
    kn = k * lax.rsqrt(_seg_mean_sq(k) + EPS) * kng_ref[0]
    kv_ref[0, 0, :, :MEM_W] = kn.astype(BF16)
    kv_ref[0, 0, :, MEM_W:] = kv[:, MEM_W:].astype(BF16)


def _memkv(mem, g, wkv, kng):
    depth = wkv.shape[0]
    b = mem.shape[0]
    return pl.pallas_call(
        _memkv_kernel,
        out_shape=jax.ShapeDtypeStruct((depth, b, MEM_LEN, 2 * MEM_W), BF16),
        grid=(depth, b),
        in_specs=[pl.BlockSpec((1, MEM_LEN, D_MODEL), lambda d, bi: (bi, 0, 0)),
                  pl.BlockSpec((1, D_MODEL), lambda d, bi: (0, 0)),
                  pl.BlockSpec((1, D_MODEL, 2 * MEM_W), lambda d, bi: (d, 0, 0)),
                  pl.BlockSpec((1, 1, MEM_W), lambda d, bi: (d, 0, 0))],
        out_specs=pl.BlockSpec((1, 1, MEM_LEN, 2 * MEM_W), lambda d, bi: (d, bi, 0, 0)),
        compiler_params=_cparams(("parallel", "parallel")),
        name="memkv",
    )(mem, g, wkv, kng)


def _mem_attention(mq, qg, kv):
    qn = (mq * lax.rsqrt(_seg_mean_sq(mq) + EPS) * qg).astype(BF16)
    k = kv[:, :MEM_W]
    v = kv[:, MEM_W:]
    lane = lax.broadcasted_iota(jnp.int32, qn.shape, 1) >> 6
    out = jnp.zeros(qn.shape, F32)
    for h in range(MEM_HEADS):
        sel = lane == h
        qh = jnp.where(sel, qn, jnp.zeros_like(qn))
        s = lax.dot_general(qh, k, (((1,), (1,)), ((), ())), preferred_element_type=F32)
        p = jnp.exp(s - jnp.max(s, axis=1, keepdims=True))
        l = jnp.sum(p, axis=1, keepdims=True)
        o = jnp.dot(p.astype(BF16), v, preferred_element_type=F32)
        out = jnp.where(sel, o / l, out)
    return out


def _top2_gates(logits):
    lane = lax.broadcasted_iota(jnp.int32, logits.shape, 1)
    lg = jnp.where(lane < N_EXPERTS, logits, NEG)
    m1 = jnp.max(lg, axis=1, keepdims=True)
    i1 = jnp.min(jnp.where(lg == m1, lane, LANES), axis=1, keepdims=True)
    lg2 = jnp.where(lane == i1, NEG, lg)
    m2 = jnp.max(lg2, axis=1, keepdims=True)
    i2 = jnp.min(jnp.where(lg2 == m2, lane, LANES), axis=1, keepdims=True)
    g1 = 1.0 / (1.0 + jnp.exp(m2 - m1))
    g2 = 1.0 - g1
    return jnp.where(lane == i1, g1, jnp.where(lane == i2, g2, 0.0))


def _mixout_kernel(*refs, with_router):
    if with_router:
        (tok_ref, mq_ref, kv_ref, qg_ref, wo_ref, x_ref, g2_ref, wr_ref,
         x1_ref, h2_ref, gates_ref) = refs
    else:
        tok_ref, mq_ref, kv_ref, qg_ref, wo_ref, x_ref, g2_ref, x1_ref, h2_ref = refs
    mo = _mem_attention(mq_ref[...].astype(F32), qg_ref[...], kv_ref[0])
    y = jnp.dot(tok_ref[...], wo_ref[:TOK_W, :], preferred_element_type=F32)
    y = y + jnp.dot(mo.astype(BF16), wo_ref[TOK_W:, :], preferred_element_type=F32)
    x1 = x_ref[...] + y
    x1_ref[...] = x1
    h2 = _rms(x1, g2_ref[...])
    h2_ref[...] = h2.astype(BF16)
    if with_router:
        logits = jnp.dot(h2, wr_ref[...], preferred_element_type=F32, precision=lax.Precision.HIGHEST)
        gates_ref[...] = _top2_gates(logits)


def _mixout(tok, mq_src, mq_block, kv, qg, wo, x, g2, wr=None, tm=512):
    t = x.shape[0]
    tiles_per_batch = t // kv.shape[0] // tm
    with_router = wr is not None
    in_specs = [pl.BlockSpec((tm, TOK_W), lambda i: (i, 0)),
                pl.BlockSpec((tm, MEM_W), lambda i: (i, mq_block)),
                pl.BlockSpec((1, MEM_LEN, 2 * MEM_W), lambda i: (i // tiles_per_batch, 0, 0)),
                pl.BlockSpec((1, MEM_W), lambda i: (0, 0)),
                pl.BlockSpec((D_MODEL, D_MODEL), lambda i: (0, 0)),
                pl.BlockSpec((tm, D_MODEL), lambda i: (i, 0)),
                pl.BlockSpec((1, D_MODEL), lambda i: (0, 0))]
    out_shape = [jax.ShapeDtypeStruct((t, D_MODEL), F32), jax.ShapeDtypeStruct((t, D_MODEL), BF16)]
    out_specs = [pl.BlockSpec((tm, D_MODEL), lambda i: (i, 0)), pl.BlockSpec((tm, D_MODEL), lambda i: (i, 0))]
    args = [tok, mq_src, kv, qg, wo, x, g2]
    if with_router:
        in_specs.append(pl.BlockSpec((D_MODEL, LANES), lambda i: (0, 0)))
        out_shape.append(jax.ShapeDtypeStruct((t, LANES), F32))
        out_specs.append(pl.BlockSpec((tm, LANES), lambda i: (i, 0)))
        args.append(wr)
    return pl.pallas_call(
        functools.partial(_mixout_kernel, with_router=with_router),
        out_shape=out_shape,
        grid=(t // tm,),
        in_specs=in_specs,
        out_specs=out_specs,
        compiler_params=_cparams(("parallel",)),
        name="mixout_router" if with_router else "mixout",
    )(*args)


def _ffn_kernel(*refs, with_gates, with_norm):
    refs = list(refs)
    h_ref, x_ref = refs[0], refs[1]
    pos = 2
    gates_ref = None
    if with_gates:
        gates_ref = refs[pos]
        pos += 1
    wg_ref, wu_ref, wd_ref = refs[pos:pos + 3]
    pos += 3
    gn_ref = None
    if with_norm:
        gn_ref = refs[pos]
        pos += 1
    xo_ref = refs[pos]
    pos += 1
    ho_ref = None
    if with_norm:
        ho_ref = refs[pos]
        pos += 1
    acc_ref = refs[pos]

    e = pl.program_id(1)
    f = pl.program_id(2)
    first = jnp.logical_and(e == 0, f == 0)
    last = jnp.logical_and(e == pl.num_programs(1) - 1, f == pl.num_programs(2) - 1)

    @pl.when(first)
    def _():
        acc_ref[...] = x_ref[...]

    h = h_ref[...]
    a = jnp.dot(h, wg_ref[0], preferred_element_type=F32)
    u = jnp.dot(h, wu_ref[0], preferred_element_type=F32)
    act = a * jax.nn.sigmoid(a) * u
    if with_gates:
        gates = gates_ref[...]
        lane = lax.broadcasted_iota(jnp.int32, gates.shape, 1)
        ge = jnp.sum(jnp.where(lane == e, gates, 0.0), axis=1, keepdims=True)
        act = act * ge
    acc_ref[...] += jnp.dot(act.astype(BF16), wd_ref[0], preferred_element_type=F32)

    @pl.when(last)
    def _():
        xn = acc_ref[...]
        xo_ref[...] = xn
        if with_norm:
            ho_ref[...] = _rms(xn, gn_ref[...]).astype(BF16)


def _ffn(h, x, wg, wu, wd, gates=None, gn=None, tm=512, tf=512):
    t = x.shape[0]
    ne, _, fdim = wg.shape
    with_gates = gates is not None
    with_norm = gn is not None
    in_specs = [pl.BlockSpec((tm, D_MODEL), lambda i, e, f: (i, 0)),
                pl.BlockSpec((tm, D_MODEL), lambda i, e, f: (i, 0))]
    args = [h, x]
    if with_gates:
        in_specs.append(pl.BlockSpec((tm, LANES), lambda i, e, f: (i, 0)))
        args.append(gates)
    in_specs += [pl.BlockSpec((1, D_MODEL, tf), lambda i, e, f: (e, 0, f)),
                 pl.BlockSpec((1, D_MODEL, tf), lambda i, e, f: (e, 0, f)),
                 pl.BlockSpec((1, tf, D_MODEL), lambda i, e, f: (e, f, 0))]
    args += [wg, wu, wd]
    out_shape = [jax.ShapeDtypeStruct((t, D_MODEL), F32)]
    out_specs = [pl.BlockSpec((tm, D_MODEL), lambda i, e, f: (i, 0))]
    if with_norm:
        in_specs.append(pl.BlockSpec((1, D_MODEL), lambda i, e, f: (0, 0)))
        args.append(gn)
        out_shape.append(jax.ShapeDtypeStruct((t, D_MODEL), BF16))
        out_specs.append(pl.BlockSpec((tm, D_MODEL), lambda i, e, f: (i, 0)))
    res = pl.pallas_call(
        functools.partial(_ffn_kernel, with_gates=with_gates, with_norm=with_norm),
        out_shape=out_shape,
        grid=(t // tm, ne, fdim // tf),
        in_specs=in_specs,
        out_specs=out_specs,
        scratch_shapes=[pltpu.VMEM((tm, D_MODEL), F32)],
        compiler_params=_cparams(("parallel", "arbitrary", "arbitrary")),
        name="moe_dense" if with_gates else "ffn",
    )(*args)
    return res


def _inproj1_kernel(x_ref, wz_ref, wx_ref, wm_ref, wdt_ref, z_ref, xbc_ref, mq_ref, dt_ref):
    h = x_ref[...]
    z_ref[...] = jnp.dot(h, wz_ref[...], preferred_element_type=F32).astype(BF16)
    xbc_ref[...] = jnp.dot(h, wx_ref[...], preferred_element_type=F32).astype(BF16)
    mq_ref[...] = jnp.dot(h, wm_ref[...], preferred_element_type=F32).astype(BF16)
    dt_ref[...] = jnp.dot(h, wdt_ref[...], preferred_element_type=F32)


def _inproj1(h, wz, wx, wm, wdt, tm=512):
    t = h.shape[0]
    full = lambda a: pl.BlockSpec(a.shape, lambda i: (0, 0))
    row = lambda n: pl.BlockSpec((tm, n), lambda i: (i, 0))
    return pl.pallas_call(
        _inproj1_kernel,
        out_shape=[jax.ShapeDtypeStruct((t, TOK_W), BF16), jax.ShapeDtypeStruct((t, SSM_CONV_DIM), BF16),
                   jax.ShapeDtypeStruct((t, MEM_W), BF16), jax.ShapeDtypeStruct((t, LANES), F32)],
        grid=(t // tm,),
        in_specs=[row(D_MODEL), full(wz), full(wx), full(wm), full(wdt)],
        out_specs=[row(TOK_W), row(SSM_CONV_DIM), row(MEM_W), row(LANES)],
        compiler_params=_cparams(("parallel",)),
        name="inproj1",
    )(h, wz, wx, wm, wdt)


def _ssd_kernel(z_ref, xbc_ref, dt_ref, cw_ref, cb_ref, dtb_ref, alog_ref, dsk_ref, ng_ref, ex_ref,
                o_ref, xe_ref, st_ref, *, tl):
    j = pl.program_id(1)
    pad = 8

    @pl.when(j == 0)
    def _():
        xe_ref[0:pad, :] = jnp.zeros((pad, SSM_CONV_DIM), F32)
        st_ref[...] = jnp.zeros_like(st_ref)

    x = xbc_ref[0].astype(F32)
    xe_ref[pad:pad + tl, :] = x
    acc = cb_ref[...] + cw_ref[SSM_CONV_K - 1:SSM_CONV_K, :] * x
    for k in range(SSM_CONV_K - 1):
        sh = SSM_CONV_K - 1 - k
        acc = acc + cw_ref[k:k + 1, :] * xe_ref[pad - sh:pad - sh + tl, :]
    xe_ref[0:pad, :] = x[tl - pad:tl, :]
    xc = acc * jax.nn.sigmoid(acc)
    xs = xc[:, :TOK_W]

    dt_in = dt_ref[0] + dtb_ref[...]
    dt = jnp.maximum(dt_in, 0.0) + jnp.log1p(jnp.exp(-jnp.abs(dt_in)))
    a = -jnp.exp(alog_ref[...])
    dta = dt * a
    r = lax.broadcasted_iota(jnp.int32, (tl, tl), 0)
    c = lax.broadcasted_iota(jnp.int32, (tl, tl), 1)
    tril = r >= c
    ltri = jnp.where(tril, 1.0, 0.0).astype(F32)
    cum = jnp.dot(ltri, dta, preferred_element_type=F32, precision=lax.Precision.HIGHEST)
    cum_t = cum.T
    dt_t = dt.T
    lane = lax.broadcasted_iota(jnp.int32, (tl, LANES), 1)

    hg = SSM_HEADS // SSM_GROUPS
    y_pairs = []
    cbs = []
    for g in range(SSM_GROUPS):
        bm = xc[:, TOK_W + g * SSM_STATE:TOK_W + (g + 1) * SSM_STATE].astype(BF16)
        cm = xc[:, TOK_W + (SSM_GROUPS + g) * SSM_STATE:TOK_W + (SSM_GROUPS + g + 1) * SSM_STATE].astype(BF16)
        cbs.append(lax.dot_general(cm, bm, (((1,), (1,)), ((), ())), preferred_element_type=F32))
    for pr in range(SSM_HEADS // 2):
        xp = xs[:, pr * LANES:(pr + 1) * LANES].astype(BF16)
        ys = []
        for hh in (2 * pr, 2 * pr + 1):
            g = hh // hg
            seg = cum[:, hh:hh + 1] - cum_t[hh:hh + 1, :]
            decay = jnp.exp(jnp.where(tril, seg, NEG))
            w = cbs[g] * decay * dt_t[hh:hh + 1, :]
            ys.append(jnp.dot(w.astype(BF16), xp, preferred_element_type=F32))
        y_pairs.append(jnp.where(lane < SSM_HD, ys[0], ys[1]))
    y = jnp.concatenate(y_pairs, axis=1)

    expcum = jnp.exp(cum)
    to_end = jnp.exp(cum[tl - 1:tl, :] - cum) * dt
    stacked = jnp.concatenate([expcum, to_end], axis=0)
    exd = jnp.dot(stacked, ex_ref[...], preferred_element_type=F32, precision=lax.Precision.HIGHEST)
    expcum_x = exd[:tl]
    xw = (xs * exd[tl:]).astype(BF16)
    gw = TOK_W // SSM_GROUPS
    y_off = []
    for g in range(SSM_GROUPS):
        cs = slice(g * gw, (g + 1) * gw)
        bm_t = xc[:, TOK_W + g * SSM_STATE:TOK_W + (g + 1) * SSM_STATE].T.astype(BF16)
        cm = xc[:, TOK_W + (SSM_GROUPS + g) * SSM_STATE:TOK_W + (SSM_GROUPS + g + 1) * SSM_STATE].astype(BF16)
        sg = st_ref[:, cs]
        y_off.append(jnp.dot(cm, sg.astype(BF16), preferred_element_type=F32) * expcum_x[:, cs])
        st_ref[:, cs] = sg * expcum_x[tl - 1:tl, cs] + jnp.dot(bm_t, xw[:, cs], preferred_element_type=F32)
    y = y + jnp.concatenate(y_off, axis=1) + dsk_ref[...] * xs
    zf = z_ref[0].astype(F32)
    y = y * (zf * jax.nn.sigmoid(zf))
    outs = []
    for g in range(SSM_GROUPS):
        cs = slice(g * gw, (g + 1) * gw)
        outs.append(_rms(y[:, cs], ng_ref[:, cs]))
    o_ref[0] = jnp.concatenate(outs, axis=1).astype(o_ref.dtype)


def _ssd(z3, xbc3, dt3, cw, cb, dtb, alog, dsk, ng, ex, tl=256):
    b, s, _ = z3.shape
    full = lambda a: pl.BlockSpec(a.shape, lambda bi, j: (0, 0))
    blk = lambda n: pl.BlockSpec((1, tl, n), lambda bi, j: (bi, j, 0))
    return pl.pallas_call(
        functools.partial(_ssd_kernel, tl=tl),
        out_shape=jax.ShapeDtypeStruct((b, s, TOK_W), BF16),
        grid=(b, s // tl),
        in_specs=[blk(TOK_W), blk(SSM_CONV_DIM), blk(LANES), full(cw), full(cb), full(dtb), full(alog),
                  full(dsk), full(ng), full(ex)],
        out_specs=blk(TOK_W),
        scratch_shapes=[pltpu.VMEM((tl + 8, SSM_CONV_DIM), F32), pltpu.VMEM((SSM_STATE, TOK_W), F32)],
        compiler_params=_cparams(("parallel", "arbitrary")),
        name="conv_ssd",
    )(z3, xbc3, dt3, cw, cb, dtb, alog, dsk, ng, ex)


def _pad_lanes(v, n=LANES):
    return jnp.pad(v, [(0, 0)] * (v.ndim - 1) + [(0, n - v.shape[-1])])


def kernel(x, mem, ln1_g, ln2_g, mem_norm_g, w_out, mem_w_kv, mem_qn_g, mem_kn_g, da_w_in, da_qn_g, da_kn_g,
           da_lq1, da_lk1, da_lq2, da_lk2, da_sub_g, ssm_w_in, ssm_conv_w, ssm_conv_b, ssm_dt_bias, ssm_a_log,
           ssm_d, ssm_norm_g, ffn_w_gate, ffn_w_up, ffn_w_down, moe_w_router, moe_w_gate, moe_w_up, moe_w_down):
    b, s, d = x.shape
    t = b * s
    row = lambda v: v.reshape(1, -1).astype(F32)
    xt = x.reshape(t, d)

    mem_qg = jnp.tile(mem_qn_g.astype(F32) * (MEM_HD ** -0.5), (1, MEM_HEADS))
    mem_kg = jnp.tile(mem_kn_g.astype(F32), (1, MEM_HEADS))[:, None, :]
    kv = _memkv(mem, row(mem_norm_g), mem_w_kv.astype(BF16), mem_kg)
    wo = w_out.astype(BF16)

    lambda_init = 0.8 - 0.6 * math.exp(-0.3 * 0)
    qkg = jnp.concatenate([jnp.tile(da_qn_g[0].astype(F32) * (DA_DH ** -0.5), 2 * DA_HEADS),
                           jnp.tile(da_kn_g[0].astype(F32), 2 * DA_HEADS)]).reshape(1, -1)
    u = _inproj0(xt, row(ln1_g[0]), da_w_in[0].astype(BF16), qkg)
    tok = _diff_attention(u.reshape(b, s, -1), row(da_lq1[0]), row(da_lk1[0]), row(da_lq2[0]), row(da_lk2[0]),
                          row(da_sub_g[0]), lambda_init)
    x1, h2 = _mixout(tok.reshape(t, TOK_W), u, 3 * TOK_W // MEM_W, kv[0], mem_qg[0:1], wo[0], xt, row(ln2_g[0]))
    x2, h3 = _ffn(h2, x1, ffn_w_gate.astype(BF16), ffn_w_up.astype(BF16), ffn_w_down.astype(BF16),
                  gn=row(ln1_g[1]), tm=512, tf=256)

    w_in = ssm_w_in[0]
    o1 = TOK_W
    o2 = o1 + SSM_CONV_DIM
    o3 = o2 + SSM_HEADS
    z, xbc, mq, dt = _inproj1(h3, w_in[:, :o1].astype(BF16), w_in[:, o1:o2].astype(BF16),
                              w_in[:, o3:].astype(BF16), _pad_lanes(w_in[:, o2:o3]).astype(BF16))
    expand = jnp.repeat(jnp.eye(SSM_HEADS, dtype=F32), SSM_HD, axis=1)
    expand = jnp.pad(expand, ((0, LANES - SSM_HEADS), (0, 0)))
    dsk = jnp.repeat(ssm_d[0].astype(F32), SSM_HD).reshape(1, -1)
    tok1 = _ssd(z.reshape(b, s, -1), xbc.reshape(b, s, -1), dt.reshape(b, s, -1),
                ssm_conv_w[0].astype(F32), row(ssm_conv_b[0]), _pad_lanes(row(ssm_dt_bias[0])),
                _pad_lanes(row(ssm_a_log[0])), dsk, row(ssm_norm_g[0]), expand)
    x3, h4, gates = _mixout(tok1.reshape(t, TOK_W), mq, 0, kv[1], mem_qg[1:2], wo[1], x2, row(ln2_g[1]),
                            wr=_pad_lanes(moe_w_router[0].astype(F32)))
    (x4,) = _ffn(h4, x3, moe_w_gate[0].astype(BF16), moe_w_up[0].astype(BF16), moe_w_down[0].astype(BF16),
                 gates=gates, tm=512, tf=512)
    return x4.reshape(b, s, d)
```

```python
import functools
import math

import jax
import jax.numpy as jnp
from jax import lax
from jax.experimental import pallas as pl
from jax.experimental.pallas import tpu as pltpu

F32 = jnp.float32
BF16 = jnp.bfloat16

D_MODEL = 1024
CHUNK = 64
MEM_LEN = 256
MEM_W = 256
MEM_HEADS = 4
MEM_HD = 64
TOK_W = 768
DA_DH = 64
DA_HEADS = 6
SSM_HD = 64
SSM_HEADS = 12
SSM_GROUPS = 2
SSM_STATE = 128
SSM_CONV_K = 4
SSM_CONV_DIM = 1280
N_EXPERTS = 8
EPS = 1e-6
LANES = 128
NEG = -1e30
SUBLANES = 8
ONES_ROWS = SUBLANES
LOG2E = math.log2(math.e)
VMEM_LIMIT = 56 * 1024 * 1024


def _cparams(sem):
    return pltpu.CompilerParams(dimension_semantics=sem, vmem_limit_bytes=VMEM_LIMIT)


def _rms(xf, g):
    ms = jnp.mean(xf * xf, axis=-1, keepdims=True)
    return xf * lax.rsqrt(ms + EPS) * g


def _seg_mean_matrix(n, seg_shift):
    r = lax.broadcasted_iota(jnp.int32, (n, n), 0) >> seg_shift
    c = lax.broadcasted_iota(jnp.int32, (n, n), 1) >> seg_shift
    return jnp.where(r == c, 1.0 / (1 << seg_shift), 0.0).astype(BF16)


def _seg_mean_sq(y, seg_shift=6):
    bd = _seg_mean_matrix(256, seg_shift)
    sq = (y * y).astype(BF16)
    parts = [jnp.dot(sq[:, c:c + 256], bd, preferred_element_type=F32)
             for c in range(0, y.shape[1], 256)]
    return parts[0] if len(parts) == 1 else jnp.concatenate(parts, axis=1)


def _inproj0_kernel(x_ref, g_ref, w_ref, qkg_ref, u_ref):
    h = _rms(x_ref[...], g_ref[...]).astype(BF16)
    u = jnp.dot(h, w_ref[...], preferred_element_type=F32)
    nqk = 2 * TOK_W
    qk = u[:, :nqk]
    qkn = qk * lax.rsqrt(_seg_mean_sq(qk) + EPS) * qkg_ref[...]
    u_ref[:, :nqk] = qkn.astype(BF16)
    u_ref[:, nqk:] = u[:, nqk:].astype(BF16)


def _inproj0(x, g, w, qkg, tm=512):
    t = x.shape[0]
    n = w.shape[1]
    return pl.pallas_call(
        _inproj0_kernel,
        out_shape=jax.ShapeDtypeStruct((t, n), BF16),
        grid=(t // tm,),
        in_specs=[pl.BlockSpec((tm, D_MODEL), lambda i: (i, 0)),
                  pl.BlockSpec((1, D_MODEL), lambda i: (0, 0)),
                  pl.BlockSpec((D_MODEL, n), lambda i: (0, 0)),
                  pl.BlockSpec((1, 2 * TOK_W), lambda i: (0, 0))],
        out_specs=pl.BlockSpec((tm, n), lambda i: (i, 0)),
        compiler_params=_cparams(("parallel",)),
        name="inproj0",
    )(x, g, w, qkg)


def _attn_kernel(lq1_ref, lk1_ref, lq2_ref, lk2_ref, subg_ref, qt_ref, k_ref, vt_ref, o_ref,
                 acc_ref, m_ref, alpha_ref, s_ref, p_ref, *, tq, tk, lambda_init):
    qi = pl.program_id(2)
    qt = qt_ref[0, 0]
    row = lax.broadcasted_iota(jnp.int32, qt.shape, 0)
    zero = jnp.zeros_like(qt)
    qs = (jnp.where(row < DA_DH, qt, zero), jnp.where(row >= DA_DH, qt, zero))
    hd = 2 * DA_DH
    acc_ref[...] = jnp.zeros_like(acc_ref)
    m_ref[...] = jnp.full_like(m_ref, NEG)
    p_ref[1] = jnp.zeros(p_ref.shape[1:], p_ref.dtype)
    alpha_ref[1] = jnp.ones(alpha_ref.shape[1:], alpha_ref.dtype)

    def scores(ki, slot):
        start = pl.multiple_of(ki * tk, tk)
        k = k_ref[0, pl.ds(start, tk), :]
        for i in range(2):
            s_ref[slot, i] = jnp.dot(k, qs[i], preferred_element_type=F32)

    def softmax(slot, masked):
        if masked:
            kc = lax.broadcasted_iota(jnp.int32, (tk, tq), 0) >> 6
            qc = lax.broadcasted_iota(jnp.int32, (tk, tq), 1) >> 6
            vis = kc <= qc
        for i in range(2):
            s = s_ref[slot, i]
            if masked:
                s = jnp.where(vis, s, NEG)
            m_old = m_ref[i]
            m_new = jnp.maximum(m_old, jnp.max(s, axis=0, keepdims=True))
            alpha_ref[slot, i] = jnp.exp2(m_old - m_new)
            p_ref[slot, i] = jnp.exp2(s - m_new).astype(BF16)
            m_ref[i] = m_new

    def values(ki, slot):
        vt = vt_ref[0, 0, ki]
        for i in range(2):
            acc_ref[i] = alpha_ref[slot, i] * acc_ref[i] + jnp.dot(vt, p_ref[slot, i],
                                                                    preferred_element_type=F32)

    odd = (qi & 1) == 1

    @pl.when(odd)
    def _():
        scores(0, 1)
        scores(1, 0)
        softmax(1, False)

    @pl.when(jnp.logical_not(odd))
    def _():
        scores(0, 0)

    def body(j, carry):
        ki = (qi & 1) + 2 * j
        scores(ki + 1, 1)
        softmax(0, False)
        values(jnp.maximum(ki - 1, 0), 1)
        scores(ki + 2, 0)
        softmax(1, False)
        values(ki, 0)
        return carry

    lax.fori_loop(0, qi >> 1, body, 0)
    softmax(0, True)
    values(jnp.maximum(qi - 1, 0), 1)
    values(qi, 0)

    lam = (jnp.exp(jnp.sum(lq1_ref[...] * lk1_ref[...])) - jnp.exp(jnp.sum(lq2_ref[...] * lk2_ref[...]))
           + lambda_init)
    o1 = acc_ref[0, :hd, :] / acc_ref[0, hd:hd + 1, :]
    o2 = acc_ref[1, :hd, :] / acc_ref[1, hd:hd + 1, :]
    ot = o1 - lam * o2
    ms = jnp.mean(ot * ot, axis=0, keepdims=True)
    ot = ot * lax.rsqrt(ms + EPS)
    o_ref[0] = (ot.T * (subg_ref[...] * (1.0 - lambda_init))).astype(o_ref.dtype)


def _diff_attention(u3, lq1, lk1, lq2, lk2, subg, lambda_init, tq=512):
    b, s, _ = u3.shape
    tk = tq
    nq = s // tq
    hd = 2 * DA_DH
    qt = u3[:, :, :TOK_W].reshape(b, s, DA_HEADS, hd).transpose(0, 2, 3, 1)
    vt = u3[:, :, 2 * TOK_W:3 * TOK_W].reshape(b, s // tk, tk, DA_HEADS, hd)
    vt = vt.transpose(0, 3, 1, 4, 2)
    vt = jnp.concatenate([vt, jnp.ones(vt.shape[:3] + (ONES_ROWS, tk), vt.dtype)], axis=3)
    hde = hd + ONES_ROWS
    vec = lambda n: pl.BlockSpec((1, n), lambda bi, h, qi: (0, 0))
    kern = functools.partial(_attn_kernel, tq=tq, tk=tk, lambda_init=lambda_init)
    return pl.pallas_call(
        kern,
        out_shape=jax.ShapeDtypeStruct((b, s, TOK_W), BF16),
        grid=(b, DA_HEADS, nq),
        in_specs=[vec(DA_DH), vec(DA_DH), vec(DA_DH), vec(DA_DH), vec(hd),
                  pl.BlockSpec((1, 1, hd, tq), lambda bi, h, qi: (bi, h, 0, qi)),
                  pl.BlockSpec((1, s, hd), lambda bi, h, qi: (bi, 0, DA_HEADS + h)),
                  pl.BlockSpec((1, 1, s // tk, hde, tk), lambda bi, h, qi: (bi, h, 0, 0, 0))],
        out_specs=pl.BlockSpec((1, tq, hd), lambda bi, h, qi: (bi, qi, h)),
        scratch_shapes=[pltpu.VMEM((2, hde, tq), F32), pltpu.VMEM((2, 1, tq), F32),
                        pltpu.VMEM((2, 2, 1, tq), F32), pltpu.VMEM((2, 2, tk, tq), F32),
                        pltpu.VMEM((2, 2, tk, tq), BF16)],
        compiler_params=_cparams(("parallel", "parallel", "parallel")),
        name="diff_attn",
    )(lq1, lk1, lq2, lk2, subg, qt, u3, vt)


def _memkv_kernel(mem_ref, g_ref, wkv_ref, kng_ref, kv_ref):
    mn = _rms(mem_ref[0], g_ref[...]).astype(BF16)
    kv = jnp.dot(mn, wkv_ref[0], preferred_element_type=F32)
    k = kv[:, :MEM_W]
    kn = k * lax.rsqrt(_seg_mean_sq(k) + EPS) * kng_ref[0]
    kv_ref[0, 0, :, :MEM_W] = kn.astype(BF16)
    kv_ref[0, 0, :, MEM_W:] = kv[:, MEM_W:].astype(BF16)


def _memkv(mem, g, wkv, kng):
    depth = wkv.shape[0]
    b = mem.shape[0]
    return pl.pallas_call(
        _memkv_kernel,
        out_shape=jax.ShapeDtypeStruct((depth, b, MEM_LEN, 2 * MEM_W), BF16),
        grid=(depth, b),
        in_specs=[pl.BlockSpec((1, MEM_LEN, D_MODEL), lambda d, bi: (bi, 0, 0)),
                  pl.BlockSpec((1, D_MODEL), lambda d, bi: (0, 0)),
                  pl.BlockSpec((1, D_MODEL, 2 * MEM_W), lambda d, bi: (d, 0, 0)),
                  pl.BlockSpec((1, 1, MEM_W), lambda d, bi: (d, 0, 0))],
        out_specs=pl.BlockSpec((1, 1, MEM_LEN, 2 * MEM_W), lambda d, bi: (d, bi, 0, 0)),
        compiler_params=_cparams(("parallel", "parallel")),
        name="memkv",
    )(mem, g, wkv, kng)


def _mem_attention(mq, qg, kv):
    qn = (mq * lax.rsqrt(_seg_mean_sq(mq) + EPS) * qg).astype(BF16)
    k = kv[:, :MEM_W]
    v = kv[:, MEM_W:]
    lane = lax.broadcasted_iota(jnp.int32, qn.shape, 1) >> 6
    out = jnp.zeros(qn.shape, F32)
    for h in range(MEM_HEADS):
        sel = lane == h
        qh = jnp.where(sel, qn, jnp.zeros_like(qn))
        s = lax.dot_general(qh, k, (((1,), (1,)), ((), ())), preferred_element_type=F32)
        p = jnp.exp(s - jnp.max(s, axis=1, keepdims=True))
        l = jnp.sum(p, axis=1, keepdims=True)
        o = jnp.dot(p.astype(BF16), v, preferred_element_type=F32)
        out = jnp.where(sel, o / l, out)
    return out


R_E1, R_E2, R_G1, R_G2, R_RANK1, R_RANK2 = range(6)


def _top2_route(logits, count_ref):
    tm = logits.shape[0]
    lane = lax.broadcasted_iota(jnp.int32, logits.shape, 1)
    lg = jnp.where(lane < N_EXPERTS, logits, NEG)
    m1 = jnp.max(lg, axis=1, keepdims=True)
    i1 = jnp.min(jnp.where(lg == m1, lane, LANES), axis=1, keepdims=True)
    lg2 = jnp.where(lane == i1, NEG, lg)
    m2 = jnp.max(lg2, axis=1, keepdims=True)
    i2 = jnp.min(jnp.where(lg2 == m2, lane, LANES), axis=1, keepdims=True)
    g1 = 1.0 / (1.0 + jnp.exp(m2 - m1))
    g2 = 1.0 - g1
    chosen = jnp.logical_or(lane == i1, lane == i2)
    onehot = jnp.where(chosen, 1.0, 0.0)
    r = lax.broadcasted_iota(jnp.int32, (tm, tm), 0)
    c = lax.broadcasted_iota(jnp.int32, (tm, tm), 1)
    before = jnp.where(c < r, 1.0, 0.0).astype(BF16)
    prefix = jnp.dot(before, onehot.astype(BF16), preferred_element_type=F32) + count_ref[...]
    count_ref[...] += jnp.sum(onehot, axis=0, keepdims=True)
    rank1 = jnp.sum(jnp.where(lane == i1, prefix, 0.0), axis=1, keepdims=True)
    rank2 = jnp.sum(jnp.where(lane == i2, prefix, 0.0), axis=1, keepdims=True)
    rec = jnp.zeros(logits.shape, F32)
    for idx, val in ((R_E1, i1.astype(F32)), (R_E2, i2.astype(F32)), (R_G1, g1), (R_G2, g2),
                     (R_RANK1, rank1), (R_RANK2, rank2)):
        rec = jnp.where(lane == idx, val, rec)
    return rec


def _pack_bf16_pairs(h):
    w = h.shape[1] // 2
    bits = pltpu.bitcast(h.astype(BF16).astype(F32), jnp.uint32)
    return (bits[:, :w] >> 16) | (bits[:, w:] & jnp.uint32(0xFFFF0000))


def _unpack_bf16_pairs(p):
    lo = pltpu.bitcast(p << 16, F32)
    hi = pltpu.bitcast(p & jnp.uint32(0xFFFF0000), F32)
    return jnp.concatenate([lo, hi], axis=1).astype(BF16)


def _mixout_kernel(*refs, with_router):
    if with_router:
        (tok_ref, mq_ref, kv_ref, qg_ref, wo_ref, x_ref, g2_ref, wr_ref,
         x1_ref, hp_ref, route_ref, count_ref) = refs
    else:
        tok_ref, mq_ref, kv_ref, qg_ref, wo_ref, x_ref, g2_ref, x1_ref, h2_ref = refs
    mo = _mem_attention(mq_ref[...].astype(F32), qg_ref[...], kv_ref[0])
    y = jnp.dot(tok_ref[...], wo_ref[:TOK_W, :], preferred_element_type=F32)
    y = y + jnp.dot(mo.astype(BF16), wo_ref[TOK_W:, :], preferred_element_type=F32)
    x1 = x_ref[...] + y
    x1_ref[...] = x1
    h2 = _rms(x1, g2_ref[...])
    if with_router:
        @pl.when(pl.program_id(0) == 0)
        def _():
            count_ref[...] = jnp.zeros_like(count_ref)

        hp_ref[...] = _pack_bf16_pairs(h2)
        logits = jnp.dot(h2, wr_ref[...], preferred_element_type=F32, precision=lax.Precision.HIGHEST)
        route_ref[...] = _top2_route(logits, count_ref)
    else:
        h2_ref[...] = h2.astype(BF16)


def _mixout(tok, mq_src, mq_block, kv, qg, wo, x, g2, wr=None, tm=512):
    t = x.shape[0]
    tiles_per_batch = t // kv.shape[0] // tm
    with_router = wr is not None
    in_specs = [pl.BlockSpec((tm, TOK_W), lambda i: (i, 0)),
                pl.BlockSpec((tm, MEM_W), lambda i: (i, mq_block)),
                pl.BlockSpec((1, MEM_LEN, 2 * MEM_W), lambda i: (i // tiles_per_batch, 0, 0)),
                pl.BlockSpec((1, MEM_W), lambda i: (0, 0)),
                pl.BlockSpec((D_MODEL, D_MODEL), lambda i: (0, 0)),
                pl.BlockSpec((tm, D_MODEL), lambda i: (i, 0)),
                pl.BlockSpec((1, D_MODEL), lambda i: (0, 0))]
    args = [tok, mq_src, kv, qg, wo, x, g2]
    row_spec = lambda n: pl.BlockSpec((tm, n), lambda i: (i, 0))
    if with_router:
        in_specs.append(pl.BlockSpec((D_MODEL, LANES), lambda i: (0, 0)))
        args.append(wr)
        out_shape = [jax.ShapeDtypeStruct((t, D_MODEL), F32), jax.ShapeDtypeStruct((t, D_MODEL // 2), jnp.uint32),
                     jax.ShapeDtypeStruct((t, LANES), F32), jax.ShapeDtypeStruct((1, LANES), F32)]
        out_specs = [row_spec(D_MODEL), row_spec(D_MODEL // 2), row_spec(LANES),
                     pl.BlockSpec((1, LANES), lambda i: (0, 0))]
    else:
        out_shape = [jax.ShapeDtypeStruct((t, D_MODEL), F32), jax.ShapeDtypeStruct((t, D_MODEL), BF16)]
        out_specs = [row_spec(D_MODEL), row_spec(D_MODEL)]
    return pl.pallas_call(
        functools.partial(_mixout_kernel, with_router=with_router),
        out_shape=out_shape,
        grid=(t // tm,),
        in_specs=in_specs,
        out_specs=out_specs,
        compiler_params=_cparams(("arbitrary",) if with_router else ("parallel",)),
        name="mixout_router" if with_router else "mixout",
    )(*args)


def _ffn_kernel(h_ref, x_ref, wg_ref, wu_ref, wd_ref, gn_ref, xo_ref, ho_ref, acc_ref):
    f = pl.program_id(1)

    @pl.when(f == 0)
    def _():
        acc_ref[...] = x_ref[...]

    h = h_ref[...]
    a = jnp.dot(h, wg_ref[...], preferred_element_type=F32)
    u = jnp.dot(h, wu_ref[...], preferred_element_type=F32)
    act = a * jax.nn.sigmoid(a) * u
    acc_ref[...] += jnp.dot(act.astype(BF16), wd_ref[...], preferred_element_type=F32)

    @pl.when(f == pl.num_programs(1) - 1)
    def _():
        xn = acc_ref[...]
        xo_ref[...] = xn
        ho_ref[...] = _rms(xn, gn_ref[...]).astype(BF16)


def _ffn(h, x, wg, wu, wd, gn, tm=512, tf=256):
    t = x.shape[0]
    fdim = wg.shape[1]
    row = pl.BlockSpec((tm, D_MODEL), lambda i, f: (i, 0))
    return pl.pallas_call(
        _ffn_kernel,
        out_shape=[jax.ShapeDtypeStruct((t, D_MODEL), F32), jax.ShapeDtypeStruct((t, D_MODEL), BF16)],
        grid=(t // tm, fdim // tf),
        in_specs=[row, row,
                  pl.BlockSpec((D_MODEL, tf), lambda i, f: (0, f)),
                  pl.BlockSpec((D_MODEL, tf), lambda i, f: (0, f)),
                  pl.BlockSpec((tf, D_MODEL), lambda i, f: (f, 0)),
                  pl.BlockSpec((1, D_MODEL), lambda i, f: (0, 0))],
        out_specs=[row, row],
        scratch_shapes=[pltpu.VMEM((tm, D_MODEL), F32)],
        compiler_params=_cparams(("parallel", "arbitrary")),
        name="ffn",
    )(h, x, wg, wu, wd, gn)


def _row_copy(src_ref, src_row, dst_ref, dst_row, sem):
    return pltpu.make_async_copy(src_ref.at[pl.ds(src_row, 1), :], dst_ref.at[pl.ds(dst_row, 1), :], sem)


def _dispatch_kernel(pad_start_ref, pad_len_ref, pos_ref, hp_ref, xs_ref, zeros_ref, sem, zsem, *, tm, tmr):
    nbits = tmr.bit_length() - 1

    @pl.when(pl.program_id(0) == 0)
    def _():
        zeros_ref[...] = jnp.zeros_like(zeros_ref)
        for e in range(N_EXPERTS):
            start = pad_start_ref[e]
            length = pad_len_ref[e]
            singles = length & (SUBLANES - 1)
            for j in range(SUBLANES - 1):
                @pl.when(j < singles)
                def _(j=j, start=start):
                    cp = _row_copy(zeros_ref, 0, xs_ref, start + j, zsem)
                    cp.start()
                    cp.wait()

            done = start + singles
            for b in range(SUBLANES.bit_length() - 1, nbits):
                n = 1 << b
                bit = (length >> b) & 1

                @pl.when(bit == 1)
                def _(n=n, done=done):
                    cp = pltpu.make_async_copy(zeros_ref.at[pl.ds(0, n), :],
                                               xs_ref.at[pl.ds(pl.multiple_of(done, SUBLANES), n), :], zsem)
                    cp.start()
                    cp.wait()

                done = done + bit * n

        n_tiles = xs_ref.shape[0] // tmr
        half = tmr // 2
        for j in range(n_tiles - N_EXPERTS, n_tiles):
            @pl.when(j * tmr >= pad_start_ref[N_EXPERTS])
            def _(j=j):
                for c in range(2):
                    cp = pltpu.make_async_copy(zeros_ref, xs_ref.at[pl.ds(j * tmr + c * half, half), :], zsem)
                    cp.start()
                    cp.wait()

    def issue(r, carry):
        _row_copy(hp_ref, r, xs_ref, pos_ref[0, 0, r], sem).start()
        _row_copy(hp_ref, r, xs_ref, pos_ref[0, 0, tm + r], sem).start()
        return carry

    lax.fori_loop(0, tm, issue, 0)
    for _ in range(2):
        pltpu.make_async_copy(hp_ref, xs_ref.at[pl.ds(0, tm), :], sem).wait()


def _dispatch(pad_start, pad_len, pos, hp, n_rows, tm, tmr):
    t, w = hp.shape
    return pl.pallas_call(
        functools.partial(_dispatch_kernel, tm=tm, tmr=tmr),
        out_shape=jax.ShapeDtypeStruct((n_rows, w), hp.dtype),
        grid_spec=pltpu.PrefetchScalarGridSpec(
            num_scalar_prefetch=2,
            grid=(t // tm,),
            in_specs=[pl.BlockSpec((1, 1, 2 * tm), lambda i, ps, pn: (i, 0, 0), memory_space=pltpu.SMEM),
                      pl.BlockSpec((tm, w), lambda i, ps, pn: (i, 0))],
            out_specs=pl.BlockSpec(memory_space=pl.ANY),
            scratch_shapes=[pltpu.VMEM((tmr // 2, w), hp.dtype), pltpu.SemaphoreType.DMA,
                            pltpu.SemaphoreType.DMA]),
        compiler_params=_cparams(("arbitrary",)),
        name="moe_dispatch",
    )(pad_start, pad_len, pos, hp)


def _experts_kernel(te_ref, tv_ref, xs_ref, wg_ref, wu_ref, wd_ref, y_ref, h_ref):
    i = pl.program_id(0)
    f = pl.program_id(1)

    @pl.when(tv_ref[i] == 1)
    def _():
        @pl.when(f == 0)
        def _():
            h_ref[...] = _unpack_bf16_pairs(xs_ref[...])

        h = h_ref[...]
        a = jnp.dot(h, wg_ref[0], preferred_element_type=F32)
        u = jnp.dot(h, wu_ref[0], preferred_element_type=F32)
        act = a * jax.nn.sigmoid(a) * u
        contrib = jnp.dot(act.astype(BF16), wd_ref[0], preferred_element_type=F32)

        @pl.when(f == 0)
        def _():
            y_ref[...] = contrib

        @pl.when(f != 0)
        def _():
            y_ref[...] += contrib

    @pl.when(jnp.logical_and(tv_ref[i] == 0, f == 0))
    def _():
        y_ref[...] = jnp.zeros_like(y_ref)


def _experts(tile_expert, tile_valid, xs, wg, wu, wd, tmr, tf):
    n_rows, w = xs.shape
    fdim = wg.shape[2]
    nf = fdim // tf
    fidx = lambda i, f, te, tv: jnp.where(tv[i] == 1, f, nf - 1)
    return pl.pallas_call(
        _experts_kernel,
        out_shape=jax.ShapeDtypeStruct((n_rows, D_MODEL), F32),
        grid_spec=pltpu.PrefetchScalarGridSpec(
            num_scalar_prefetch=2,
            grid=(n_rows // tmr, nf),
            in_specs=[pl.BlockSpec((tmr, w), lambda i, f, te, tv: (jnp.where(tv[i] == 1, i, 0), 0)),
                      pl.BlockSpec((1, D_MODEL, tf), lambda i, f, te, tv: (te[i], 0, fidx(i, f, te, tv))),
                      pl.BlockSpec((1, D_MODEL, tf), lambda i, f, te, tv: (te[i], 0, fidx(i, f, te, tv))),
                      pl.BlockSpec((1, tf, D_MODEL), lambda i, f, te, tv: (te[i], fidx(i, f, te, tv), 0))],
            out_specs=pl.BlockSpec((tmr, D_MODEL), lambda i, f, te, tv: (i, 0)),
            scratch_shapes=[pltpu.VMEM((tmr, D_MODEL), BF16)]),
        compiler_params=_cparams(("parallel", "arbitrary")),
        name="moe_experts",
    )(tile_expert, tile_valid, xs, wg, wu, wd)


def _combine_kernel(pos_ref, route_ref, x_ref, y_ref, o_ref, buf_ref, sem, *, tm):
    def issue(r, carry):
        _row_copy(y_ref, pos_ref[0, 0, r], buf_ref.at[0], r, sem).start()
        _row_copy(y_ref, pos_ref[0, 0, tm + r], buf_ref.at[1], r, sem).start()
        return carry

    lax.fori_loop(0, tm, issue, 0)
    for k in range(2):
        pltpu.make_async_copy(y_ref.at[pl.ds(0, tm), :], buf_ref.at[k], sem).wait()
    route = route_ref[...]
    g1 = route[:, R_G1:R_G1 + 1]
    g2 = route[:, R_G2:R_G2 + 1]
    o_ref[...] = x_ref[...] + (g1 * buf_ref[0] + g2 * buf_ref[1])


def _combine(pos, route, x, y, tm):
    t = x.shape[0]
    return pl.pallas_call(
        functools.partial(_combine_kernel, tm=tm),
        out_shape=jax.ShapeDtypeStruct((t, D_MODEL), F32),
        grid=(t // tm,),
        in_specs=[pl.BlockSpec((1, 1, 2 * tm), lambda i: (i, 0, 0), memory_space=pltpu.SMEM),
                  pl.BlockSpec((tm, LANES), lambda i: (i, 0)),
                  pl.BlockSpec((tm, D_MODEL), lambda i: (i, 0)),
                  pl.BlockSpec(memory_space=pl.ANY)],
        out_specs=pl.BlockSpec((tm, D_MODEL), lambda i: (i, 0)),
        scratch_shapes=[pltpu.VMEM((2, tm, D_MODEL), F32), pltpu.SemaphoreType.DMA],
        compiler_params=_cparams(("arbitrary",)),
        name="moe_combine",
    )(pos, route, x, y)


def _moe(hp, route, counts, x, wg, wu, wd, tm=512, tmr=512, tf=512):
    t = hp.shape[0]
    n_tiles = 2 * t // tmr + N_EXPERTS
    n_rows = n_tiles * tmr
    cnt = counts[0, :N_EXPERTS].astype(jnp.int32)
    padded = (cnt + tmr - 1) // tmr * tmr
    ends = jnp.cumsum(padded)
    offs = ends - padded
    e1 = route[:, R_E1].astype(jnp.int32)
    e2 = route[:, R_E2].astype(jnp.int32)
    pos1 = offs[e1] + route[:, R_RANK1].astype(jnp.int32)
    pos2 = offs[e2] + route[:, R_RANK2].astype(jnp.int32)
    pos = jnp.concatenate([pos1.reshape(t // tm, 1, tm), pos2.reshape(t // tm, 1, tm)], axis=2)
    tile_start = jnp.arange(n_tiles, dtype=jnp.int32) * tmr
    tile_valid = (tile_start < ends[-1]).astype(jnp.int32)
    last_tile = ends[-1] // tmr - 1
    tile_expert = jnp.searchsorted(ends, jnp.minimum(tile_start, last_tile * tmr), side="right").astype(jnp.int32)
    pad_start = jnp.concatenate([offs + cnt, ends[-1:]])
    xs = _dispatch(pad_start, padded - cnt, pos, hp, n_rows, tm, tmr)
    y = _experts(tile_expert, tile_valid, xs, wg, wu, wd, tmr, tf)
    return _combine(pos, route, x, y, tm)


def _inproj1_kernel(x_ref, wz_ref, wx_ref, wm_ref, wdt_ref, z_ref, xbc_ref, mq_ref, dt_ref):
    h = x_ref[...]
    z_ref[...] = jnp.dot(h, wz_ref[...], preferred_element_type=F32).astype(BF16)
    xbc_ref[...] = jnp.dot(h, wx_ref[...], preferred_element_type=F32).astype(BF16)
    mq_ref[...] = jnp.dot(h, wm_ref[...], preferred_element_type=F32).astype(BF16)
    dt_ref[...] = jnp.dot(h, wdt_ref[...], preferred_element_type=F32)


def _inproj1(h, wz, wx, wm, wdt, tm=512):
    t = h.shape[0]
    full = lambda a: pl.BlockSpec(a.shape, lambda i: (0, 0))
    row = lambda n: pl.BlockSpec((tm, n), lambda i: (i, 0))
    return pl.pallas_call(
        _inproj1_kernel,
        out_shape=[jax.ShapeDtypeStruct((t, TOK_W), BF16), jax.ShapeDtypeStruct((t, SSM_CONV_DIM), BF16),
                   jax.ShapeDtypeStruct((t, MEM_W), BF16), jax.ShapeDtypeStruct((t, LANES), F32)],
        grid=(t // tm,),
        in_specs=[row(D_MODEL), full(wz), full(wx), full(wm), full(wdt)],
        out_specs=[row(TOK_W), row(SSM_CONV_DIM), row(MEM_W), row(LANES)],
        compiler_params=_cparams(("parallel",)),
        name="inproj1",
    )(h, wz, wx, wm, wdt)


def _ssd_kernel(z_ref, xbc_ref, dt_ref, cw_ref, cb_ref, dtb_ref, alog_ref, dsk_ref, ng_ref, ex_ref,
                o_ref, xe_ref, st_ref, *, tl):
    j = pl.program_id(1)
    pad = 8

    @pl.when(j == 0)
    def _():
        xe_ref[0:pad, :] = jnp.zeros((pad, SSM_CONV_DIM), F32)
        st_ref[...] = jnp.zeros_like(st_ref)

    x = xbc_ref[0].astype(F32)
    xe_ref[pad:pad + tl, :] = x
    acc = cb_ref[...] + cw_ref[SSM_CONV_K - 1:SSM_CONV_K, :] * x
    for k in range(SSM_CONV_K - 1):
        sh = SSM_CONV_K - 1 - k
        acc = acc + cw_ref[k:k + 1, :] * xe_ref[pad - sh:pad - sh + tl, :]
    xe_ref[0:pad, :] = x[tl - pad:tl, :]
    xc = acc * jax.nn.sigmoid(acc)
    xs = xc[:, :TOK_W]

    dt_in = dt_ref[0] + dtb_ref[...]
    dt = jnp.maximum(dt_in, 0.0) + jnp.log1p(jnp.exp(-jnp.abs(dt_in)))
    a = -jnp.exp(alog_ref[...])
    dta = dt * a
    r = lax.broadcasted_iota(jnp.int32, (tl, tl), 0)
    c = lax.broadcasted_iota(jnp.int32, (tl, tl), 1)
    tril = r >= c
    ltri = jnp.where(tril, 1.0, 0.0).astype(F32)
    cum = jnp.dot(ltri, dta, preferred_element_type=F32, precision=lax.Precision.HIGHEST)
    cum_t = cum.T
    dt_t = dt.T
    lane = lax.broadcasted_iota(jnp.int32, (tl, LANES), 1)

    hg = SSM_HEADS // SSM_GROUPS
    y_pairs = []
    cbs = []
    for g in range(SSM_GROUPS):
        bm = xc[:, TOK_W + g * SSM_STATE:TOK_W + (g + 1) * SSM_STATE].astype(BF16)
        cm = xc[:, TOK_W + (SSM_GROUPS + g) * SSM_STATE:TOK_W + (SSM_GROUPS + g + 1) * SSM_STATE].astype(BF16)
        cbs.append(lax.dot_general(cm, bm, (((1,), (1,)), ((), ())), preferred_element_type=F32))
    for pr in range(SSM_HEADS // 2):
        xp = xs[:, pr * LANES:(pr + 1) * LANES].astype(BF16)
        ys = []
        for hh in (2 * pr, 2 * pr + 1):
            g = hh // hg
            seg = cum[:, hh:hh + 1] - cum_t[hh:hh + 1, :]
            decay = jnp.exp(jnp.where(tril, seg, NEG))
            w = cbs[g] * decay * dt_t[hh:hh + 1, :]
            ys.append(jnp.dot(w.astype(BF16), xp, preferred_element_type=F32))
        y_pairs.append(jnp.where(lane < SSM_HD, ys[0], ys[1]))
    y = jnp.concatenate(y_pairs, axis=1)

    expcum = jnp.exp(cum)
    to_end = jnp.exp(cum[tl - 1:tl, :] - cum) * dt
    stacked = jnp.concatenate([expcum, to_end], axis=0)
    exd = jnp.dot(stacked, ex_ref[...], preferred_element_type=F32, precision=lax.Precision.HIGHEST)
    expcum_x = exd[:tl]
    xw = (xs * exd[tl:]).astype(BF16)
    gw = TOK_W // SSM_GROUPS
    y_off = []
    for g in range(SSM_GROUPS):
        cs = slice(g * gw, (g + 1) * gw)
        bm_t = xc[:, TOK_W + g * SSM_STATE:TOK_W + (g + 1) * SSM_STATE].T.astype(BF16)
        cm = xc[:, TOK_W + (SSM_GROUPS + g) * SSM_STATE:TOK_W + (SSM_GROUPS + g + 1) * SSM_STATE].astype(BF16)
        sg = st_ref[:, cs]
        y_off.append(jnp.dot(cm, sg.astype(BF16), preferred_element_type=F32) * expcum_x[:, cs])
        st_ref[:, cs] = sg * expcum_x[tl - 1:tl, cs] + jnp.dot(bm_t, xw[:, cs], preferred_element_type=F32)
    y = y + jnp.concatenate(y_off, axis=1) + dsk_ref[...] * xs
    zf = z_ref[0].astype(F32)
    y = y * (zf * jax.nn.sigmoid(zf))
    outs = []
    for g in range(SSM_GROUPS):
        cs = slice(g * gw, (g + 1) * gw)
        outs.append(_rms(y[:, cs], ng_ref[:, cs]))
    o_ref[0] = jnp.concatenate(outs, axis=1).astype(o_ref.dtype)


def _ssd(z3, xbc3, dt3, cw, cb, dtb, alog, dsk, ng, ex, tl=256):
    b, s, _ = z3.shape
    full = lambda a: pl.BlockSpec(a.shape, lambda bi, j: (0, 0))
    blk = lambda n: pl.BlockSpec((1, tl, n), lambda bi, j: (bi, j, 0))
    return pl.pallas_call(
        functools.partial(_ssd_kernel, tl=tl),
        out_shape=jax.ShapeDtypeStruct((b, s, TOK_W), BF16),
        grid=(b, s // tl),
        in_specs=[blk(TOK_W), blk(SSM_CONV_DIM), blk(LANES), full(cw), full(cb), full(dtb), full(alog),
                  full(dsk), full(ng), full(ex)],
        out_specs=blk(TOK_W),
        scratch_shapes=[pltpu.VMEM((tl + 8, SSM_CONV_DIM), F32), pltpu.VMEM((SSM_STATE, TOK_W), F32)],
        compiler_params=_cparams(("parallel", "arbitrary")),
        name="conv_ssd",
    )(z3, xbc3, dt3, cw, cb, dtb, alog, dsk, ng, ex)


def _pad_lanes(v, n=LANES):
    return jnp.pad(v, [(0, 0)] * (v.ndim - 1) + [(0, n - v.shape[-1])])


def kernel(x, mem, ln1_g, ln2_g, mem_norm_g, w_out, mem_w_kv, mem_qn_g, mem_kn_g, da_w_in, da_qn_g, da_kn_g,
           da_lq1, da_lk1, da_lq2, da_lk2, da_sub_g, ssm_w_in, ssm_conv_w, ssm_conv_b, ssm_dt_bias, ssm_a_log,
           ssm_d, ssm_norm_g, ffn_w_gate, ffn_w_up, ffn_w_down, moe_w_router, moe_w_gate, moe_w_up, moe_w_down):
    b, s, d = x.shape
    t = b * s
    row = lambda v: v.reshape(1, -1).astype(F32)
    xt = x.reshape(t, d)

    mem_qg = jnp.tile(mem_qn_g.astype(F32) * (MEM_HD ** -0.5), (1, MEM_HEADS))
    mem_kg = jnp.tile(mem_kn_g.astype(F32), (1, MEM_HEADS))[:, None, :]
    kv = _memkv(mem, row(mem_norm_g), mem_w_kv.astype(BF16), mem_kg)
    wo = w_out.astype(BF16)

    lambda_init = 0.8 - 0.6 * math.exp(-0.3 * 0)
    qkg = jnp.concatenate([jnp.tile(da_qn_g[0].astype(F32) * (DA_DH ** -0.5 * LOG2E), 2 * DA_HEADS),
                           jnp.tile(da_kn_g[0].astype(F32), 2 * DA_HEADS)]).reshape(1, -1)
    u = _inproj0(xt, row(ln1_g[0]), da_w_in[0].astype(BF16), qkg)
    tok = _diff_attention(u.reshape(b, s, -1), row(da_lq1[0]), row(da_lk1[0]), row(da_lq2[0]), row(da_lk2[0]),
                          row(da_sub_g[0]), lambda_init)
    x1, h2 = _mixout(tok.reshape(t, TOK_W), u, 3 * TOK_W // MEM_W, kv[0], mem_qg[0:1], wo[0], xt, row(ln2_g[0]))
    x2, h3 = _ffn(h2, x1, ffn_w_gate[0].astype(BF16), ffn_w_up[0].astype(BF16), ffn_w_down[0].astype(BF16),
                  row(ln1_g[1]))

    w_in = ssm_w_in[0]
    o1 = TOK_W
    o2 = o1 + SSM_CONV_DIM
    o3 = o2 + SSM_HEADS
    z, xbc, mq, dt = _inproj1(h3, w_in[:, :o1].astype(BF16), w_in[:, o1:o2].astype(BF16),
                              w_in[:, o3:].astype(BF16), _pad_lanes(w_in[:, o2:o3]).astype(BF16))
    expand = jnp.repeat(jnp.eye(SSM_HEADS, dtype=F32), SSM_HD, axis=1)
    expand = jnp.pad(expand, ((0, LANES - SSM_HEADS), (0, 0)))
    dsk = jnp.repeat(ssm_d[0].astype(F32), SSM_HD).reshape(1, -1)
    tok1 = _ssd(z.reshape(b, s, -1), xbc.reshape(b, s, -1), dt.reshape(b, s, -1),
                ssm_conv_w[0].astype(F32), row(ssm_conv_b[0]), _pad_lanes(row(ssm_dt_bias[0])),
                _pad_lanes(row(ssm_a_log[0])), dsk, row(ssm_norm_g[0]), expand)
    x3, hp, route, counts = _mixout(tok1.reshape(t, TOK_W), mq, 0, kv[1], mem_qg[1:2], wo[1], x2, row(ln2_g[1]),
                                    wr=_pad_lanes(moe_w_router[0].astype(F32)))
    x4 = _moe(hp, route, counts, x3, moe_w_gate[0].astype(BF16), moe_w_up[0].astype(BF16),
              moe_w_down[0].astype(BF16))
    return x4.reshape(b, s, d)
```

```python
import functools
import math

import jax
import jax.numpy as jnp
from jax import lax
from jax.experimental import pallas as pl
from jax.experimental.pallas import tpu as pltpu

F32 = jnp.float32
BF16 = jnp.bfloat16

D_MODEL = 1024
CHUNK = 64
MEM_LEN = 256
MEM_W = 256
MEM_HEADS = 4
MEM_HD = 64
TOK_W = 768
DA_DH = 64
DA_HEADS = 6
SSM_HD = 64
SSM_HEADS = 12
SSM_GROUPS = 2
SSM_STATE = 128
SSM_CONV_K = 4
SSM_CONV_DIM = 1280
N_EXPERTS = 8
EPS = 1e-6
LANES = 128
NEG = -1e30
SUBLANES = 8
ONES_ROWS = SUBLANES
LOG2E = math.log2(math.e)
ISSUE_UNROLL = 8
VMEM_LIMIT = 56 * 1024 * 1024


def _cparams(sem):
    return pltpu.CompilerParams(dimension_semantics=sem, vmem_limit_bytes=VMEM_LIMIT)


def _rms(xf, g):
    ms = jnp.mean(xf * xf, axis=-1, keepdims=True)
    return xf * lax.rsqrt(ms + EPS) * g


def _seg_mean_matrix(n, seg_shift):
    r = lax.broadcasted_iota(jnp.int32, (n, n), 0) >> seg_shift
    c = lax.broadcasted_iota(jnp.int32, (n, n), 1) >> seg_shift
    return jnp.where(r == c, 1.0 / (1 << seg_shift), 0.0).astype(BF16)


def _seg_mean_sq(y, seg_shift=6):
    bd = _seg_mean_matrix(256, seg_shift)
    sq = (y * y).astype(BF16)
    parts = [jnp.dot(sq[:, c:c + 256], bd, preferred_element_type=F32)
             for c in range(0, y.shape[1], 256)]
    return parts[0] if len(parts) == 1 else jnp.concatenate(parts, axis=1)


def _inproj0_kernel(x_ref, g_ref, w_ref, qkg_ref, qt_ref, k_ref, vt_ref, mq_ref):
    h = _rms(x_ref[...], g_ref[...]).astype(BF16)
    u = jnp.dot(h, w_ref[...], preferred_element_type=F32)
    tm = u.shape[0]
    hd = 2 * DA_DH
    nqk = 2 * TOK_W
    qk = u[:, :nqk]
    qkn = qk * lax.rsqrt(_seg_mean_sq(qk) + EPS) * qkg_ref[...]
    qt_ref[0] = qkn[:, :TOK_W].T.reshape(DA_HEADS, hd, tm).astype(BF16)
    k_ref[...] = qkn[:, TOK_W:].astype(BF16)
    vt_ref[0, :, 0, :hd, :] = u[:, nqk:nqk + TOK_W].T.reshape(DA_HEADS, hd, tm).astype(BF16)
    vt_ref[0, :, 0, hd:, :] = jnp.ones((DA_HEADS, ONES_ROWS, tm), BF16)
    mq_ref[...] = u[:, nqk + TOK_W:].astype(BF16)


def _inproj0(x, g, w, qkg, b, tm=512):
    t = x.shape[0]
    n = w.shape[1]
    s = t // b
    nt = s // tm
    hd = 2 * DA_DH
    return pl.pallas_call(
        _inproj0_kernel,
        out_shape=[jax.ShapeDtypeStruct((b, DA_HEADS, hd, s), BF16),
                   jax.ShapeDtypeStruct((t, TOK_W), BF16),
                   jax.ShapeDtypeStruct((b, DA_HEADS, nt, hd + ONES_ROWS, tm), BF16),
                   jax.ShapeDtypeStruct((t, MEM_W), BF16)],
        grid=(t // tm,),
        in_specs=[pl.BlockSpec((tm, D_MODEL), lambda i: (i, 0)),
                  pl.BlockSpec((1, D_MODEL), lambda i: (0, 0)),
                  pl.BlockSpec((D_MODEL, n), lambda i: (0, 0)),
                  pl.BlockSpec((1, 2 * TOK_W), lambda i: (0, 0))],
        out_specs=[pl.BlockSpec((1, DA_HEADS, hd, tm), lambda i: (i // nt, 0, 0, i % nt)),
                   pl.BlockSpec((tm, TOK_W), lambda i: (i, 0)),
                   pl.BlockSpec((1, DA_HEADS, 1, hd + ONES_ROWS, tm), lambda i: (i // nt, 0, i % nt, 0, 0)),
                   pl.BlockSpec((tm, MEM_W), lambda i: (i, 0))],
        compiler_params=_cparams(("parallel",)),
        name="inproj0",
    )(x, g, w, qkg)


def _attn_kernel(lq1_ref, lk1_ref, lq2_ref, lk2_ref, subg_ref, qt_ref, k_ref, vt_ref, o_ref,
                 acc_ref, m_ref, alpha_ref, s_ref, p_ref, *, tq, tk, lambda_init):
    qi = pl.program_id(2)
    qt = qt_ref[0, 0]
    row = lax.broadcasted_iota(jnp.int32, qt.shape, 0)
    zero = jnp.zeros_like(qt)
    qs = (jnp.where(row < DA_DH, qt, zero), jnp.where(row >= DA_DH, qt, zero))
    hd = 2 * DA_DH
    acc_ref[...] = jnp.zeros_like(acc_ref)
    m_ref[...] = jnp.full_like(m_ref, NEG)
    p_ref[1] = jnp.zeros(p_ref.shape[1:], p_ref.dtype)
    alpha_ref[1] = jnp.ones(alpha_ref.shape[1:], alpha_ref.dtype)

    def scores(ki, slot):
        start = pl.multiple_of(ki * tk, tk)
        k = k_ref[0, pl.ds(start, tk), :]
        for i in range(2):
            s_ref[slot, i] = jnp.dot(k, qs[i], preferred_element_type=F32)

    def softmax(slot, masked):
        if masked:
            kc = lax.broadcasted_iota(jnp.int32, (tk, tq), 0) >> 6
            qc = lax.broadcasted_iota(jnp.int32, (tk, tq), 1) >> 6
            vis = kc <= qc
        for i in range(2):
            s = s_ref[slot, i]
            if masked:
                s = jnp.where(vis, s, NEG)
            m_old = m_ref[i]
            m_new = jnp.maximum(m_old, jnp.max(s, axis=0, keepdims=True))
            alpha_ref[slot, i] = jnp.exp2(m_old - m_new)
            p_ref[slot, i] = jnp.exp2(s - m_new).astype(BF16)
            m_ref[i] = m_new

    def values(ki, slot):
        vt = vt_ref[0, 0, ki]
        for i in range(2):
            acc_ref[i] = alpha_ref[slot, i] * acc_ref[i] + jnp.dot(vt, p_ref[slot, i],
                                                                    preferred_element_type=F32)

    odd = (qi & 1) == 1

    @pl.when(odd)
    def _():
        scores(0, 1)
        scores(1, 0)
        softmax(1, False)

    @pl.when(jnp.logical_not(odd))
    def _():
        scores(0, 0)

    def body(j, carry):
        ki = (qi & 1) + 2 * j
        scores(ki + 1, 1)
        softmax(0, False)
        values(jnp.maximum(ki - 1, 0), 1)
        scores(ki + 2, 0)
        softmax(1, False)
        values(ki, 0)
        return carry

    lax.fori_loop(0, qi >> 1, body, 0)
    softmax(0, True)
    values(jnp.maximum(qi - 1, 0), 1)
    values(qi, 0)

    lam = (jnp.exp(jnp.sum(lq1_ref[...] * lk1_ref[...])) - jnp.exp(jnp.sum(lq2_ref[...] * lk2_ref[...]))
           + lambda_init)
    o1 = acc_ref[0, :hd, :] / acc_ref[0, hd:hd + 1, :]
    o2 = acc_ref[1, :hd, :] / acc_ref[1, hd:hd + 1, :]
    ot = o1 - lam * o2
    ms = jnp.mean(ot * ot, axis=0, keepdims=True)
    ot = ot * lax.rsqrt(ms + EPS)
    o_ref[0] = (ot.T * (subg_ref[...] * (1.0 - lambda_init))).astype(o_ref.dtype)


def _diff_attention(qt, k3, vt, lq1, lk1, lq2, lk2, subg, lambda_init):
    b, s, _ = k3.shape
    tk = vt.shape[-1]
    tq = tk
    nq = s // tq
    hd = 2 * DA_DH
    hde = hd + ONES_ROWS
    vec = lambda n: pl.BlockSpec((1, n), lambda bi, h, qi: (0, 0))
    kern = functools.partial(_attn_kernel, tq=tq, tk=tk, lambda_init=lambda_init)
    return pl.pallas_call(
        kern,
        out_shape=jax.ShapeDtypeStruct((b, s, TOK_W), BF16),
        grid=(b, DA_HEADS, nq),
        in_specs=[vec(DA_DH), vec(DA_DH), vec(DA_DH), vec(DA_DH), vec(hd),
                  pl.BlockSpec((1, 1, hd, tq), lambda bi, h, qi: (bi, h, 0, qi)),
                  pl.BlockSpec((1, s, hd), lambda bi, h, qi: (bi, 0, h)),
                  pl.BlockSpec((1, 1, s // tk, hde, tk), lambda bi, h, qi: (bi, h, 0, 0, 0))],
        out_specs=pl.BlockSpec((1, tq, hd), lambda bi, h, qi: (bi, qi, h)),
        scratch_shapes=[pltpu.VMEM((2, hde, tq), F32), pltpu.VMEM((2, 1, tq), F32),
                        pltpu.VMEM((2, 2, 1, tq), F32), pltpu.VMEM((2, 2, tk, tq), F32),
                        pltpu.VMEM((2, 2, tk, tq), BF16)],
        compiler_params=_cparams(("parallel", "parallel", "parallel")),
        name="diff_attn",
    )(lq1, lk1, lq2, lk2, subg, qt, k3, vt)


def _memkv_kernel(mem_ref, g_ref, wkv_ref, kng_ref, kv_ref):
    mn = _rms(mem_ref[0], g_ref[...]).astype(BF16)
    kv = jnp.dot(mn, wkv_ref[0], preferred_element_type=F32)
    k = kv[:, :MEM_W]
    kn = k * lax.rsqrt(_seg_mean_sq(k) + EPS) * kng_ref[0]
    kv_ref[0, 0, :, :MEM_W] = kn.astype(BF16)
    kv_ref[0, 0, :, MEM_W:] = kv[:, MEM_W:].astype(BF16)


def _memkv(mem, g, wkv, kng):
    depth = wkv.shape[0]
    b = mem.shape[0]
    return pl.pallas_call(
        _memkv_kernel,
        out_shape=jax.ShapeDtypeStruct((depth, b, MEM_LEN, 2 * MEM_W), BF16),
        grid=(depth, b),
        in_specs=[pl.BlockSpec((1, MEM_LEN, D_MODEL), lambda d, bi: (bi, 0, 0)),
                  pl.BlockSpec((1, D_MODEL), lambda d, bi: (0, 0)),
                  pl.BlockSpec((1, D_MODEL, 2 * MEM_W), lambda d, bi: (d, 0, 0)),
                  pl.BlockSpec((1, 1, MEM_W), lambda d, bi: (d, 0, 0))],
        out_specs=pl.BlockSpec((1, 1, MEM_LEN, 2 * MEM_W), lambda d, bi: (d, bi, 0, 0)),
        compiler_params=_cparams(("parallel", "parallel")),
        name="memkv",
    )(mem, g, wkv, kng)


def _mem_attention(mq, qg, kv):
    qn = (mq * lax.rsqrt(_seg_mean_sq(mq) + EPS) * qg).astype(BF16)
    k = kv[:, :MEM_W]
    v = kv[:, MEM_W:]
    lane = lax.broadcasted_iota(jnp.int32, qn.shape, 1) >> 6
    out = jnp.zeros(qn.shape, F32)
    for h in range(MEM_HEADS):
        sel = lane == h
        qh = jnp.where(sel, qn, jnp.zeros_like(qn))
        s = lax.dot_general(qh, k, (((1,), (1,)), ((), ())), preferred_element_type=F32)
        p = jnp.exp(s - jnp.max(s, axis=1, keepdims=True))
        l = jnp.sum(p, axis=1, keepdims=True)
        o = jnp.dot(p.astype(BF16), v, preferred_element_type=F32)
        out = jnp.where(sel, o / l, out)
    return out


R_E1, R_E2, R_G1, R_G2, R_RANK1, R_RANK2 = range(6)


def _top2_route(logits, count_ref):
    tm = logits.shape[0]
    lane = lax.broadcasted_iota(jnp.int32, logits.shape, 1)
    lg = jnp.where(lane < N_EXPERTS, logits, NEG)
    m1 = jnp.max(lg, axis=1, keepdims=True)
    i1 = jnp.min(jnp.where(lg == m1, lane, LANES), axis=1, keepdims=True)
    lg2 = jnp.where(lane == i1, NEG, lg)
    m2 = jnp.max(lg2, axis=1, keepdims=True)
    i2 = jnp.min(jnp.where(lg2 == m2, lane, LANES), axis=1, keepdims=True)
    g1 = 1.0 / (1.0 + jnp.exp(m2 - m1))
    g2 = 1.0 - g1
    chosen = jnp.logical_or(lane == i1, lane == i2)
    onehot = jnp.where(chosen, 1.0, 0.0)
    r = lax.broadcasted_iota(jnp.int32, (tm, tm), 0)
    c = lax.broadcasted_iota(jnp.int32, (tm, tm), 1)
    before = jnp.where(c < r, 1.0, 0.0).astype(BF16)
    prefix = jnp.dot(before, onehot.astype(BF16), preferred_element_type=F32) + count_ref[...]
    count_ref[...] += jnp.sum(onehot, axis=0, keepdims=True)
    rank1 = jnp.sum(jnp.where(lane == i1, prefix, 0.0), axis=1, keepdims=True)
    rank2 = jnp.sum(jnp.where(lane == i2, prefix, 0.0), axis=1, keepdims=True)
    rec = jnp.zeros(logits.shape, F32)
    for idx, val in ((R_E1, i1.astype(F32)), (R_E2, i2.astype(F32)), (R_G1, g1), (R_G2, g2),
                     (R_RANK1, rank1), (R_RANK2, rank2)):
        rec = jnp.where(lane == idx, val, rec)
    return rec


def _mixout_kernel(*refs, with_router):
    if with_router:
        (tok_ref, mq_ref, kv_ref, qg_ref, wo_ref, x_ref, g2_ref, wr_ref,
         x1_ref, hp_ref, route_ref, count_ref) = refs
    else:
        tok_ref, mq_ref, kv_ref, qg_ref, wo_ref, x_ref, g2_ref, x1_ref, h2_ref = refs
    mo = _mem_attention(mq_ref[...].astype(F32), qg_ref[...], kv_ref[0])
    y = jnp.dot(tok_ref[...], wo_ref[:TOK_W, :], preferred_element_type=F32)
    y = y + jnp.dot(mo.astype(BF16), wo_ref[TOK_W:, :], preferred_element_type=F32)
    x1 = x_ref[...] + y
    x1_ref[...] = x1
    h2 = _rms(x1, g2_ref[...])
    if with_router:
        @pl.when(pl.program_id(0) == 0)
        def _():
            count_ref[...] = jnp.zeros_like(count_ref)

        hp_ref[...] = h2
        wr = wr_ref[...]
        w_hi = wr.astype(BF16)
        w_lo = (wr - w_hi.astype(F32)).astype(BF16)
        h_hi = h2.astype(BF16)
        h_lo = (h2 - h_hi.astype(F32)).astype(BF16)
        logits = (jnp.dot(h_hi, w_hi, preferred_element_type=F32) + jnp.dot(h_lo, w_hi, preferred_element_type=F32)
                  + jnp.dot(h_hi, w_lo, preferred_element_type=F32))
        route_ref[...] = _top2_route(logits, count_ref)
    else:
        h2_ref[...] = h2.astype(BF16)


def _mixout(tok, mq_src, mq_block, kv, qg, wo, x, g2, wr=None, tm=512):
    t = x.shape[0]
    tiles_per_batch = t // kv.shape[0] // tm
    with_router = wr is not None
    in_specs = [pl.BlockSpec((tm, TOK_W), lambda i: (i, 0)),
                pl.BlockSpec((tm, MEM_W), lambda i: (i, mq_block)),
                pl.BlockSpec((1, MEM_LEN, 2 * MEM_W), lambda i: (i // tiles_per_batch, 0, 0)),
                pl.BlockSpec((1, MEM_W), lambda i: (0, 0)),
                pl.BlockSpec((D_MODEL, D_MODEL), lambda i: (0, 0)),
                pl.BlockSpec((tm, D_MODEL), lambda i: (i, 0)),
                pl.BlockSpec((1, D_MODEL), lambda i: (0, 0))]
    args = [tok, mq_src, kv, qg, wo, x, g2]
    row_spec = lambda n: pl.BlockSpec((tm, n), lambda i: (i, 0))
    if with_router:
        in_specs.append(pl.BlockSpec((D_MODEL, LANES), lambda i: (0, 0)))
        args.append(wr)
        out_shape = [jax.ShapeDtypeStruct((t, D_MODEL), F32), jax.ShapeDtypeStruct((t, D_MODEL), F32),
                     jax.ShapeDtypeStruct((t, LANES), F32), jax.ShapeDtypeStruct((1, LANES), F32)]
        out_specs = [row_spec(D_MODEL), row_spec(D_MODEL), row_spec(LANES),
                     pl.BlockSpec((1, LANES), lambda i: (0, 0))]
    else:
        out_shape = [jax.ShapeDtypeStruct((t, D_MODEL), F32), jax.ShapeDtypeStruct((t, D_MODEL), BF16)]
        out_specs = [row_spec(D_MODEL), row_spec(D_MODEL)]
    return pl.pallas_call(
        functools.partial(_mixout_kernel, with_router=with_router),
        out_shape=out_shape,
        grid=(t // tm,),
        in_specs=in_specs,
        out_specs=out_specs,
        compiler_params=_cparams(("arbitrary",) if with_router else ("parallel",)),
        name="mixout_router" if with_router else "mixout",
    )(*args)


def _ffn_kernel(h_ref, x_ref, wg_ref, wu_ref, wd_ref, gn_ref, xo_ref, ho_ref, acc_ref):
    f = pl.program_id(1)

    @pl.when(f == 0)
    def _():
        acc_ref[...] = x_ref[...]

    h = h_ref[...]
    a = jnp.dot(h, wg_ref[...], preferred_element_type=F32)
    u = jnp.dot(h, wu_ref[...], preferred_element_type=F32)
    act = a * jax.nn.sigmoid(a) * u
    acc_ref[...] += jnp.dot(act.astype(BF16), wd_ref[...], preferred_element_type=F32)

    @pl.when(f == pl.num_programs(1) - 1)
    def _():
        xn = acc_ref[...]
        xo_ref[...] = xn
        ho_ref[...] = _rms(xn, gn_ref[...]).astype(BF16)


def _ffn(h, x, wg, wu, wd, gn, tm=512, tf=1408):
    t = x.shape[0]
    fdim = wg.shape[1]
    row = pl.BlockSpec((tm, D_MODEL), lambda i, f: (i, 0))
    return pl.pallas_call(
        _ffn_kernel,
        out_shape=[jax.ShapeDtypeStruct((t, D_MODEL), F32), jax.ShapeDtypeStruct((t, D_MODEL), BF16)],
        grid=(t // tm, fdim // tf),
        in_specs=[row, row,
                  pl.BlockSpec((D_MODEL, tf), lambda i, f: (0, f)),
                  pl.BlockSpec((D_MODEL, tf), lambda i, f: (0, f)),
                  pl.BlockSpec((tf, D_MODEL), lambda i, f: (f, 0)),
                  pl.BlockSpec((1, D_MODEL), lambda i, f: (0, 0))],
        out_specs=[row, row],
        scratch_shapes=[pltpu.VMEM((tm, D_MODEL), F32)],
        compiler_params=_cparams(("parallel", "arbitrary")),
        name="ffn",
    )(h, x, wg, wu, wd, gn)


def _row_copy(src_ref, src_row, dst_ref, dst_row, sem):
    return pltpu.make_async_copy(src_ref.at[pl.ds(src_row, 1), :], dst_ref.at[pl.ds(dst_row, 1), :], sem)


def _dispatch_kernel(pad_start_ref, pad_len_ref, pos_ref, hp_ref, xs_ref, zeros_ref, sem, zsem, *, tm, tmr):
    nbits = tmr.bit_length() - 1

    @pl.when(pl.program_id(0) == 0)
    def _():
        zeros_ref[...] = jnp.zeros_like(zeros_ref)
        for e in range(N_EXPERTS):
            start = pad_start_ref[e]
            length = pad_len_ref[e]
            singles = length & (SUBLANES - 1)
            for j in range(SUBLANES - 1):
                @pl.when(j < singles)
                def _(j=j, start=start):
                    cp = _row_copy(zeros_ref, 0, xs_ref, start + j, zsem)
                    cp.start()
                    cp.wait()

            done = start + singles
            for b in range(SUBLANES.bit_length() - 1, nbits):
                n = 1 << b
                bit = (length >> b) & 1

                @pl.when(bit == 1)
                def _(n=n, done=done):
                    cp = pltpu.make_async_copy(zeros_ref.at[pl.ds(0, n), :],
                                               xs_ref.at[pl.ds(pl.multiple_of(done, SUBLANES), n), :], zsem)
                    cp.start()
                    cp.wait()

                done = done + bit * n

        n_tiles = xs_ref.shape[0] // tmr
        half = tmr // 2
        for j in range(n_tiles - N_EXPERTS, n_tiles):
            @pl.when(j * tmr >= pad_start_ref[N_EXPERTS])
            def _(j=j):
                for c in range(2):
                    cp = pltpu.make_async_copy(zeros_ref, xs_ref.at[pl.ds(j * tmr + c * half, half), :], zsem)
                    cp.start()
                    cp.wait()

    def issue(r, carry):
        _row_copy(hp_ref, r, xs_ref, pos_ref[0, 0, r], sem).start(priority=0)
        _row_copy(hp_ref, r, xs_ref, pos_ref[0, 0, tm + r], sem).start(priority=1)
        return carry

    lax.fori_loop(0, tm, issue, 0, unroll=ISSUE_UNROLL)
    for _ in range(2):
        pltpu.make_async_copy(hp_ref, xs_ref.at[pl.ds(0, tm), :], sem).wait()


def _dispatch(pad_start, pad_len, pos, hp, n_rows, tm, tmr):
    t, w = hp.shape
    return pl.pallas_call(
        functools.partial(_dispatch_kernel, tm=tm, tmr=tmr),
        out_shape=jax.ShapeDtypeStruct((n_rows, w), hp.dtype),
        grid_spec=pltpu.PrefetchScalarGridSpec(
            num_scalar_prefetch=2,
            grid=(t // tm,),
            in_specs=[pl.BlockSpec((1, 1, 2 * tm), lambda i, ps, pn: (i, 0, 0), memory_space=pltpu.SMEM),
                      pl.BlockSpec((tm, w), lambda i, ps, pn: (i, 0))],
            out_specs=pl.BlockSpec(memory_space=pl.ANY),
            scratch_shapes=[pltpu.VMEM((tmr // 2, w), hp.dtype), pltpu.SemaphoreType.DMA,
                            pltpu.SemaphoreType.DMA]),
        compiler_params=_cparams(("arbitrary",)),
        name="moe_dispatch",
    )(pad_start, pad_len, pos, hp)


def _experts_kernel(te_ref, tv_ref, xs_ref, wg_ref, wu_ref, wd_ref, y_ref, h_ref):
    i = pl.program_id(0)
    f = pl.program_id(1)

    @pl.when(tv_ref[i] == 1)
    def _():
        @pl.when(f == 0)
        def _():
            h_ref[...] = xs_ref[...].astype(BF16)

        h = h_ref[...]
        a = jnp.dot(h, wg_ref[0], preferred_element_type=F32)
        u = jnp.dot(h, wu_ref[0], preferred_element_type=F32)
        act = a * jax.nn.sigmoid(a) * u
        contrib = jnp.dot(act.astype(BF16), wd_ref[0], preferred_element_type=F32)

        @pl.when(f == 0)
        def _():
            y_ref[...] = contrib

        @pl.when(f != 0)
        def _():
            y_ref[...] += contrib

    @pl.when(jnp.logical_and(tv_ref[i] == 0, f == 0))
    def _():
        y_ref[...] = jnp.zeros_like(y_ref)


def _experts(tile_expert, tile_valid, xs, wg, wu, wd, tmr, tf):
    n_rows, w = xs.shape
    fdim = wg.shape[2]
    nf = fdim // tf
    fidx = lambda i, f, te, tv: jnp.where(tv[i] == 1, f, nf - 1)
    return pl.pallas_call(
        _experts_kernel,
        out_shape=jax.ShapeDtypeStruct((n_rows, D_MODEL), F32),
        grid_spec=pltpu.PrefetchScalarGridSpec(
            num_scalar_prefetch=2,
            grid=(n_rows // tmr, nf),
            in_specs=[pl.BlockSpec((tmr, w), lambda i, f, te, tv: (jnp.where(tv[i] == 1, i, 0), 0)),
                      pl.BlockSpec((1, D_MODEL, tf), lambda i, f, te, tv: (te[i], 0, fidx(i, f, te, tv))),
                      pl.BlockSpec((1, D_MODEL, tf), lambda i, f, te, tv: (te[i], 0, fidx(i, f, te, tv))),
                      pl.BlockSpec((1, tf, D_MODEL), lambda i, f, te, tv: (te[i], fidx(i, f, te, tv), 0))],
            out_specs=pl.BlockSpec((tmr, D_MODEL), lambda i, f, te, tv: (i, 0)),
            scratch_shapes=[pltpu.VMEM((tmr, D_MODEL), BF16)]),
        compiler_params=_cparams(("parallel", "arbitrary")),
        name="moe_experts",
    )(tile_expert, tile_valid, xs, wg, wu, wd)


def _combine_kernel(pos_ref, route_ref, x_ref, y_ref, o_ref, buf_ref, sem, *, tm):
    def issue(r, carry):
        _row_copy(y_ref, pos_ref[0, 0, r], buf_ref.at[0], r, sem).start(priority=0)
        _row_copy(y_ref, pos_ref[0, 0, tm + r], buf_ref.at[1], r, sem).start(priority=1)
        return carry

    lax.fori_loop(0, tm, issue, 0, unroll=ISSUE_UNROLL)
    for k in range(2):
        pltpu.make_async_copy(y_ref.at[pl.ds(0, tm), :], buf_ref.at[k], sem).wait()
    route = route_ref[...]
    g1 = route[:, R_G1:R_G1 + 1]
    g2 = route[:, R_G2:R_G2 + 1]
    o_ref[...] = x_ref[...] + (g1 * buf_ref[0] + g2 * buf_ref[1])


def _combine(pos, route, x, y, tm):
    t = x.shape[0]
    return pl.pallas_call(
        functools.partial(_combine_kernel, tm=tm),
        out_shape=jax.ShapeDtypeStruct((t, D_MODEL), F32),
        grid=(t // tm,),
        in_specs=[pl.BlockSpec((1, 1, 2 * tm), lambda i: (i, 0, 0), memory_space=pltpu.SMEM),
                  pl.BlockSpec((tm, LANES), lambda i: (i, 0)),
                  pl.BlockSpec((tm, D_MODEL), lambda i: (i, 0)),
                  pl.BlockSpec(memory_space=pl.ANY)],
        out_specs=pl.BlockSpec((tm, D_MODEL), lambda i: (i, 0)),
        scratch_shapes=[pltpu.VMEM((2, tm, D_MODEL), F32), pltpu.SemaphoreType.DMA],
        compiler_params=_cparams(("arbitrary",)),
        name="moe_combine",
    )(pos, route, x, y)


def _moe(hp, route, counts, x, wg, wu, wd, tm=512, tmr=512, tf=1792):
    t = hp.shape[0]
    n_tiles = 2 * t // tmr + N_EXPERTS
    n_rows = n_tiles * tmr
    cnt = counts[0, :N_EXPERTS].astype(jnp.int32)
    padded = (cnt + tmr - 1) // tmr * tmr
    ends = jnp.cumsum(padded)
    offs = ends - padded
    e1 = route[:, R_E1].astype(jnp.int32)
    e2 = route[:, R_E2].astype(jnp.int32)
    pos1 = offs[e1] + route[:, R_RANK1].astype(jnp.int32)
    pos2 = offs[e2] + route[:, R_RANK2].astype(jnp.int32)
    pos = jnp.concatenate([pos1.reshape(t // tm, 1, tm), pos2.reshape(t // tm, 1, tm)], axis=2)
    tile_start = jnp.arange(n_tiles, dtype=jnp.int32) * tmr
    tile_valid = (tile_start < ends[-1]).astype(jnp.int32)
    last_tile = ends[-1] // tmr - 1
    clamped = jnp.minimum(tile_start, last_tile * tmr)
    tile_expert = jnp.sum((clamped[:, None] >= ends[None, :]).astype(jnp.int32), axis=1)
    pad_start = jnp.concatenate([offs + cnt, ends[-1:]])
    xs = _dispatch(pad_start, padded - cnt, pos, hp, n_rows, tm, tmr)
    y = _experts(tile_expert, tile_valid, xs, wg, wu, wd, tmr, tf)
    return _combine(pos, route, x, y, tm)


def _inproj1_kernel(x_ref, wz_ref, wx_ref, wm_ref, wdt_ref, z_ref, xbc_ref, mq_ref, dt_ref):
    h = x_ref[...]
    z_ref[...] = jnp.dot(h, wz_ref[...], preferred_element_type=F32).astype(BF16)
    xbc_ref[...] = jnp.dot(h, wx_ref[...], preferred_element_type=F32).astype(BF16)
    mq_ref[...] = jnp.dot(h, wm_ref[...], preferred_element_type=F32).astype(BF16)
    dt_ref[...] = jnp.dot(h, wdt_ref[...], preferred_element_type=F32)


def _inproj1(h, wz, wx, wm, wdt, tm=512):
    t = h.shape[0]
    full = lambda a: pl.BlockSpec(a.shape, lambda i: (0, 0))
    row = lambda n: pl.BlockSpec((tm, n), lambda i: (i, 0))
    return pl.pallas_call(
        _inproj1_kernel,
        out_shape=[jax.ShapeDtypeStruct((t, TOK_W), BF16), jax.ShapeDtypeStruct((t, SSM_CONV_DIM), BF16),
                   jax.ShapeDtypeStruct((t, MEM_W), BF16), jax.ShapeDtypeStruct((t, LANES), F32)],
        grid=(t // tm,),
        in_specs=[row(D_MODEL), full(wz), full(wx), full(wm), full(wdt)],
        out_specs=[row(TOK_W), row(SSM_CONV_DIM), row(MEM_W), row(LANES)],
        compiler_params=_cparams(("parallel",)),
        name="inproj1",
    )(h, wz, wx, wm, wdt)


def _ssd_kernel(z_ref, xbc_ref, dt_ref, cw_ref, cb_ref, dtb_ref, alog_ref, dsk_ref, ng_ref, ex_ref,
                o_ref, xe_ref, st_ref, *, tl):
    j = pl.program_id(1)
    pad = 8

    @pl.when(j == 0)
    def _():
        xe_ref[0:pad, :] = jnp.zeros((pad, SSM_CONV_DIM), F32)
        st_ref[...] = jnp.zeros_like(st_ref)

    x = xbc_ref[0].astype(F32)
    xe_ref[pad:pad + tl, :] = x
    acc = cb_ref[...] + cw_ref[SSM_CONV_K - 1:SSM_CONV_K, :] * x
    for k in range(SSM_CONV_K - 1):
        sh = SSM_CONV_K - 1 - k
        acc = acc + cw_ref[k:k + 1, :] * xe_ref[pad - sh:pad - sh + tl, :]
    xe_ref[0:pad, :] = x[tl - pad:tl, :]
    xc = acc * jax.nn.sigmoid(acc)
    xs = xc[:, :TOK_W]

    dt_in = dt_ref[0] + dtb_ref[...]
    dt = jnp.maximum(dt_in, 0.0) + jnp.log1p(jnp.exp(-jnp.abs(dt_in)))
    a = -jnp.exp(alog_ref[...])
    dta = dt * a
    r = lax.broadcasted_iota(jnp.int32, (tl, tl), 0)
    c = lax.broadcasted_iota(jnp.int32, (tl, tl), 1)
    tril = r >= c
    ltri = jnp.where(tril, 1.0, 0.0).astype(F32)
    cum = jnp.dot(ltri, dta, preferred_element_type=F32, precision=lax.Precision.HIGHEST)
    cum_t = cum.T
    dt_t = dt.T
    lane = lax.broadcasted_iota(jnp.int32, (tl, LANES), 1)

    hg = SSM_HEADS // SSM_GROUPS
    y_pairs = []
    cbs = []
    for g in range(SSM_GROUPS):
        bm = xc[:, TOK_W + g * SSM_STATE:TOK_W + (g + 1) * SSM_STATE].astype(BF16)
        cm = xc[:, TOK_W + (SSM_GROUPS + g) * SSM_STATE:TOK_W + (SSM_GROUPS + g + 1) * SSM_STATE].astype(BF16)
        cbs.append(lax.dot_general(cm, bm, (((1,), (1,)), ((), ())), preferred_element_type=F32))
    for pr in range(SSM_HEADS // 2):
        xp = xs[:, pr * LANES:(pr + 1) * LANES].astype(BF16)
        ys = []
        for hh in (2 * pr, 2 * pr + 1):
            g = hh // hg
            seg = cum[:, hh:hh + 1] - cum_t[hh:hh + 1, :]
            decay = jnp.exp(jnp.where(tril, seg, NEG))
            w = cbs[g] * decay * dt_t[hh:hh + 1, :]
            ys.append(jnp.dot(w.astype(BF16), xp, preferred_element_type=F32))
        y_pairs.append(jnp.where(lane < SSM_HD, ys[0], ys[1]))
    y = jnp.concatenate(y_pairs, axis=1)

    expcum = jnp.exp(cum)
    to_end = jnp.exp(cum[tl - 1:tl, :] - cum) * dt
    stacked = jnp.concatenate([expcum, to_end], axis=0)
    exd = jnp.dot(stacked, ex_ref[...], preferred_element_type=F32, precision=lax.Precision.HIGHEST)
    expcum_x = exd[:tl]
    xw = (xs * exd[tl:]).astype(BF16)
    gw = TOK_W // SSM_GROUPS
    y_off = []
    for g in range(SSM_GROUPS):
        cs = slice(g * gw, (g + 1) * gw)
        bm_t = xc[:, TOK_W + g * SSM_STATE:TOK_W + (g + 1) * SSM_STATE].T.astype(BF16)
        cm = xc[:, TOK_W + (SSM_GROUPS + g) * SSM_STATE:TOK_W + (SSM_GROUPS + g + 1) * SSM_STATE].astype(BF16)
        sg = st_ref[:, cs]
        y_off.append(jnp.dot(cm, sg.astype(BF16), preferred_element_type=F32) * expcum_x[:, cs])
        st_ref[:, cs] = sg * expcum_x[tl - 1:tl, cs] + jnp.dot(bm_t, xw[:, cs], preferred_element_type=F32)
    y = y + jnp.concatenate(y_off, axis=1) + dsk_ref[...] * xs
    zf = z_ref[0].astype(F32)
    y = y * (zf * jax.nn.sigmoid(zf))
    outs = []
    for g in range(SSM_GROUPS):
        cs = slice(g * gw, (g + 1) * gw)
        outs.append(_rms(y[:, cs], ng_ref[:, cs]))
    o_ref[0] = jnp.concatenate(outs, axis=1).astype(o_ref.dtype)


def _ssd(z3, xbc3, dt3, cw, cb, dtb, alog, dsk, ng, ex, tl=256):
    b, s, _ = z3.shape
    full = lambda a: pl.BlockSpec(a.shape, lambda bi, j: (0, 0))
    blk = lambda n: pl.BlockSpec((1, tl, n), lambda bi, j: (bi, j, 0))
    return pl.pallas_call(
        functools.partial(_ssd_kernel, tl=tl),
        out_shape=jax.ShapeDtypeStruct((b, s, TOK_W), BF16),
        grid=(b, s // tl),
        in_specs=[blk(TOK_W), blk(SSM_CONV_DIM), blk(LANES), full(cw), full(cb), full(dtb), full(alog),
                  full(dsk), full(ng), full(ex)],
        out_specs=blk(TOK_W),
        scratch_shapes=[pltpu.VMEM((tl + 8, SSM_CONV_DIM), F32), pltpu.VMEM((SSM_STATE, TOK_W), F32)],
        compiler_params=_cparams(("parallel", "arbitrary")),
        name="conv_ssd",
    )(z3, xbc3, dt3, cw, cb, dtb, alog, dsk, ng, ex)


def _pad_lanes(v, n=LANES):
    return jnp.pad(v, [(0, 0)] * (v.ndim - 1) + [(0, n - v.shape[-1])])


def kernel(x, mem, ln1_g, ln2_g, mem_norm_g, w_out, mem_w_kv, mem_qn_g, mem_kn_g, da_w_in, da_qn_g, da_kn_g,
           da_lq1, da_lk1, da_lq2, da_lk2, da_sub_g, ssm_w_in, ssm_conv_w, ssm_conv_b, ssm_dt_bias, ssm_a_log,
           ssm_d, ssm_norm_g, ffn_w_gate, ffn_w_up, ffn_w_down, moe_w_router, moe_w_gate, moe_w_up, moe_w_down):
    b, s, d = x.shape
    t = b * s
    row = lambda v: v.reshape(1, -1).astype(F32)
    xt = x.reshape(t, d)

    mem_qg = jnp.tile(mem_qn_g.astype(F32) * (MEM_HD ** -0.5), (1, MEM_HEADS))
    mem_kg = jnp.tile(mem_kn_g.astype(F32), (1, MEM_HEADS))[:, None, :]
    kv = _memkv(mem, row(mem_norm_g), mem_w_kv.astype(BF16), mem_kg)
    wo = w_out.astype(BF16)

    lambda_init = 0.8 - 0.6 * math.exp(-0.3 * 0)
    qkg = jnp.concatenate([jnp.tile(da_qn_g[0].astype(F32) * (DA_DH ** -0.5 * LOG2E), 2 * DA_HEADS),
                           jnp.tile(da_kn_g[0].astype(F32), 2 * DA_HEADS)]).reshape(1, -1)
    qt, kn, vt, mq0 = _inproj0(xt, row(ln1_g[0]), da_w_in[0].astype(BF16), qkg, b)
    tok = _diff_attention(qt, kn.reshape(b, s, TOK_W), vt, row(da_lq1[0]), row(da_lk1[0]), row(da_lq2[0]),
                          row(da_lk2[0]), row(da_sub_g[0]), lambda_init)
    x1, h2 = _mixout(tok.reshape(t, TOK_W), mq0, 0, kv[0], mem_qg[0:1], wo[0], xt, row(ln2_g[0]))
    x2, h3 = _ffn(h2, x1, ffn_w_gate[0].astype(BF16), ffn_w_up[0].astype(BF16), ffn_w_down[0].astype(BF16),
                  row(ln1_g[1]))

    w_in = ssm_w_in[0]
    o1 = TOK_W
    o2 = o1 + SSM_CONV_DIM
    o3 = o2 + SSM_HEADS
    z, xbc, mq, dt = _inproj1(h3, w_in[:, :o1].astype(BF16), w_in[:, o1:o2].astype(BF16),
                              w_in[:, o3:].astype(BF16), _pad_lanes(w_in[:, o2:o3]).astype(BF16))
    expand = jnp.repeat(jnp.eye(SSM_HEADS, dtype=F32), SSM_HD, axis=1)
    expand = jnp.pad(expand, ((0, LANES - SSM_HEADS), (0, 0)))
    dsk = jnp.repeat(ssm_d[0].astype(F32), SSM_HD).reshape(1, -1)
    tok1 = _ssd(z.reshape(b, s, -1), xbc.reshape(b, s, -1), dt.reshape(b, s, -1),
                ssm_conv_w[0].astype(F32), row(ssm_conv_b[0]), _pad_lanes(row(ssm_dt_bias[0])),
                _pad_lanes(row(ssm_a_log[0])), dsk, row(ssm_norm_g[0]), expand)
    x3, hp, route, counts = _mixout(tok1.reshape(t, TOK_W), mq, 0, kv[1], mem_qg[1:2], wo[1], x2, row(ln2_g[1]),
                                    wr=_pad_lanes(moe_w_router[0].astype(F32)))
    x4 = _moe(hp, route, counts, x3, moe_w_gate[0].astype(BF16), moe_w_up[0].astype(BF16),
              moe_w_down[0].astype(BF16))
    return x4.reshape(b, s, d)
```

```python
import functools
import math

import jax
import jax.numpy as jnp
from jax import lax
from jax.experimental import pallas as pl
from jax.experimental.pallas import tpu as pltpu

F32 = jnp.float32
BF16 = jnp.bfloat16

D_MODEL = 1024
CHUNK = 64
MEM_LEN = 256
MEM_W = 256
MEM_HEADS = 4
MEM_HD = 64
TOK_W = 768
DA_DH = 64
DA_HEADS = 6
SSM_HD = 64
SSM_HEADS = 12
SSM_GROUPS = 2
SSM_STATE = 128
SSM_CONV_K = 4
SSM_CONV_DIM = 1280
N_EXPERTS = 8
EPS = 1e-6
LANES = 128
NEG = -1e30
SUBLANES = 8
ONES_ROWS = SUBLANES
LOG2E = math.log2(math.e)
ISSUE_UNROLL = True
VMEM_LIMIT = 56 * 1024 * 1024


def _cparams(sem):
    return pltpu.CompilerParams(dimension_semantics=sem, vmem_limit_bytes=VMEM_LIMIT)


def _rms(xf, g):
    ms = jnp.mean(xf * xf, axis=-1, keepdims=True)
    return xf * lax.rsqrt(ms + EPS) * g


def _seg_mean_matrix(n, seg_shift):
    r = lax.broadcasted_iota(jnp.int32, (n, n), 0) >> seg_shift
    c = lax.broadcasted_iota(jnp.int32, (n, n), 1) >> seg_shift
    return jnp.where(r == c, 1.0 / (1 << seg_shift), 0.0).astype(BF16)


def _seg_mean_sq(y, seg_shift=6):
    bd = _seg_mean_matrix(256, seg_shift)
    sq = (y * y).astype(BF16)
    parts = [jnp.dot(sq[:, c:c + 256], bd, preferred_element_type=F32)
             for c in range(0, y.shape[1], 256)]
    return parts[0] if len(parts) == 1 else jnp.concatenate(parts, axis=1)


def _inproj0_kernel(x_ref, g_ref, w_ref, qkg_ref, qt_ref, k_ref, vt_ref, mq_ref):
    h = _rms(x_ref[...], g_ref[...]).astype(BF16)
    u = jnp.dot(h, w_ref[...], preferred_element_type=F32)
    tm = u.shape[0]
    hd = 2 * DA_DH
    nqk = 2 * TOK_W
    qk = u[:, :nqk]
    qkn = qk * lax.rsqrt(_seg_mean_sq(qk) + EPS) * qkg_ref[...]
    qt_ref[0] = qkn[:, :TOK_W].T.reshape(DA_HEADS, hd, tm).astype(BF16)
    k_ref[...] = qkn[:, TOK_W:].astype(BF16)
    vt_ref[0, :, 0, :hd, :] = u[:, nqk:nqk + TOK_W].T.reshape(DA_HEADS, hd, tm).astype(BF16)
    vt_ref[0, :, 0, hd:, :] = jnp.ones((DA_HEADS, ONES_ROWS, tm), BF16)
    mq_ref[...] = u[:, nqk + TOK_W:].astype(BF16)


def _inproj0(x, g, w, qkg, b, tm=512):
    t = x.shape[0]
    n = w.shape[1]
    s = t // b
    nt = s // tm
    hd = 2 * DA_DH
    return pl.pallas_call(
        _inproj0_kernel,
        out_shape=[jax.ShapeDtypeStruct((b, DA_HEADS, hd, s), BF16),
                   jax.ShapeDtypeStruct((t, TOK_W), BF16),
                   jax.ShapeDtypeStruct((b, DA_HEADS, nt, hd + ONES_ROWS, tm), BF16),
                   jax.ShapeDtypeStruct((t, MEM_W), BF16)],
        grid=(t // tm,),
        in_specs=[pl.BlockSpec((tm, D_MODEL), lambda i: (i, 0)),
                  pl.BlockSpec((1, D_MODEL), lambda i: (0, 0)),
                  pl.BlockSpec((D_MODEL, n), lambda i: (0, 0)),
                  pl.BlockSpec((1, 2 * TOK_W), lambda i: (0, 0))],
        out_specs=[pl.BlockSpec((1, DA_HEADS, hd, tm), lambda i: (i // nt, 0, 0, i % nt)),
                   pl.BlockSpec((tm, TOK_W), lambda i: (i, 0)),
                   pl.BlockSpec((1, DA_HEADS, 1, hd + ONES_ROWS, tm), lambda i: (i // nt, 0, i % nt, 0, 0)),
                   pl.BlockSpec((tm, MEM_W), lambda i: (i, 0))],
        compiler_params=_cparams(("parallel",)),
        name="inproj0",
    )(x, g, w, qkg)


def _attn_kernel(lq1_ref, lk1_ref, lq2_ref, lk2_ref, subg_ref, qt_ref, k_ref, vt_ref, o_ref,
                 acc_ref, m_ref, alpha_ref, s_ref, p_ref, *, tq, tk, nh, lambda_init):
    qi = pl.program_id(2)
    hd = 2 * DA_DH
    qs = []
    for hs in range(nh):
        qt = qt_ref[0, hs]
        row = lax.broadcasted_iota(jnp.int32, qt.shape, 0)
        zero = jnp.zeros_like(qt)
        qs += [jnp.where(row < DA_DH, qt, zero), jnp.where(row >= DA_DH, qt, zero)]
    acc_ref[...] = jnp.zeros_like(acc_ref)
    m_ref[...] = jnp.full_like(m_ref, NEG)
    p_ref[1] = jnp.zeros(p_ref.shape[1:], p_ref.dtype)
    alpha_ref[1] = jnp.ones(alpha_ref.shape[1:], alpha_ref.dtype)

    def scores(ki, slot):
        start = pl.multiple_of(ki * tk, tk)
        for hs in range(nh):
            k = k_ref[0, pl.ds(start, tk), hs * hd:(hs + 1) * hd]
            for i in range(2):
                c = 2 * hs + i
                s_ref[slot, c] = jnp.dot(k, qs[c], preferred_element_type=F32)

    def softmax(slot, masked):
        if masked:
            kc = lax.broadcasted_iota(jnp.int32, (tk, tq), 0) >> 6
            qc = lax.broadcasted_iota(jnp.int32, (tk, tq), 1) >> 6
            vis = kc <= qc
        for c in range(2 * nh):
            s = s_ref[slot, c]
            if masked:
                s = jnp.where(vis, s, NEG)
            m_old = m_ref[c]
            m_new = jnp.maximum(m_old, jnp.max(s, axis=0, keepdims=True))
            alpha_ref[slot, c] = jnp.exp2(m_old - m_new)
            p_ref[slot, c] = jnp.exp2(s - m_new).astype(BF16)
            m_ref[c] = m_new

    def values(ki, slot):
        for hs in range(nh):
            vt = vt_ref[0, hs, ki]
            for i in range(2):
                c = 2 * hs + i
                acc_ref[c] = alpha_ref[slot, c] * acc_ref[c] + jnp.dot(vt, p_ref[slot, c],
                                                                        preferred_element_type=F32)

    odd = (qi & 1) == 1

    @pl.when(odd)
    def _():
        scores(0, 1)
        scores(1, 0)
        softmax(1, False)

    @pl.when(jnp.logical_not(odd))
    def _():
        scores(0, 0)

    def body(j, carry):
        ki = (qi & 1) + 2 * j
        scores(ki + 1, 1)
        softmax(0, False)
        values(jnp.maximum(ki - 1, 0), 1)
        scores(ki + 2, 0)
        softmax(1, False)
        values(ki, 0)
        return carry

    lax.fori_loop(0, qi >> 1, body, 0)
    softmax(0, True)
    values(jnp.maximum(qi - 1, 0), 1)
    values(qi, 0)

    lam = (jnp.exp(jnp.sum(lq1_ref[...] * lk1_ref[...])) - jnp.exp(jnp.sum(lq2_ref[...] * lk2_ref[...]))
           + lambda_init)
    for hs in range(nh):
        o1 = acc_ref[2 * hs, :hd, :] / acc_ref[2 * hs, hd:hd + 1, :]
        o2 = acc_ref[2 * hs + 1, :hd, :] / acc_ref[2 * hs + 1, hd:hd + 1, :]
        ot = o1 - lam * o2
        ms = jnp.mean(ot * ot, axis=0, keepdims=True)
        ot = ot * lax.rsqrt(ms + EPS)
        o_ref[0, :, hs * hd:(hs + 1) * hd] = (ot.T * (subg_ref[...] * (1.0 - lambda_init))).astype(o_ref.dtype)


def _diff_attention(qt, k3, vt, lq1, lk1, lq2, lk2, subg, lambda_init, nh=1):
    b, s, _ = k3.shape
    tk = vt.shape[-1]
    tq = tk
    nq = s // tq
    hd = 2 * DA_DH
    hde = hd + ONES_ROWS
    nc = 2 * nh
    vec = lambda n: pl.BlockSpec((1, n), lambda bi, h, qi: (0, 0))
    kern = functools.partial(_attn_kernel, tq=tq, tk=tk, nh=nh, lambda_init=lambda_init)
    return pl.pallas_call(
        kern,
        out_shape=jax.ShapeDtypeStruct((b, s, TOK_W), BF16),
        grid=(b, DA_HEADS // nh, nq),
        in_specs=[vec(DA_DH), vec(DA_DH), vec(DA_DH), vec(DA_DH), vec(hd),
                  pl.BlockSpec((1, nh, hd, tq), lambda bi, h, qi: (bi, h, 0, qi)),
                  pl.BlockSpec((1, s, nh * hd), lambda bi, h, qi: (bi, 0, h)),
                  pl.BlockSpec((1, nh, s // tk, hde, tk), lambda bi, h, qi: (bi, h, 0, 0, 0))],
        out_specs=pl.BlockSpec((1, tq, nh * hd), lambda bi, h, qi: (bi, qi, h)),
        scratch_shapes=[pltpu.VMEM((nc, hde, tq), F32), pltpu.VMEM((nc, 1, tq), F32),
                        pltpu.VMEM((2, nc, 1, tq), F32), pltpu.VMEM((2, nc, tk, tq), F32),
                        pltpu.VMEM((2, nc, tk, tq), BF16)],
        compiler_params=_cparams(("parallel", "parallel", "parallel")),
        name="diff_attn",
    )(lq1, lk1, lq2, lk2, subg, qt, k3, vt)


def _memkv_kernel(mem_ref, g_ref, wkv_ref, kng_ref, kv_ref):
    mn = _rms(mem_ref[0], g_ref[...]).astype(BF16)
    kv = jnp.dot(mn, wkv_ref[0], preferred_element_type=F32)
    k = kv[:, :MEM_W]
    kn = k * lax.rsqrt(_seg_mean_sq(k) + EPS) * kng_ref[0]
    kv_ref[0, 0, :, :MEM_W] = kn.astype(BF16)
    kv_ref[0, 0, :, MEM_W:] = kv[:, MEM_W:].astype(BF16)


def _memkv(mem, g, wkv, kng):
    depth = wkv.shape[0]
    b = mem.shape[0]
    return pl.pallas_call(
        _memkv_kernel,
        out_shape=jax.ShapeDtypeStruct((depth, b, MEM_LEN, 2 * MEM_W), BF16),
        grid=(depth, b),
        in_specs=[pl.BlockSpec((1, MEM_LEN, D_MODEL), lambda d, bi: (bi, 0, 0)),
                  pl.BlockSpec((1, D_MODEL), lambda d, bi: (0, 0)),
                  pl.BlockSpec((1, D_MODEL, 2 * MEM_W), lambda d, bi: (d, 0, 0)),
                  pl.BlockSpec((1, 1, MEM_W), lambda d, bi: (d, 0, 0))],
        out_specs=pl.BlockSpec((1, 1, MEM_LEN, 2 * MEM_W), lambda d, bi: (d, bi, 0, 0)),
        compiler_params=_cparams(("parallel", "parallel")),
        name="memkv",
    )(mem, g, wkv, kng)


def _mem_attention(mq, qg, kv):
    qn = (mq * lax.rsqrt(_seg_mean_sq(mq) + EPS) * qg).astype(BF16)
    k = kv[:, :MEM_W]
    v = kv[:, MEM_W:]
    lane = lax.broadcasted_iota(jnp.int32, qn.shape, 1) >> 6
    out = jnp.zeros(qn.shape, F32)
    for h in range(MEM_HEADS):
        sel = lane == h
        qh = jnp.where(sel, qn, jnp.zeros_like(qn))
        s = lax.dot_general(qh, k, (((1,), (1,)), ((), ())), preferred_element_type=F32)
        p = jnp.exp(s - jnp.max(s, axis=1, keepdims=True))
        l = jnp.sum(p, axis=1, keepdims=True)
        o = jnp.dot(p.astype(BF16), v, preferred_element_type=F32)
        out = jnp.where(sel, o / l, out)
    return out


R_E1, R_E2, R_G1, R_G2, R_RANK1, R_RANK2 = range(6)


def _top2_route(logits, count_ref):
    tm = logits.shape[0]
    lane = lax.broadcasted_iota(jnp.int32, logits.shape, 1)
    lg = jnp.where(lane < N_EXPERTS, logits, NEG)
    m1 = jnp.max(lg, axis=1, keepdims=True)
    i1 = jnp.min(jnp.where(lg == m1, lane, LANES), axis=1, keepdims=True)
    lg2 = jnp.where(lane == i1, NEG, lg)
    m2 = jnp.max(lg2, axis=1, keepdims=True)
    i2 = jnp.min(jnp.where(lg2 == m2, lane, LANES), axis=1, keepdims=True)
    g1 = 1.0 / (1.0 + jnp.exp(m2 - m1))
    g2 = 1.0 - g1
    chosen = jnp.logical_or(lane == i1, lane == i2)
    onehot = jnp.where(chosen, 1.0, 0.0)
    r = lax.broadcasted_iota(jnp.int32, (tm, tm), 0)
    c = lax.broadcasted_iota(jnp.int32, (tm, tm), 1)
    before = jnp.where(c < r, 1.0, 0.0).astype(BF16)
    prefix = jnp.dot(before, onehot.astype(BF16), preferred_element_type=F32) + count_ref[...]
    count_ref[...] += jnp.sum(onehot, axis=0, keepdims=True)
    rank1 = jnp.sum(jnp.where(lane == i1, prefix, 0.0), axis=1, keepdims=True)
    rank2 = jnp.sum(jnp.where(lane == i2, prefix, 0.0), axis=1, keepdims=True)
    rec = jnp.zeros(logits.shape, F32)
    for idx, val in ((R_E1, i1.astype(F32)), (R_E2, i2.astype(F32)), (R_G1, g1), (R_G2, g2),
                     (R_RANK1, rank1), (R_RANK2, rank2)):
        rec = jnp.where(lane == idx, val, rec)
    return rec


def _mixout_kernel(*refs, with_router):
    if with_router:
        (tok_ref, mq_ref, kv_ref, qg_ref, wo_ref, x_ref, g2_ref, wr_ref,
         x1_ref, hp_ref, route_ref, count_ref) = refs
    else:
        tok_ref, mq_ref, kv_ref, qg_ref, wo_ref, x_ref, g2_ref, x1_ref, h2_ref = refs
    mo = _mem_attention(mq_ref[...].astype(F32), qg_ref[...], kv_ref[0])
    y = jnp.dot(tok_ref[...], wo_ref[:TOK_W, :], preferred_element_type=F32)
    y = y + jnp.dot(mo.astype(BF16), wo_ref[TOK_W:, :], preferred_element_type=F32)
    x1 = x_ref[...] + y
    x1_ref[...] = x1
    h2 = _rms(x1, g2_ref[...])
    if with_router:
        @pl.when(pl.program_id(0) == 0)
        def _():
            count_ref[...] = jnp.zeros_like(count_ref)

        hp_ref[...] = h2
        wr = wr_ref[...]
        w_hi = wr.astype(BF16)
        w_lo = (wr - w_hi.astype(F32)).astype(BF16)
        h_hi = h2.astype(BF16)
        h_lo = (h2 - h_hi.astype(F32)).astype(BF16)
        logits = (jnp.dot(h_hi, w_hi, preferred_element_type=F32) + jnp.dot(h_lo, w_hi, preferred_element_type=F32)
                  + jnp.dot(h_hi, w_lo, preferred_element_type=F32))
        route_ref[...] = _top2_route(logits, count_ref)
    else:
        h2_ref[...] = h2.astype(BF16)


def _mixout(tok, mq_src, mq_block, kv, qg, wo, x, g2, wr=None, tm=512):
    t = x.shape[0]
    tiles_per_batch = t // kv.shape[0] // tm
    with_router = wr is not None
    in_specs = [pl.BlockSpec((tm, TOK_W), lambda i: (i, 0)),
                pl.BlockSpec((tm, MEM_W), lambda i: (i, mq_block)),
                pl.BlockSpec((1, MEM_LEN, 2 * MEM_W), lambda i: (i // tiles_per_batch, 0, 0)),
                pl.BlockSpec((1, MEM_W), lambda i: (0, 0)),
                pl.BlockSpec((D_MODEL, D_MODEL), lambda i: (0, 0)),
                pl.BlockSpec((tm, D_MODEL), lambda i: (i, 0)),
                pl.BlockSpec((1, D_MODEL), lambda i: (0, 0))]
    args = [tok, mq_src, kv, qg, wo, x, g2]
    row_spec = lambda n: pl.BlockSpec((tm, n), lambda i: (i, 0))
    if with_router:
        in_specs.append(pl.BlockSpec((D_MODEL, LANES), lambda i: (0, 0)))
        args.append(wr)
        out_shape = [jax.ShapeDtypeStruct((t, D_MODEL), F32), jax.ShapeDtypeStruct((t, D_MODEL), F32),
                     jax.ShapeDtypeStruct((t, LANES), F32), jax.ShapeDtypeStruct((1, LANES), F32)]
        out_specs = [row_spec(D_MODEL), row_spec(D_MODEL), row_spec(LANES),
                     pl.BlockSpec((1, LANES), lambda i: (0, 0))]
    else:
        out_shape = [jax.ShapeDtypeStruct((t, D_MODEL), F32), jax.ShapeDtypeStruct((t, D_MODEL), BF16)]
        out_specs = [row_spec(D_MODEL), row_spec(D_MODEL)]
    return pl.pallas_call(
        functools.partial(_mixout_kernel, with_router=with_router),
        out_shape=out_shape,
        grid=(t // tm,),
        in_specs=in_specs,
        out_specs=out_specs,
        compiler_params=_cparams(("arbitrary",) if with_router else ("parallel",)),
        name="mixout_router" if with_router else "mixout",
    )(*args)


def _ffn_kernel(h_ref, x_ref, wg_ref, wu_ref, wd_ref, gn_ref, xo_ref, ho_ref, acc_ref):
    f = pl.program_id(1)

    @pl.when(f == 0)
    def _():
        acc_ref[...] = x_ref[...]

    h = h_ref[...]
    a = jnp.dot(h, wg_ref[...], preferred_element_type=F32)
    u = jnp.dot(h, wu_ref[...], preferred_element_type=F32)
    act = a * jax.nn.sigmoid(a) * u
    acc_ref[...] += jnp.dot(act.astype(BF16), wd_ref[...], preferred_element_type=F32)

    @pl.when(f == pl.num_programs(1) - 1)
    def _():
        xn = acc_ref[...]
        xo_ref[...] = xn
        ho_ref[...] = _rms(xn, gn_ref[...]).astype(BF16)


def _ffn(h, x, wg, wu, wd, gn, tm=512, tf=1408):
    t = x.shape[0]
    fdim = wg.shape[1]
    row = pl.BlockSpec((tm, D_MODEL), lambda i, f: (i, 0))
    return pl.pallas_call(
        _ffn_kernel,
        out_shape=[jax.ShapeDtypeStruct((t, D_MODEL), F32), jax.ShapeDtypeStruct((t, D_MODEL), BF16)],
        grid=(t // tm, fdim // tf),
        in_specs=[row, row,
                  pl.BlockSpec((D_MODEL, tf), lambda i, f: (0, f)),
                  pl.BlockSpec((D_MODEL, tf), lambda i, f: (0, f)),
                  pl.BlockSpec((tf, D_MODEL), lambda i, f: (f, 0)),
                  pl.BlockSpec((1, D_MODEL), lambda i, f: (0, 0))],
        out_specs=[row, row],
        scratch_shapes=[pltpu.VMEM((tm, D_MODEL), F32)],
        compiler_params=_cparams(("parallel", "arbitrary")),
        name="ffn",
    )(h, x, wg, wu, wd, gn)


def _row_copy(src_ref, src_row, dst_ref, dst_row, sem):
    return pltpu.make_async_copy(src_ref.at[pl.ds(src_row, 1), :], dst_ref.at[pl.ds(dst_row, 1), :], sem)


def _dispatch_kernel(pad_start_ref, pad_len_ref, pos_ref, hp_ref, xs_ref, zeros_ref, sem, zsem, *, tm, tmr):
    nbits = tmr.bit_length() - 1

    @pl.when(pl.program_id(0) == 0)
    def _():
        zeros_ref[...] = jnp.zeros_like(zeros_ref)
        for e in range(N_EXPERTS):
            start = pad_start_ref[e]
            length = pad_len_ref[e]
            singles = length & (SUBLANES - 1)
            for j in range(SUBLANES - 1):
                @pl.when(j < singles)
                def _(j=j, start=start):
                    cp = _row_copy(zeros_ref, 0, xs_ref, start + j, zsem)
                    cp.start()
                    cp.wait()

            done = start + singles
            for b in range(SUBLANES.bit_length() - 1, nbits):
                n = 1 << b
                bit = (length >> b) & 1

                @pl.when(bit == 1)
                def _(n=n, done=done):
                    cp = pltpu.make_async_copy(zeros_ref.at[pl.ds(0, n), :],
                                               xs_ref.at[pl.ds(pl.multiple_of(done, SUBLANES), n), :], zsem)
                    cp.start()
                    cp.wait()

                done = done + bit * n

        n_tiles = xs_ref.shape[0] // tmr
        half = tmr // 2
        for j in range(n_tiles - N_EXPERTS, n_tiles):
            @pl.when(j * tmr >= pad_start_ref[N_EXPERTS])
            def _(j=j):
                for c in range(2):
                    cp = pltpu.make_async_copy(zeros_ref, xs_ref.at[pl.ds(j * tmr + c * half, half), :], zsem)
                    cp.start()
                    cp.wait()

    def issue(r, carry):
        _row_copy(hp_ref, r, xs_ref, pos_ref[0, 0, r], sem).start(priority=0)
        _row_copy(hp_ref, r, xs_ref, pos_ref[0, 0, tm + r], sem).start(priority=1)
        return carry

    lax.fori_loop(0, tm, issue, 0, unroll=ISSUE_UNROLL)
    for _ in range(2):
        pltpu.make_async_copy(hp_ref, xs_ref.at[pl.ds(0, tm), :], sem).wait()


def _dispatch(pad_start, pad_len, pos, hp, n_rows, tm, tmr):
    t, w = hp.shape
    return pl.pallas_call(
        functools.partial(_dispatch_kernel, tm=tm, tmr=tmr),
        out_shape=jax.ShapeDtypeStruct((n_rows, w), hp.dtype),
        grid_spec=pltpu.PrefetchScalarGridSpec(
            num_scalar_prefetch=2,
            grid=(t // tm,),
            in_specs=[pl.BlockSpec((1, 1, 2 * tm), lambda i, ps, pn: (i, 0, 0), memory_space=pltpu.SMEM),
                      pl.BlockSpec((tm, w), lambda i, ps, pn: (i, 0))],
            out_specs=pl.BlockSpec(memory_space=pl.ANY),
            scratch_shapes=[pltpu.VMEM((tmr // 2, w), hp.dtype), pltpu.SemaphoreType.DMA,
                            pltpu.SemaphoreType.DMA]),
        compiler_params=_cparams(("arbitrary",)),
        name="moe_dispatch",
    )(pad_start, pad_len, pos, hp)


def _experts_kernel(te_ref, tv_ref, xs_ref, wg_ref, wu_ref, wd_ref, y_ref, h_ref):
    i = pl.program_id(0)
    f = pl.program_id(1)

    @pl.when(tv_ref[i] == 1)
    def _():
        @pl.when(f == 0)
        def _():
            h_ref[...] = xs_ref[...].astype(BF16)

        h = h_ref[...]
        a = jnp.dot(h, wg_ref[0], preferred_element_type=F32)
        u = jnp.dot(h, wu_ref[0], preferred_element_type=F32)
        act = a * jax.nn.sigmoid(a) * u
        contrib = jnp.dot(act.astype(BF16), wd_ref[0], preferred_element_type=F32)

        @pl.when(f == 0)
        def _():
            y_ref[...] = contrib

        @pl.when(f != 0)
        def _():
            y_ref[...] += contrib

    @pl.when(jnp.logical_and(tv_ref[i] == 0, f == 0))
    def _():
        y_ref[...] = jnp.zeros_like(y_ref)


def _experts(tile_expert, tile_valid, xs, wg, wu, wd, tmr, tf):
    n_rows, w = xs.shape
    fdim = wg.shape[2]
    nf = fdim // tf
    fidx = lambda i, f, te, tv: jnp.where(tv[i] == 1, f, nf - 1)
    return pl.pallas_call(
        _experts_kernel,
        out_shape=jax.ShapeDtypeStruct((n_rows, D_MODEL), F32),
        grid_spec=pltpu.PrefetchScalarGridSpec(
            num_scalar_prefetch=2,
            grid=(n_rows // tmr, nf),
            in_specs=[pl.BlockSpec((tmr, w), lambda i, f, te, tv: (jnp.where(tv[i] == 1, i, 0), 0)),
                      pl.BlockSpec((1, D_MODEL, tf), lambda i, f, te, tv: (te[i], 0, fidx(i, f, te, tv))),
                      pl.BlockSpec((1, D_MODEL, tf), lambda i, f, te, tv: (te[i], 0, fidx(i, f, te, tv))),
                      pl.BlockSpec((1, tf, D_MODEL), lambda i, f, te, tv: (te[i], fidx(i, f, te, tv), 0))],
            out_specs=pl.BlockSpec((tmr, D_MODEL), lambda i, f, te, tv: (i, 0)),
            scratch_shapes=[pltpu.VMEM((tmr, D_MODEL), BF16)]),
        compiler_params=_cparams(("parallel", "arbitrary")),
        name="moe_experts",
    )(tile_expert, tile_valid, xs, wg, wu, wd)


def _combine_kernel(pos_ref, route_ref, x_ref, y_ref, o_ref, buf_ref, sem, *, tm):
    def issue(r, carry):
        _row_copy(y_ref, pos_ref[0, 0, r], buf_ref.at[0], r, sem).start(priority=0)
        _row_copy(y_ref, pos_ref[0, 0, tm + r], buf_ref.at[1], r, sem).start(priority=1)
        return carry

    lax.fori_loop(0, tm, issue, 0, unroll=ISSUE_UNROLL)
    for k in range(2):
        pltpu.make_async_copy(y_ref.at[pl.ds(0, tm), :], buf_ref.at[k], sem).wait()
    route = route_ref[...]
    g1 = route[:, R_G1:R_G1 + 1]
    g2 = route[:, R_G2:R_G2 + 1]
    o_ref[...] = x_ref[...] + (g1 * buf_ref[0] + g2 * buf_ref[1])


def _combine(pos, route, x, y, tm):
    t = x.shape[0]
    return pl.pallas_call(
        functools.partial(_combine_kernel, tm=tm),
        out_shape=jax.ShapeDtypeStruct((t, D_MODEL), F32),
        grid=(t // tm,),
        in_specs=[pl.BlockSpec((1, 1, 2 * tm), lambda i: (i, 0, 0), memory_space=pltpu.SMEM),
                  pl.BlockSpec((tm, LANES), lambda i: (i, 0)),
                  pl.BlockSpec((tm, D_MODEL), lambda i: (i, 0)),
                  pl.BlockSpec(memory_space=pl.ANY)],
        out_specs=pl.BlockSpec((tm, D_MODEL), lambda i: (i, 0)),
        scratch_shapes=[pltpu.VMEM((2, tm, D_MODEL), F32), pltpu.SemaphoreType.DMA],
        compiler_params=_cparams(("arbitrary",)),
        name="moe_combine",
    )(pos, route, x, y)


def _moe(hp, route, counts, x, wg, wu, wd, tm=512, tmr=512, tf=1792):
    t = hp.shape[0]
    n_tiles = 2 * t // tmr + N_EXPERTS
    n_rows = n_tiles * tmr
    cnt = counts[0, :N_EXPERTS].astype(jnp.int32)
    padded = (cnt + tmr - 1) // tmr * tmr
    ends = jnp.cumsum(padded)
    offs = ends - padded
    e1 = route[:, R_E1].astype(jnp.int32)
    e2 = route[:, R_E2].astype(jnp.int32)
    pos1 = offs[e1] + route[:, R_RANK1].astype(jnp.int32)
    pos2 = offs[e2] + route[:, R_RANK2].astype(jnp.int32)
    pos = jnp.concatenate([pos1.reshape(t // tm, 1, tm), pos2.reshape(t // tm, 1, tm)], axis=2)
    tile_start = jnp.arange(n_tiles, dtype=jnp.int32) * tmr
    tile_valid = (tile_start < ends[-1]).astype(jnp.int32)
    last_tile = ends[-1] // tmr - 1
    clamped = jnp.minimum(tile_start, last_tile * tmr)
    tile_expert = jnp.sum((clamped[:, None] >= ends[None, :]).astype(jnp.int32), axis=1)
    pad_start = jnp.concatenate([offs + cnt, ends[-1:]])
    xs = _dispatch(pad_start, padded - cnt, pos, hp, n_rows, tm, tmr)
    y = _experts(tile_expert, tile_valid, xs, wg, wu, wd, tmr, tf)
    return _combine(pos, route, x, y, tm)


def _inproj1_kernel(x_ref, wz_ref, wx_ref, wm_ref, wdt_ref, z_ref, xbc_ref, mq_ref, dt_ref):
    h = x_ref[...]
    z_ref[...] = jnp.dot(h, wz_ref[...], preferred_element_type=F32).astype(BF16)
    xbc_ref[...] = jnp.dot(h, wx_ref[...], preferred_element_type=F32).astype(BF16)
    mq_ref[...] = jnp.dot(h, wm_ref[...], preferred_element_type=F32).astype(BF16)
    dt_ref[...] = jnp.dot(h, wdt_ref[...], preferred_element_type=F32)


def _inproj1(h, wz, wx, wm, wdt, tm=512):
    t = h.shape[0]
    full = lambda a: pl.BlockSpec(a.shape, lambda i: (0, 0))
    row = lambda n: pl.BlockSpec((tm, n), lambda i: (i, 0))
    return pl.pallas_call(
        _inproj1_kernel,
        out_shape=[jax.ShapeDtypeStruct((t, TOK_W), BF16), jax.ShapeDtypeStruct((t, SSM_CONV_DIM), BF16),
                   jax.ShapeDtypeStruct((t, MEM_W), BF16), jax.ShapeDtypeStruct((t, LANES), F32)],
        grid=(t // tm,),
        in_specs=[row(D_MODEL), full(wz), full(wx), full(wm), full(wdt)],
        out_specs=[row(TOK_W), row(SSM_CONV_DIM), row(MEM_W), row(LANES)],
        compiler_params=_cparams(("parallel",)),
        name="inproj1",
    )(h, wz, wx, wm, wdt)


def _split_dot(a, b, terms, split_rhs):
    rem = b if split_rhs else a
    out = None
    for _ in range(terms):
        piece = rem.astype(BF16)
        part = (jnp.dot(a, piece, preferred_element_type=F32) if split_rhs
                else jnp.dot(piece, b, preferred_element_type=F32))
        out = part if out is None else out + part
        rem = rem - piece.astype(F32)
    return out


def _ssd_kernel(z_ref, xbc_ref, dt_ref, cw_ref, cb_ref, dtb_ref, alog_ref, dsk_ref, ng_ref, ex_ref,
                o_ref, xe_ref, st_ref, *, tl):
    j = pl.program_id(1)
    pad = 8

    @pl.when(j == 0)
    def _():
        xe_ref[0:pad, :] = jnp.zeros((pad, SSM_CONV_DIM), F32)
        st_ref[...] = jnp.zeros_like(st_ref)

    x = xbc_ref[0].astype(F32)
    xe_ref[pad:pad + tl, :] = x
    acc = cb_ref[...] + cw_ref[SSM_CONV_K - 1:SSM_CONV_K, :] * x
    for k in range(SSM_CONV_K - 1):
        sh = SSM_CONV_K - 1 - k
        acc = acc + cw_ref[k:k + 1, :] * xe_ref[pad - sh:pad - sh + tl, :]
    xe_ref[0:pad, :] = x[tl - pad:tl, :]
    xc = acc * jax.nn.sigmoid(acc)
    xs = xc[:, :TOK_W]

    dt_in = dt_ref[0] + dtb_ref[...]
    dt = jnp.maximum(dt_in, 0.0) + jnp.log1p(jnp.exp(-jnp.abs(dt_in)))
    a = -jnp.exp(alog_ref[...])
    dta = dt * a
    r = lax.broadcasted_iota(jnp.int32, (tl, tl), 0)
    c = lax.broadcasted_iota(jnp.int32, (tl, tl), 1)
    tril = r >= c
    ltri = jnp.where(tril, 1.0, 0.0).astype(BF16)
    cum = _split_dot(ltri, dta, terms=3, split_rhs=True)
    cum_t = cum.T
    dt_t = dt.T
    lane = lax.broadcasted_iota(jnp.int32, (tl, LANES), 1)

    hg = SSM_HEADS // SSM_GROUPS
    y_pairs = []
    cbs = []
    for g in range(SSM_GROUPS):
        bm = xc[:, TOK_W + g * SSM_STATE:TOK_W + (g + 1) * SSM_STATE].astype(BF16)
        cm = xc[:, TOK_W + (SSM_GROUPS + g) * SSM_STATE:TOK_W + (SSM_GROUPS + g + 1) * SSM_STATE].astype(BF16)
        cbs.append(lax.dot_general(cm, bm, (((1,), (1,)), ((), ())), preferred_element_type=F32))
    for pr in range(SSM_HEADS // 2):
        xp = xs[:, pr * LANES:(pr + 1) * LANES].astype(BF16)
        ys = []
        for hh in (2 * pr, 2 * pr + 1):
            g = hh // hg
            seg = cum[:, hh:hh + 1] - cum_t[hh:hh + 1, :]
            decay = jnp.exp(jnp.where(tril, seg, NEG))
            w = cbs[g] * decay * dt_t[hh:hh + 1, :]
            ys.append(jnp.dot(w.astype(BF16), xp, preferred_element_type=F32))
        y_pairs.append(jnp.where(lane < SSM_HD, ys[0], ys[1]))
    y = jnp.concatenate(y_pairs, axis=1)

    expcum = jnp.exp(cum)
    to_end = jnp.exp(cum[tl - 1:tl, :] - cum) * dt
    stacked = jnp.concatenate([expcum, to_end], axis=0)
    exd = _split_dot(stacked, ex_ref[...], terms=2, split_rhs=False)
    expcum_x = exd[:tl]
    xw = (xs * exd[tl:]).astype(BF16)
    gw = TOK_W // SSM_GROUPS
    y_off = []
    for g in range(SSM_GROUPS):
        cs = slice(g * gw, (g + 1) * gw)
        bm_t = xc[:, TOK_W + g * SSM_STATE:TOK_W + (g + 1) * SSM_STATE].T.astype(BF16)
        cm = xc[:, TOK_W + (SSM_GROUPS + g) * SSM_STATE:TOK_W + (SSM_GROUPS + g + 1) * SSM_STATE].astype(BF16)
        sg = st_ref[:, cs]
        y_off.append(jnp.dot(cm, sg.astype(BF16), preferred_element_type=F32) * expcum_x[:, cs])
        st_ref[:, cs] = sg * expcum_x[tl - 1:tl, cs] + jnp.dot(bm_t, xw[:, cs], preferred_element_type=F32)
    y = y + jnp.concatenate(y_off, axis=1) + dsk_ref[...] * xs
    zf = z_ref[0].astype(F32)
    y = y * (zf * jax.nn.sigmoid(zf))
    outs = []
    for g in range(SSM_GROUPS):
        cs = slice(g * gw, (g + 1) * gw)
        outs.append(_rms(y[:, cs], ng_ref[:, cs]))
    o_ref[0] = jnp.concatenate(outs, axis=1).astype(o_ref.dtype)


def _ssd(z3, xbc3, dt3, cw, cb, dtb, alog, dsk, ng, ex, tl=256):
    b, s, _ = z3.shape
    full = lambda a: pl.BlockSpec(a.shape, lambda bi, j: (0, 0))
    blk = lambda n: pl.BlockSpec((1, tl, n), lambda bi, j: (bi, j, 0))
    return pl.pallas_call(
        functools.partial(_ssd_kernel, tl=tl),
        out_shape=jax.ShapeDtypeStruct((b, s, TOK_W), BF16),
        grid=(b, s // tl),
        in_specs=[blk(TOK_W), blk(SSM_CONV_DIM), blk(LANES), full(cw), full(cb), full(dtb), full(alog),
                  full(dsk), full(ng), full(ex)],
        out_specs=blk(TOK_W),
        scratch_shapes=[pltpu.VMEM((tl + 8, SSM_CONV_DIM), F32), pltpu.VMEM((SSM_STATE, TOK_W), F32)],
        compiler_params=_cparams(("parallel", "arbitrary")),
        name="conv_ssd",
    )(z3, xbc3, dt3, cw, cb, dtb, alog, dsk, ng, ex)


def _pad_lanes(v, n=LANES):
    return jnp.pad(v, [(0, 0)] * (v.ndim - 1) + [(0, n - v.shape[-1])])


def kernel(x, mem, ln1_g, ln2_g, mem_norm_g, w_out, mem_w_kv, mem_qn_g, mem_kn_g, da_w_in, da_qn_g, da_kn_g,
           da_lq1, da_lk1, da_lq2, da_lk2, da_sub_g, ssm_w_in, ssm_conv_w, ssm_conv_b, ssm_dt_bias, ssm_a_log,
           ssm_d, ssm_norm_g, ffn_w_gate, ffn_w_up, ffn_w_down, moe_w_router, moe_w_gate, moe_w_up, moe_w_down):
    b, s, d = x.shape
    t = b * s
    row = lambda v: v.reshape(1, -1).astype(F32)
    xt = x.reshape(t, d)

    mem_qg = jnp.tile(mem_qn_g.astype(F32) * (MEM_HD ** -0.5), (1, MEM_HEADS))
    mem_kg = jnp.tile(mem_kn_g.astype(F32), (1, MEM_HEADS))[:, None, :]
    kv = _memkv(mem, row(mem_norm_g), mem_w_kv.astype(BF16), mem_kg)
    wo = w_out.astype(BF16)

    lambda_init = 0.8 - 0.6 * math.exp(-0.3 * 0)
    qkg = jnp.concatenate([jnp.tile(da_qn_g[0].astype(F32) * (DA_DH ** -0.5 * LOG2E), 2 * DA_HEADS),
                           jnp.tile(da_kn_g[0].astype(F32), 2 * DA_HEADS)]).reshape(1, -1)
    qt, kn, vt, mq0 = _inproj0(xt, row(ln1_g[0]), da_w_in[0].astype(BF16), qkg, b)
    tok = _diff_attention(qt, kn.reshape(b, s, TOK_W), vt, row(da_lq1[0]), row(da_lk1[0]), row(da_lq2[0]),
                          row(da_lk2[0]), row(da_sub_g[0]), lambda_init)
    x1, h2 = _mixout(tok.reshape(t, TOK_W), mq0, 0, kv[0], mem_qg[0:1], wo[0], xt, row(ln2_g[0]))
    x2, h3 = _ffn(h2, x1, ffn_w_gate[0].astype(BF16), ffn_w_up[0].astype(BF16), ffn_w_down[0].astype(BF16),
                  row(ln1_g[1]))

    w_in = ssm_w_in[0]
    o1 = TOK_W
    o2 = o1 + SSM_CONV_DIM
    o3 = o2 + SSM_HEADS
    z, xbc, mq, dt = _inproj1(h3, w_in[:, :o1].astype(BF16), w_in[:, o1:o2].astype(BF16),
                              w_in[:, o3:].astype(BF16), _pad_lanes(w_in[:, o2:o3]).astype(BF16))
    expand = jnp.repeat(jnp.eye(SSM_HEADS, dtype=F32), SSM_HD, axis=1)
    expand = jnp.pad(expand, ((0, LANES - SSM_HEADS), (0, 0))).astype(BF16)
    dsk = jnp.repeat(ssm_d[0].astype(F32), SSM_HD).reshape(1, -1)
    tok1 = _ssd(z.reshape(b, s, -1), xbc.reshape(b, s, -1), dt.reshape(b, s, -1),
                ssm_conv_w[0].astype(F32), row(ssm_conv_b[0]), _pad_lanes(row(ssm_dt_bias[0])),
                _pad_lanes(row(ssm_a_log[0])), dsk, row(ssm_norm_g[0]), expand)
    x3, hp, route, counts = _mixout(tok1.reshape(t, TOK_W), mq, 0, kv[1], mem_qg[1:2], wo[1], x2, row(ln2_g[1]),
                                    wr=_pad_lanes(moe_w_router[0].astype(F32)))
    x4 = _moe(hp, route, counts, x3, moe_w_gate[0].astype(BF16), moe_w_up[0].astype(BF16),
              moe_w_down[0].astype(BF16))
    return x4.reshape(b, s, d)
```

```python
import functools
import math

import jax
import jax.numpy as jnp
from jax import lax
from jax.experimental import pallas as pl
from jax.experimental.pallas import tpu as pltpu

F32 = jnp.float32
BF16 = jnp.bfloat16

D_MODEL = 1024
CHUNK = 64
MEM_LEN = 256
MEM_W = 256
MEM_HEADS = 4
MEM_HD = 64
TOK_W = 768
DA_DH = 64
DA_HEADS = 6
SSM_HD = 64
SSM_HEADS = 12
SSM_GROUPS = 2
SSM_STATE = 128
SSM_CONV_K = 4
SSM_CONV_DIM = 1280
N_EXPERTS = 8
EPS = 1e-6
LANES = 128
NEG = -1e30
SUBLANES = 8
ONES_ROWS = SUBLANES
LOG2E = math.log2(math.e)
ISSUE_UNROLL = True
VMEM_LIMIT = 56 * 1024 * 1024


def _cparams(sem):
    return pltpu.CompilerParams(dimension_semantics=sem, vmem_limit_bytes=VMEM_LIMIT)


def _rms(xf, g):
    ms = jnp.mean(xf * xf, axis=-1, keepdims=True)
    return xf * lax.rsqrt(ms + EPS) * g


def _seg_mean_matrix(n, seg_shift):
    r = lax.broadcasted_iota(jnp.int32, (n, n), 0) >> seg_shift
    c = lax.broadcasted_iota(jnp.int32, (n, n), 1) >> seg_shift
    return jnp.where(r == c, 1.0 / (1 << seg_shift), 0.0).astype(BF16)


def _seg_mean_sq(y, seg_shift=6):
    bd = _seg_mean_matrix(256, seg_shift)
    sq = (y * y).astype(BF16)
    parts = [jnp.dot(sq[:, c:c + 256], bd, preferred_element_type=F32)
             for c in range(0, y.shape[1], 256)]
    return parts[0] if len(parts) == 1 else jnp.concatenate(parts, axis=1)


def _inproj0_kernel(x_ref, g_ref, w_ref, qkg_ref, qt_ref, k_ref, vt_ref, mq_ref):
    h = _rms(x_ref[...], g_ref[...]).astype(BF16)
    u = jnp.dot(h, w_ref[...], preferred_element_type=F32)
    tm = u.shape[0]
    hd = 2 * DA_DH
    nqk = 2 * TOK_W
    qk = u[:, :nqk]
    qkn = qk * lax.rsqrt(_seg_mean_sq(qk) + EPS) * qkg_ref[...]
    qt_ref[0] = qkn[:, :TOK_W].T.reshape(DA_HEADS, hd, tm).astype(BF16)
    k_ref[...] = qkn[:, TOK_W:].astype(BF16)
    vt_ref[0, :, 0, :hd, :] = u[:, nqk:nqk + TOK_W].T.reshape(DA_HEADS, hd, tm).astype(BF16)
    vt_ref[0, :, 0, hd:, :] = jnp.ones((DA_HEADS, ONES_ROWS, tm), BF16)
    mq_ref[...] = u[:, nqk + TOK_W:].astype(BF16)


def _inproj0(x, g, w, qkg, b, tm=512):
    t = x.shape[0]
    n = w.shape[1]
    s = t // b
    nt = s // tm
    hd = 2 * DA_DH
    return pl.pallas_call(
        _inproj0_kernel,
        out_shape=[jax.ShapeDtypeStruct((b, DA_HEADS, hd, s), BF16),
                   jax.ShapeDtypeStruct((t, TOK_W), BF16),
                   jax.ShapeDtypeStruct((b, DA_HEADS, nt, hd + ONES_ROWS, tm), BF16),
                   jax.ShapeDtypeStruct((t, MEM_W), BF16)],
        grid=(t // tm,),
        in_specs=[pl.BlockSpec((tm, D_MODEL), lambda i: (i, 0)),
                  pl.BlockSpec((1, D_MODEL), lambda i: (0, 0)),
                  pl.BlockSpec((D_MODEL, n), lambda i: (0, 0)),
                  pl.BlockSpec((1, 2 * TOK_W), lambda i: (0, 0))],
        out_specs=[pl.BlockSpec((1, DA_HEADS, hd, tm), lambda i: (i // nt, 0, 0, i % nt)),
                   pl.BlockSpec((tm, TOK_W), lambda i: (i, 0)),
                   pl.BlockSpec((1, DA_HEADS, 1, hd + ONES_ROWS, tm), lambda i: (i // nt, 0, i % nt, 0, 0)),
                   pl.BlockSpec((tm, MEM_W), lambda i: (i, 0))],
        compiler_params=_cparams(("parallel",)),
        name="inproj0",
    )(x, g, w, qkg)


def _attn_kernel(lq1_ref, lk1_ref, lq2_ref, lk2_ref, subg_ref, qt_ref, k_ref, vt_ref, o_ref,
                 acc_ref, m_ref, alpha_ref, s_ref, p_ref, *, tq, tk, nh, lambda_init):
    qi = pl.program_id(2)
    hd = 2 * DA_DH
    qs = []
    for hs in range(nh):
        qt = qt_ref[0, hs]
        row = lax.broadcasted_iota(jnp.int32, qt.shape, 0)
        zero = jnp.zeros_like(qt)
        qs += [jnp.where(row < DA_DH, qt, zero), jnp.where(row >= DA_DH, qt, zero)]
    acc_ref[...] = jnp.zeros_like(acc_ref)
    m_ref[...] = jnp.full_like(m_ref, NEG)
    p_ref[1] = jnp.zeros(p_ref.shape[1:], p_ref.dtype)
    alpha_ref[1] = jnp.ones(alpha_ref.shape[1:], alpha_ref.dtype)

    def scores(ki, slot):
        start = pl.multiple_of(ki * tk, tk)
        for hs in range(nh):
            k = k_ref[0, pl.ds(start, tk), hs * hd:(hs + 1) * hd]
            for i in range(2):
                c = 2 * hs + i
                s_ref[slot, c] = jnp.dot(k, qs[c], preferred_element_type=F32)

    def softmax(slot, masked):
        if masked:
            kc = lax.broadcasted_iota(jnp.int32, (tk, tq), 0) >> 6
            qc = lax.broadcasted_iota(jnp.int32, (tk, tq), 1) >> 6
            vis = kc <= qc
        for c in range(2 * nh):
            s = s_ref[slot, c]
            if masked:
                s = jnp.where(vis, s, NEG)
            m_old = m_ref[c]
            m_new = jnp.maximum(m_old, jnp.max(s, axis=0, keepdims=True))
            alpha_ref[slot, c] = jnp.exp2(m_old - m_new)
            p_ref[slot, c] = jnp.exp2(s - m_new).astype(BF16)
            m_ref[c] = m_new

    def values(ki, slot):
        for hs in range(nh):
            vt = vt_ref[0, hs, ki]
            for i in range(2):
                c = 2 * hs + i
                acc_ref[c] = alpha_ref[slot, c] * acc_ref[c] + jnp.dot(vt, p_ref[slot, c],
                                                                        preferred_element_type=F32)

    odd = (qi & 1) == 1

    @pl.when(odd)
    def _():
        scores(0, 1)
        scores(1, 0)
        softmax(1, False)

    @pl.when(jnp.logical_not(odd))
    def _():
        scores(0, 0)

    def body(j, carry):
        ki = (qi & 1) + 2 * j
        scores(ki + 1, 1)
        softmax(0, False)
        values(jnp.maximum(ki - 1, 0), 1)
        scores(ki + 2, 0)
        softmax(1, False)
        values(ki, 0)
        return carry

    lax.fori_loop(0, qi >> 1, body, 0)
    softmax(0, True)
    values(jnp.maximum(qi - 1, 0), 1)
    values(qi, 0)

    lam = (jnp.exp(jnp.sum(lq1_ref[...] * lk1_ref[...])) - jnp.exp(jnp.sum(lq2_ref[...] * lk2_ref[...]))
           + lambda_init)
    for hs in range(nh):
        o1 = acc_ref[2 * hs, :hd, :] / acc_ref[2 * hs, hd:hd + 1, :]
        o2 = acc_ref[2 * hs + 1, :hd, :] / acc_ref[2 * hs + 1, hd:hd + 1, :]
        ot = o1 - lam * o2
        ms = jnp.mean(ot * ot, axis=0, keepdims=True)
        ot = ot * lax.rsqrt(ms + EPS)
        o_ref[0, :, hs * hd:(hs + 1) * hd] = (ot.T * (subg_ref[...] * (1.0 - lambda_init))).astype(o_ref.dtype)


def _diff_attention(qt, k3, vt, lq1, lk1, lq2, lk2, subg, lambda_init, nh=1):
    b, s, _ = k3.shape
    tk = vt.shape[-1]
    tq = tk
    nq = s // tq
    hd = 2 * DA_DH
    hde = hd + ONES_ROWS
    nc = 2 * nh
    vec = lambda n: pl.BlockSpec((1, n), lambda bi, h, qi: (0, 0))
    kern = functools.partial(_attn_kernel, tq=tq, tk=tk, nh=nh, lambda_init=lambda_init)
    return pl.pallas_call(
        kern,
        out_shape=jax.ShapeDtypeStruct((b, s, TOK_W), BF16),
        grid=(b, DA_HEADS // nh, nq),
        in_specs=[vec(DA_DH), vec(DA_DH), vec(DA_DH), vec(DA_DH), vec(hd),
                  pl.BlockSpec((1, nh, hd, tq), lambda bi, h, qi: (bi, h, 0, qi)),
                  pl.BlockSpec((1, s, nh * hd), lambda bi, h, qi: (bi, 0, h)),
                  pl.BlockSpec((1, nh, s // tk, hde, tk), lambda bi, h, qi: (bi, h, 0, 0, 0))],
        out_specs=pl.BlockSpec((1, tq, nh * hd), lambda bi, h, qi: (bi, qi, h)),
        scratch_shapes=[pltpu.VMEM((nc, hde, tq), F32), pltpu.VMEM((nc, 1, tq), F32),
                        pltpu.VMEM((2, nc, 1, tq), F32), pltpu.VMEM((2, nc, tk, tq), F32),
                        pltpu.VMEM((2, nc, tk, tq), BF16)],
        compiler_params=_cparams(("parallel", "parallel", "parallel")),
        name="diff_attn",
    )(lq1, lk1, lq2, lk2, subg, qt, k3, vt)


def _memkv_kernel(mem_ref, g_ref, wkv_ref, kng_ref, kv_ref):
    mn = _rms(mem_ref[0], g_ref[...]).astype(BF16)
    kv = jnp.dot(mn, wkv_ref[0], preferred_element_type=F32)
    k = kv[:, :MEM_W]
    kn = k * lax.rsqrt(_seg_mean_sq(k) + EPS) * kng_ref[0]
    kv_ref[0, 0, :, :MEM_W] = kn.astype(BF16)
    kv_ref[0, 0, :, MEM_W:] = kv[:, MEM_W:].astype(BF16)


def _memkv(mem, g, wkv, kng):
    depth = wkv.shape[0]
    b = mem.shape[0]
    return pl.pallas_call(
        _memkv_kernel,
        out_shape=jax.ShapeDtypeStruct((depth, b, MEM_LEN, 2 * MEM_W), BF16),
        grid=(depth, b),
        in_specs=[pl.BlockSpec((1, MEM_LEN, D_MODEL), lambda d, bi: (bi, 0, 0)),
                  pl.BlockSpec((1, D_MODEL), lambda d, bi: (0, 0)),
                  pl.BlockSpec((1, D_MODEL, 2 * MEM_W), lambda d, bi: (d, 0, 0)),
                  pl.BlockSpec((1, 1, MEM_W), lambda d, bi: (d, 0, 0))],
        out_specs=pl.BlockSpec((1, 1, MEM_LEN, 2 * MEM_W), lambda d, bi: (d, bi, 0, 0)),
        compiler_params=_cparams(("parallel", "parallel")),
        name="memkv",
    )(mem, g, wkv, kng)


def _mem_attention(mq, qg, kv):
    qn = (mq * lax.rsqrt(_seg_mean_sq(mq) + EPS) * qg).astype(BF16)
    k = kv[:, :MEM_W]
    v = kv[:, MEM_W:]
    lane = lax.broadcasted_iota(jnp.int32, qn.shape, 1) >> 6
    out = jnp.zeros(qn.shape, F32)
    for h in range(MEM_HEADS):
        sel = lane == h
        qh = jnp.where(sel, qn, jnp.zeros_like(qn))
        s = lax.dot_general(qh, k, (((1,), (1,)), ((), ())), preferred_element_type=F32)
        p = jnp.exp(s - jnp.max(s, axis=1, keepdims=True))
        l = jnp.sum(p, axis=1, keepdims=True)
        o = jnp.dot(p.astype(BF16), v, preferred_element_type=F32)
        out = jnp.where(sel, o / l, out)
    return out


R_E1, R_E2, R_G1, R_G2, R_RANK1, R_RANK2 = range(6)


def _top2_route(logits, count_ref):
    tm = logits.shape[0]
    lane = lax.broadcasted_iota(jnp.int32, logits.shape, 1)
    lg = jnp.where(lane < N_EXPERTS, logits, NEG)
    m1 = jnp.max(lg, axis=1, keepdims=True)
    i1 = jnp.min(jnp.where(lg == m1, lane, LANES), axis=1, keepdims=True)
    lg2 = jnp.where(lane == i1, NEG, lg)
    m2 = jnp.max(lg2, axis=1, keepdims=True)
    i2 = jnp.min(jnp.where(lg2 == m2, lane, LANES), axis=1, keepdims=True)
    g1 = 1.0 / (1.0 + jnp.exp(m2 - m1))
    g2 = 1.0 - g1
    chosen = jnp.logical_or(lane == i1, lane == i2)
    onehot = jnp.where(chosen, 1.0, 0.0)
    r = lax.broadcasted_iota(jnp.int32, (tm, tm), 0)
    c = lax.broadcasted_iota(jnp.int32, (tm, tm), 1)
    before = jnp.where(c < r, 1.0, 0.0).astype(BF16)
    prefix = jnp.dot(before, onehot.astype(BF16), preferred_element_type=F32) + count_ref[...]
    count_ref[...] += jnp.sum(onehot, axis=0, keepdims=True)
    rank1 = jnp.sum(jnp.where(lane == i1, prefix, 0.0), axis=1, keepdims=True)
    rank2 = jnp.sum(jnp.where(lane == i2, prefix, 0.0), axis=1, keepdims=True)
    rec = jnp.zeros(logits.shape, F32)
    for idx, val in ((R_E1, i1.astype(F32)), (R_E2, i2.astype(F32)), (R_G1, g1), (R_G2, g2),
                     (R_RANK1, rank1), (R_RANK2, rank2)):
        rec = jnp.where(lane == idx, val, rec)
    return rec


def _mixout_kernel(*refs, with_router):
    if with_router:
        (tok_ref, mq_ref, kv_ref, qg_ref, wo_ref, x_ref, g2_ref, wr_ref,
         x1_ref, hp_ref, route_ref, idx_ref, count_ref) = refs
    else:
        tok_ref, mq_ref, kv_ref, qg_ref, wo_ref, x_ref, g2_ref, x1_ref, h2_ref = refs
    mo = _mem_attention(mq_ref[...].astype(F32), qg_ref[...], kv_ref[0])
    y = jnp.dot(tok_ref[...], wo_ref[:TOK_W, :], preferred_element_type=F32)
    y = y + jnp.dot(mo.astype(BF16), wo_ref[TOK_W:, :], preferred_element_type=F32)
    x1 = x_ref[...] + y
    x1_ref[...] = x1
    h2 = _rms(x1, g2_ref[...])
    if with_router:
        @pl.when(pl.program_id(0) == 0)
        def _():
            count_ref[...] = jnp.zeros_like(count_ref)

        hp_ref[...] = h2
        wr = wr_ref[...]
        w_hi = wr.astype(BF16)
        w_lo = (wr - w_hi.astype(F32)).astype(BF16)
        h_hi = h2.astype(BF16)
        h_lo = (h2 - h_hi.astype(F32)).astype(BF16)
        logits = (jnp.dot(h_hi, w_hi, preferred_element_type=F32) + jnp.dot(h_lo, w_hi, preferred_element_type=F32)
                  + jnp.dot(h_hi, w_lo, preferred_element_type=F32))
        rec = _top2_route(logits, count_ref)
        route_ref[...] = rec
        idx_ref[0] = rec.T[:SUBLANES, :].astype(jnp.int32)
    else:
        h2_ref[...] = h2.astype(BF16)


def _mixout(tok, mq_src, mq_block, kv, qg, wo, x, g2, wr=None, tm=512):
    t = x.shape[0]
    tiles_per_batch = t // kv.shape[0] // tm
    with_router = wr is not None
    in_specs = [pl.BlockSpec((tm, TOK_W), lambda i: (i, 0)),
                pl.BlockSpec((tm, MEM_W), lambda i: (i, mq_block)),
                pl.BlockSpec((1, MEM_LEN, 2 * MEM_W), lambda i: (i // tiles_per_batch, 0, 0)),
                pl.BlockSpec((1, MEM_W), lambda i: (0, 0)),
                pl.BlockSpec((D_MODEL, D_MODEL), lambda i: (0, 0)),
                pl.BlockSpec((tm, D_MODEL), lambda i: (i, 0)),
                pl.BlockSpec((1, D_MODEL), lambda i: (0, 0))]
    args = [tok, mq_src, kv, qg, wo, x, g2]
    row_spec = lambda n: pl.BlockSpec((tm, n), lambda i: (i, 0))
    if with_router:
        in_specs.append(pl.BlockSpec((D_MODEL, LANES), lambda i: (0, 0)))
        args.append(wr)
        out_shape = [jax.ShapeDtypeStruct((t, D_MODEL), F32), jax.ShapeDtypeStruct((t, D_MODEL), F32),
                     jax.ShapeDtypeStruct((t, LANES), F32), jax.ShapeDtypeStruct((t // tm, SUBLANES, tm), jnp.int32),
                     jax.ShapeDtypeStruct((1, LANES), F32)]
        out_specs = [row_spec(D_MODEL), row_spec(D_MODEL), row_spec(LANES),
                     pl.BlockSpec((1, SUBLANES, tm), lambda i: (i, 0, 0)),
                     pl.BlockSpec((1, LANES), lambda i: (0, 0))]
    else:
        out_shape = [jax.ShapeDtypeStruct((t, D_MODEL), F32), jax.ShapeDtypeStruct((t, D_MODEL), BF16)]
        out_specs = [row_spec(D_MODEL), row_spec(D_MODEL)]
    return pl.pallas_call(
        functools.partial(_mixout_kernel, with_router=with_router),
        out_shape=out_shape,
        grid=(t // tm,),
        in_specs=in_specs,
        out_specs=out_specs,
        compiler_params=_cparams(("arbitrary",) if with_router else ("parallel",)),
        name="mixout_router" if with_router else "mixout",
    )(*args)


def _ffn_kernel(h_ref, x_ref, wg_ref, wu_ref, wd_ref, gn_ref, xo_ref, ho_ref, acc_ref):
    f = pl.program_id(1)

    @pl.when(f == 0)
    def _():
        acc_ref[...] = x_ref[...]

    h = h_ref[...]
    a = jnp.dot(h, wg_ref[...], preferred_element_type=F32)
    u = jnp.dot(h, wu_ref[...], preferred_element_type=F32)
    act = a * jax.nn.sigmoid(a) * u
    acc_ref[...] += jnp.dot(act.astype(BF16), wd_ref[...], preferred_element_type=F32)

    @pl.when(f == pl.num_programs(1) - 1)
    def _():
        xn = acc_ref[...]
        xo_ref[...] = xn
        ho_ref[...] = _rms(xn, gn_ref[...]).astype(BF16)


def _ffn(h, x, wg, wu, wd, gn, tm=512, tf=1408):
    t = x.shape[0]
    fdim = wg.shape[1]
    row = pl.BlockSpec((tm, D_MODEL), lambda i, f: (i, 0))
    return pl.pallas_call(
        _ffn_kernel,
        out_shape=[jax.ShapeDtypeStruct((t, D_MODEL), F32), jax.ShapeDtypeStruct((t, D_MODEL), BF16)],
        grid=(t // tm, fdim // tf),
        in_specs=[row, row,
                  pl.BlockSpec((D_MODEL, tf), lambda i, f: (0, f)),
                  pl.BlockSpec((D_MODEL, tf), lambda i, f: (0, f)),
                  pl.BlockSpec((tf, D_MODEL), lambda i, f: (f, 0)),
                  pl.BlockSpec((1, D_MODEL), lambda i, f: (0, 0))],
        out_specs=[row, row],
        scratch_shapes=[pltpu.VMEM((tm, D_MODEL), F32)],
        compiler_params=_cparams(("parallel", "arbitrary")),
        name="ffn",
    )(h, x, wg, wu, wd, gn)


def _row_copy(src_ref, src_row, dst_ref, dst_row, sem):
    return pltpu.make_async_copy(src_ref.at[pl.ds(src_row, 1), :], dst_ref.at[pl.ds(dst_row, 1), :], sem)


def _dispatch_kernel(pad_start_ref, pad_len_ref, pos_ref, hp_ref, xs_ref, zeros_ref, sem, zsem, *, tm, tmr):
    nbits = tmr.bit_length() - 1

    @pl.when(pl.program_id(0) == 0)
    def _():
        zeros_ref[...] = jnp.zeros_like(zeros_ref)
        for e in range(N_EXPERTS):
            start = pad_start_ref[e]
            length = pad_len_ref[e]
            singles = length & (SUBLANES - 1)
            for j in range(SUBLANES - 1):
                @pl.when(j < singles)
                def _(j=j, start=start):
                    cp = _row_copy(zeros_ref, 0, xs_ref, start + j, zsem)
                    cp.start()
                    cp.wait()

            done = start + singles
            for b in range(SUBLANES.bit_length() - 1, nbits):
                n = 1 << b
                bit = (length >> b) & 1

                @pl.when(bit == 1)
                def _(n=n, done=done):
                    cp = pltpu.make_async_copy(zeros_ref.at[pl.ds(0, n), :],
                                               xs_ref.at[pl.ds(pl.multiple_of(done, SUBLANES), n), :], zsem)
                    cp.start()
                    cp.wait()

                done = done + bit * n

        n_tiles = xs_ref.shape[0] // tmr
        half = tmr // 2
        for j in range(n_tiles - N_EXPERTS, n_tiles):
            @pl.when(j * tmr >= pad_start_ref[N_EXPERTS])
            def _(j=j):
                for c in range(2):
                    cp = pltpu.make_async_copy(zeros_ref, xs_ref.at[pl.ds(j * tmr + c * half, half), :], zsem)
                    cp.start()
                    cp.wait()

    def issue(r, carry):
        _row_copy(hp_ref, r, xs_ref, pos_ref[0, 0, r], sem).start(priority=0)
        _row_copy(hp_ref, r, xs_ref, pos_ref[0, 1, r], sem).start(priority=1)
        return carry

    lax.fori_loop(0, tm, issue, 0, unroll=ISSUE_UNROLL)
    for _ in range(2):
        pltpu.make_async_copy(hp_ref, xs_ref.at[pl.ds(0, tm), :], sem).wait()


def _dispatch(pad_start, pad_len, pos, hp, n_rows, tm, tmr):
    t, w = hp.shape
    return pl.pallas_call(
        functools.partial(_dispatch_kernel, tm=tm, tmr=tmr),
        out_shape=jax.ShapeDtypeStruct((n_rows, w), hp.dtype),
        grid_spec=pltpu.PrefetchScalarGridSpec(
            num_scalar_prefetch=2,
            grid=(t // tm,),
            in_specs=[pl.BlockSpec((1, 2, tm), lambda i, ps, pn: (i, 0, 0), memory_space=pltpu.SMEM),
                      pl.BlockSpec((tm, w), lambda i, ps, pn: (i, 0))],
            out_specs=pl.BlockSpec(memory_space=pl.ANY),
            scratch_shapes=[pltpu.VMEM((tmr // 2, w), hp.dtype), pltpu.SemaphoreType.DMA,
                            pltpu.SemaphoreType.DMA]),
        compiler_params=_cparams(("arbitrary",)),
        name="moe_dispatch",
    )(pad_start, pad_len, pos, hp)


def _experts_kernel(te_ref, tv_ref, xs_ref, wg_ref, wu_ref, wd_ref, y_ref, h_ref):
    i = pl.program_id(0)
    f = pl.program_id(1)

    @pl.when(tv_ref[i] == 1)
    def _():
        @pl.when(f == 0)
        def _():
            h_ref[...] = xs_ref[...].astype(BF16)

        h = h_ref[...]
        a = jnp.dot(h, wg_ref[0], preferred_element_type=F32)
        u = jnp.dot(h, wu_ref[0], preferred_element_type=F32)
        act = a * jax.nn.sigmoid(a) * u
        contrib = jnp.dot(act.astype(BF16), wd_ref[0], preferred_element_type=F32)

        @pl.when(f == 0)
        def _():
            y_ref[...] = contrib

        @pl.when(f != 0)
        def _():
            y_ref[...] += contrib

    @pl.when(jnp.logical_and(tv_ref[i] == 0, f == 0))
    def _():
        y_ref[...] = jnp.zeros_like(y_ref)


def _experts(tile_expert, tile_valid, xs, wg, wu, wd, tmr, tf):
    n_rows, w = xs.shape
    fdim = wg.shape[2]
    nf = fdim // tf
    fidx = lambda i, f, te, tv: jnp.where(tv[i] == 1, f, nf - 1)
    return pl.pallas_call(
        _experts_kernel,
        out_shape=jax.ShapeDtypeStruct((n_rows, D_MODEL), F32),
        grid_spec=pltpu.PrefetchScalarGridSpec(
            num_scalar_prefetch=2,
            grid=(n_rows // tmr, nf),
            in_specs=[pl.BlockSpec((tmr, w), lambda i, f, te, tv: (jnp.where(tv[i] == 1, i, 0), 0)),
                      pl.BlockSpec((1, D_MODEL, tf), lambda i, f, te, tv: (te[i], 0, fidx(i, f, te, tv))),
                      pl.BlockSpec((1, D_MODEL, tf), lambda i, f, te, tv: (te[i], 0, fidx(i, f, te, tv))),
                      pl.BlockSpec((1, tf, D_MODEL), lambda i, f, te, tv: (te[i], fidx(i, f, te, tv), 0))],
            out_specs=pl.BlockSpec((tmr, D_MODEL), lambda i, f, te, tv: (i, 0)),
            scratch_shapes=[pltpu.VMEM((tmr, D_MODEL), BF16)]),
        compiler_params=_cparams(("parallel", "arbitrary")),
        name="moe_experts",
    )(tile_expert, tile_valid, xs, wg, wu, wd)


def _combine_kernel(pos_ref, route_ref, x_ref, y_ref, o_ref, buf_ref, sem, *, tm):
    def issue(r, carry):
        _row_copy(y_ref, pos_ref[0, 0, r], buf_ref.at[0], r, sem).start(priority=0)
        _row_copy(y_ref, pos_ref[0, 1, r], buf_ref.at[1], r, sem).start(priority=1)
        return carry

    lax.fori_loop(0, tm, issue, 0, unroll=ISSUE_UNROLL)
    for k in range(2):
        pltpu.make_async_copy(y_ref.at[pl.ds(0, tm), :], buf_ref.at[k], sem).wait()
    route = route_ref[...]
    g1 = route[:, R_G1:R_G1 + 1]
    g2 = route[:, R_G2:R_G2 + 1]
    o_ref[...] = x_ref[...] + (g1 * buf_ref[0] + g2 * buf_ref[1])


def _combine(pos, route, x, y, tm):
    t = x.shape[0]
    return pl.pallas_call(
        functools.partial(_combine_kernel, tm=tm),
        out_shape=jax.ShapeDtypeStruct((t, D_MODEL), F32),
        grid=(t // tm,),
        in_specs=[pl.BlockSpec((1, 2, tm), lambda i: (i, 0, 0), memory_space=pltpu.SMEM),
                  pl.BlockSpec((tm, LANES), lambda i: (i, 0)),
                  pl.BlockSpec((tm, D_MODEL), lambda i: (i, 0)),
                  pl.BlockSpec(memory_space=pl.ANY)],
        out_specs=pl.BlockSpec((tm, D_MODEL), lambda i: (i, 0)),
        scratch_shapes=[pltpu.VMEM((2, tm, D_MODEL), F32), pltpu.SemaphoreType.DMA],
        compiler_params=_cparams(("arbitrary",)),
        name="moe_combine",
    )(pos, route, x, y)


def _moe(hp, route, idx, counts, x, wg, wu, wd, tmr=512, tf=1792):
    t = hp.shape[0]
    tm = idx.shape[2]
    n_tiles = 2 * t // tmr + N_EXPERTS
    n_rows = n_tiles * tmr
    cnt = counts[0, :N_EXPERTS].astype(jnp.int32)
    padded = (cnt + tmr - 1) // tmr * tmr
    ends = jnp.cumsum(padded)
    offs = ends - padded
    def sorted_rows(e, rank):
        off = jnp.zeros_like(e)
        for k in range(N_EXPERTS):
            off = jnp.where(e == k, offs[k], off)
        return off + rank
    pos = jnp.stack([sorted_rows(idx[:, R_E1, :], idx[:, R_RANK1, :]),
                     sorted_rows(idx[:, R_E2, :], idx[:, R_RANK2, :])], axis=1)
    tile_start = jnp.arange(n_tiles, dtype=jnp.int32) * tmr
    tile_valid = (tile_start < ends[-1]).astype(jnp.int32)
    last_tile = ends[-1] // tmr - 1
    clamped = jnp.minimum(tile_start, last_tile * tmr)
    tile_expert = jnp.sum((clamped[:, None] >= ends[None, :]).astype(jnp.int32), axis=1)
    pad_start = jnp.concatenate([offs + cnt, ends[-1:]])
    xs = _dispatch(pad_start, padded - cnt, pos, hp, n_rows, tm, tmr)
    y = _experts(tile_expert, tile_valid, xs, wg, wu, wd, tmr, tf)
    return _combine(pos, route, x, y, tm)


def _inproj1_kernel(x_ref, wz_ref, wx_ref, wm_ref, wdt_ref, z_ref, xbc_ref, mq_ref, dt_ref):
    h = x_ref[...]
    z_ref[...] = jnp.dot(h, wz_ref[...], preferred_element_type=F32).astype(BF16)
    xbc_ref[...] = jnp.dot(h, wx_ref[...], preferred_element_type=F32).astype(BF16)
    mq_ref[...] = jnp.dot(h, wm_ref[...], preferred_element_type=F32).astype(BF16)
    dt_ref[...] = jnp.dot(h, wdt_ref[...], preferred_element_type=F32)


def _inproj1(h, wz, wx, wm, wdt, tm=512):
    t = h.shape[0]
    full = lambda a: pl.BlockSpec(a.shape, lambda i: (0, 0))
    row = lambda n: pl.BlockSpec((tm, n), lambda i: (i, 0))
    return pl.pallas_call(
        _inproj1_kernel,
        out_shape=[jax.ShapeDtypeStruct((t, TOK_W), BF16), jax.ShapeDtypeStruct((t, SSM_CONV_DIM), BF16),
                   jax.ShapeDtypeStruct((t, MEM_W), BF16), jax.ShapeDtypeStruct((t, LANES), F32)],
        grid=(t // tm,),
        in_specs=[row(D_MODEL), full(wz), full(wx), full(wm), full(wdt)],
        out_specs=[row(TOK_W), row(SSM_CONV_DIM), row(MEM_W), row(LANES)],
        compiler_params=_cparams(("parallel",)),
        name="inproj1",
    )(h, wz, wx, wm, wdt)


def _split_dot(a, b, terms, split_rhs):
    rem = b if split_rhs else a
    out = None
    for _ in range(terms):
        piece = rem.astype(BF16)
        part = (jnp.dot(a, piece, preferred_element_type=F32) if split_rhs
                else jnp.dot(piece, b, preferred_element_type=F32))
        out = part if out is None else out + part
        rem = rem - piece.astype(F32)
    return out


def _ssd_kernel(z_ref, xbc_ref, dt_ref, cw_ref, cb_ref, dtb_ref, alog_ref, dsk_ref, ng_ref, ex_ref,
                o_ref, xe_ref, st_ref, *, tl):
    j = pl.program_id(1)
    pad = 8

    @pl.when(j == 0)
    def _():
        xe_ref[0:pad, :] = jnp.zeros((pad, SSM_CONV_DIM), F32)
        st_ref[...] = jnp.zeros_like(st_ref)

    x = xbc_ref[0].astype(F32)
    xe_ref[pad:pad + tl, :] = x
    acc = cb_ref[...] + cw_ref[SSM_CONV_K - 1:SSM_CONV_K, :] * x
    for k in range(SSM_CONV_K - 1):
        sh = SSM_CONV_K - 1 - k
        acc = acc + cw_ref[k:k + 1, :] * xe_ref[pad - sh:pad - sh + tl, :]
    xe_ref[0:pad, :] = x[tl - pad:tl, :]
    xc = acc * jax.nn.sigmoid(acc)
    xs = xc[:, :TOK_W]

    dt_in = dt_ref[0] + dtb_ref[...]
    dt = jnp.maximum(dt_in, 0.0) + jnp.log1p(jnp.exp(-jnp.abs(dt_in)))
    a = -jnp.exp(alog_ref[...])
    dta = dt * a
    r = lax.broadcasted_iota(jnp.int32, (tl, tl), 0)
    c = lax.broadcasted_iota(jnp.int32, (tl, tl), 1)
    tril = r >= c
    ltri = jnp.where(tril, 1.0, 0.0).astype(BF16)
    cum = _split_dot(ltri, dta, terms=3, split_rhs=True)
    cum_t = cum.T
    dt_t = dt.T
    lane = lax.broadcasted_iota(jnp.int32, (tl, LANES), 1)

    hg = SSM_HEADS // SSM_GROUPS
    y_pairs = []
    cbs = []
    for g in range(SSM_GROUPS):
        bm = xc[:, TOK_W + g * SSM_STATE:TOK_W + (g + 1) * SSM_STATE].astype(BF16)
        cm = xc[:, TOK_W + (SSM_GROUPS + g) * SSM_STATE:TOK_W + (SSM_GROUPS + g + 1) * SSM_STATE].astype(BF16)
        cbs.append(lax.dot_general(cm, bm, (((1,), (1,)), ((), ())), preferred_element_type=F32))
    for pr in range(SSM_HEADS // 2):
        xp = xs[:, pr * LANES:(pr + 1) * LANES].astype(BF16)
        ys = []
        for hh in (2 * pr, 2 * pr + 1):
            g = hh // hg
            seg = cum[:, hh:hh + 1] - cum_t[hh:hh + 1, :]
            decay = jnp.exp(jnp.where(tril, seg, NEG))
            w = cbs[g] * decay * dt_t[hh:hh + 1, :]
            ys.append(jnp.dot(w.astype(BF16), xp, preferred_element_type=F32))
        y_pairs.append(jnp.where(lane < SSM_HD, ys[0], ys[1]))
    y = jnp.concatenate(y_pairs, axis=1)

    expcum = jnp.exp(cum)
    to_end = jnp.exp(cum[tl - 1:tl, :] - cum) * dt
    stacked = jnp.concatenate([expcum, to_end], axis=0)
    exd = _split_dot(stacked, ex_ref[...], terms=2, split_rhs=False)
    expcum_x = exd[:tl]
    xw = (xs * exd[tl:]).astype(BF16)
    gw = TOK_W // SSM_GROUPS
    y_off = []
    for g in range(SSM_GROUPS):
        cs = slice(g * gw, (g + 1) * gw)
        bm_t = xc[:, TOK_W + g * SSM_STATE:TOK_W + (g + 1) * SSM_STATE].T.astype(BF16)
        cm = xc[:, TOK_W + (SSM_GROUPS + g) * SSM_STATE:TOK_W + (SSM_GROUPS + g + 1) * SSM_STATE].astype(BF16)
        sg = st_ref[:, cs]
        y_off.append(jnp.dot(cm, sg.astype(BF16), preferred_element_type=F32) * expcum_x[:, cs])
        st_ref[:, cs] = sg * expcum_x[tl - 1:tl, cs] + jnp.dot(bm_t, xw[:, cs], preferred_element_type=F32)
    y = y + jnp.concatenate(y_off, axis=1) + dsk_ref[...] * xs
    zf = z_ref[0].astype(F32)
    y = y * (zf * jax.nn.sigmoid(zf))
    outs = []
    for g in range(SSM_GROUPS):
        cs = slice(g * gw, (g + 1) * gw)
        outs.append(_rms(y[:, cs], ng_ref[:, cs]))
    o_ref[0] = jnp.concatenate(outs, axis=1).astype(o_ref.dtype)


def _ssd(z3, xbc3, dt3, cw, cb, dtb, alog, dsk, ng, ex, tl=256):
    b, s, _ = z3.shape
    full = lambda a: pl.BlockSpec(a.shape, lambda bi, j: (0, 0))
    blk = lambda n: pl.BlockSpec((1, tl, n), lambda bi, j: (bi, j, 0))
    return pl.pallas_call(
        functools.partial(_ssd_kernel, tl=tl),
        out_shape=jax.ShapeDtypeStruct((b, s, TOK_W), BF16),
        grid=(b, s // tl),
        in_specs=[blk(TOK_W), blk(SSM_CONV_DIM), blk(LANES), full(cw), full(cb), full(dtb), full(alog),
                  full(dsk), full(ng), full(ex)],
        out_specs=blk(TOK_W),
        scratch_shapes=[pltpu.VMEM((tl + 8, SSM_CONV_DIM), F32), pltpu.VMEM((SSM_STATE, TOK_W), F32)],
        compiler_params=_cparams(("parallel", "arbitrary")),
        name="conv_ssd",
    )(z3, xbc3, dt3, cw, cb, dtb, alog, dsk, ng, ex)


def _pad_lanes(v, n=LANES):
    return jnp.pad(v, [(0, 0)] * (v.ndim - 1) + [(0, n - v.shape[-1])])


def kernel(x, mem, ln1_g, ln2_g, mem_norm_g, w_out, mem_w_kv, mem_qn_g, mem_kn_g, da_w_in, da_qn_g, da_kn_g,
           da_lq1, da_lk1, da_lq2, da_lk2, da_sub_g, ssm_w_in, ssm_conv_w, ssm_conv_b, ssm_dt_bias, ssm_a_log,
           ssm_d, ssm_norm_g, ffn_w_gate, ffn_w_up, ffn_w_down, moe_w_router, moe_w_gate, moe_w_up, moe_w_down):
    b, s, d = x.shape
    t = b * s
    row = lambda v: v.reshape(1, -1).astype(F32)
    xt = x.reshape(t, d)

    mem_qg = jnp.tile(mem_qn_g.astype(F32) * (MEM_HD ** -0.5), (1, MEM_HEADS))
    mem_kg = jnp.tile(mem_kn_g.astype(F32), (1, MEM_HEADS))[:, None, :]
    kv = _memkv(mem, row(mem_norm_g), mem_w_kv.astype(BF16), mem_kg)
    wo = w_out.astype(BF16)

    lambda_init = 0.8 - 0.6 * math.exp(-0.3 * 0)
    qkg = jnp.concatenate([jnp.tile(da_qn_g[0].astype(F32) * (DA_DH ** -0.5 * LOG2E), 2 * DA_HEADS),
                           jnp.tile(da_kn_g[0].astype(F32), 2 * DA_HEADS)]).reshape(1, -1)
    qt, kn, vt, mq0 = _inproj0(xt, row(ln1_g[0]), da_w_in[0].astype(BF16), qkg, b)
    tok = _diff_attention(qt, kn.reshape(b, s, TOK_W), vt, row(da_lq1[0]), row(da_lk1[0]), row(da_lq2[0]),
                          row(da_lk2[0]), row(da_sub_g[0]), lambda_init)
    x1, h2 = _mixout(tok.reshape(t, TOK_W), mq0, 0, kv[0], mem_qg[0:1], wo[0], xt, row(ln2_g[0]))
    x2, h3 = _ffn(h2, x1, ffn_w_gate[0].astype(BF16), ffn_w_up[0].astype(BF16), ffn_w_down[0].astype(BF16),
                  row(ln1_g[1]))

    w_in = ssm_w_in[0]
    o1 = TOK_W
    o2 = o1 + SSM_CONV_DIM
    o3 = o2 + SSM_HEADS
    z, xbc, mq, dt = _inproj1(h3, w_in[:, :o1].astype(BF16), w_in[:, o1:o2].astype(BF16),
                              w_in[:, o3:].astype(BF16), _pad_lanes(w_in[:, o2:o3]).astype(BF16))
    expand = jnp.repeat(jnp.eye(SSM_HEADS, dtype=F32), SSM_HD, axis=1)
    expand = jnp.pad(expand, ((0, LANES - SSM_HEADS), (0, 0))).astype(BF16)
    dsk = jnp.repeat(ssm_d[0].astype(F32), SSM_HD).reshape(1, -1)
    tok1 = _ssd(z.reshape(b, s, -1), xbc.reshape(b, s, -1), dt.reshape(b, s, -1),
                ssm_conv_w[0].astype(F32), row(ssm_conv_b[0]), _pad_lanes(row(ssm_dt_bias[0])),
                _pad_lanes(row(ssm_a_log[0])), dsk, row(ssm_norm_g[0]), expand)
    x3, hp, route, idx, counts = _mixout(tok1.reshape(t, TOK_W), mq, 0, kv[1], mem_qg[1:2], wo[1], x2,
                                         row(ln2_g[1]), wr=_pad_lanes(moe_w_router[0].astype(F32)))
    x4 = _moe(hp, route, idx, counts, x3, moe_w_gate[0].astype(BF16), moe_w_up[0].astype(BF16),
              moe_w_down[0].astype(BF16))
    return x4.reshape(b, s, d)
```

```python
import functools
import math

import jax
import jax.numpy as jnp
from jax import lax
from jax.experimental import pallas as pl
from jax.experimental.pallas import tpu as pltpu

F32 = jnp.float32
BF16 = jnp.bfloat16

D_MODEL = 1024
CHUNK = 64
MEM_LEN = 256
MEM_W = 256
MEM_HEADS = 4
MEM_HD = 64
TOK_W = 768
DA_DH = 64
DA_HEADS = 6
SSM_HD = 64
SSM_HEADS = 12
SSM_GROUPS = 2
SSM_STATE = 128
SSM_CONV_K = 4
SSM_CONV_DIM = 1280
N_EXPERTS = 8
EPS = 1e-6
LANES = 128
NEG = -1e30
SUBLANES = 8
ONES_ROWS = SUBLANES
LOG2E = math.log2(math.e)
ISSUE_UNROLL = True
VMEM_LIMIT = 56 * 1024 * 1024


def _cparams(sem):
    return pltpu.CompilerParams(dimension_semantics=sem, vmem_limit_bytes=VMEM_LIMIT)


def _rms(xf, g):
    ms = jnp.mean(xf * xf, axis=-1, keepdims=True)
    return xf * lax.rsqrt(ms + EPS) * g


def _seg_mean_matrix(n, seg_shift):
    r = lax.broadcasted_iota(jnp.int32, (n, n), 0) >> seg_shift
    c = lax.broadcasted_iota(jnp.int32, (n, n), 1) >> seg_shift
    return jnp.where(r == c, 1.0 / (1 << seg_shift), 0.0).astype(BF16)


def _seg_mean_sq(y, seg_shift=6):
    bd = _seg_mean_matrix(256, seg_shift)
    sq = (y * y).astype(BF16)
    parts = [jnp.dot(sq[:, c:c + 256], bd, preferred_element_type=F32)
             for c in range(0, y.shape[1], 256)]
    return parts[0] if len(parts) == 1 else jnp.concatenate(parts, axis=1)


def _inproj0_kernel(x_ref, g_ref, w_ref, qkg_ref, qt_ref, k_ref, vt_ref, mq_ref):
    h = _rms(x_ref[...], g_ref[...]).astype(BF16)
    u = jnp.dot(h, w_ref[...], preferred_element_type=F32)
    tm = u.shape[0]
    hd = 2 * DA_DH
    nqk = 2 * TOK_W
    qk = u[:, :nqk]
    qkn = qk * lax.rsqrt(_seg_mean_sq(qk) + EPS) * qkg_ref[...]
    qt_ref[0] = qkn[:, :TOK_W].T.reshape(DA_HEADS, hd, tm).astype(BF16)
    k_ref[...] = qkn[:, TOK_W:].astype(BF16)
    vt_ref[0, :, 0, :hd, :] = u[:, nqk:nqk + TOK_W].T.reshape(DA_HEADS, hd, tm).astype(BF16)
    vt_ref[0, :, 0, hd:, :] = jnp.ones((DA_HEADS, ONES_ROWS, tm), BF16)
    mq_ref[...] = u[:, nqk + TOK_W:].astype(BF16)


def _inproj0(x, g, w, qkg, b, tm=512):
    t = x.shape[0]
    n = w.shape[1]
    s = t // b
    nt = s // tm
    hd = 2 * DA_DH
    return pl.pallas_call(
        _inproj0_kernel,
        out_shape=[jax.ShapeDtypeStruct((b, DA_HEADS, hd, s), BF16),
                   jax.ShapeDtypeStruct((t, TOK_W), BF16),
                   jax.ShapeDtypeStruct((b, DA_HEADS, nt, hd + ONES_ROWS, tm), BF16),
                   jax.ShapeDtypeStruct((t, MEM_W), BF16)],
        grid=(t // tm,),
        in_specs=[pl.BlockSpec((tm, D_MODEL), lambda i: (i, 0)),
                  pl.BlockSpec((1, D_MODEL), lambda i: (0, 0)),
                  pl.BlockSpec((D_MODEL, n), lambda i: (0, 0)),
                  pl.BlockSpec((1, 2 * TOK_W), lambda i: (0, 0))],
        out_specs=[pl.BlockSpec((1, DA_HEADS, hd, tm), lambda i: (i // nt, 0, 0, i % nt)),
                   pl.BlockSpec((tm, TOK_W), lambda i: (i, 0)),
                   pl.BlockSpec((1, DA_HEADS, 1, hd + ONES_ROWS, tm), lambda i: (i // nt, 0, i % nt, 0, 0)),
                   pl.BlockSpec((tm, MEM_W), lambda i: (i, 0))],
        compiler_params=_cparams(("parallel",)),
        name="inproj0",
    )(x, g, w, qkg)


def _attn_kernel(lq1_ref, lk1_ref, lq2_ref, lk2_ref, subg_ref, qt_ref, k_ref, vt_ref, o_ref,
                 acc_ref, m_ref, alpha_ref, s_ref, p_ref, *, tq, tk, nh, lambda_init):
    qi = pl.program_id(2)
    hd = 2 * DA_DH
    qs = []
    for hs in range(nh):
        qt = qt_ref[0, hs]
        row = lax.broadcasted_iota(jnp.int32, qt.shape, 0)
        zero = jnp.zeros_like(qt)
        qs += [jnp.where(row < DA_DH, qt, zero), jnp.where(row >= DA_DH, qt, zero)]
    acc_ref[...] = jnp.zeros_like(acc_ref)
    m_ref[...] = jnp.full_like(m_ref, NEG)
    p_ref[1] = jnp.zeros(p_ref.shape[1:], p_ref.dtype)
    alpha_ref[1] = jnp.ones(alpha_ref.shape[1:], alpha_ref.dtype)

    def scores(ki, slot):
        start = pl.multiple_of(ki * tk, tk)
        for hs in range(nh):
            k = k_ref[0, pl.ds(start, tk), hs * hd:(hs + 1) * hd]
            for i in range(2):
                c = 2 * hs + i
                s_ref[slot, c] = jnp.dot(k, qs[c], preferred_element_type=F32)

    def softmax(slot, masked):
        if masked:
            kc = lax.broadcasted_iota(jnp.int32, (tk, tq), 0) >> 6
            qc = lax.broadcasted_iota(jnp.int32, (tk, tq), 1) >> 6
            vis = kc <= qc
        for c in range(2 * nh):
            s = s_ref[slot, c]
            if masked:
                s = jnp.where(vis, s, NEG)
            m_old = m_ref[c]
            m_new = jnp.maximum(m_old, jnp.max(s, axis=0, keepdims=True))
            alpha_ref[slot, c] = jnp.exp2(m_old - m_new)
            p_ref[slot, c] = jnp.exp2(s - m_new).astype(BF16)
            m_ref[c] = m_new

    def values(ki, slot):
        for hs in range(nh):
            vt = vt_ref[0, hs, ki]
            for i in range(2):
                c = 2 * hs + i
                acc_ref[c] = alpha_ref[slot, c] * acc_ref[c] + jnp.dot(vt, p_ref[slot, c],
                                                                        preferred_element_type=F32)

    odd = (qi & 1) == 1

    @pl.when(odd)
    def _():
        scores(0, 1)
        scores(1, 0)
        softmax(1, False)

    @pl.when(jnp.logical_not(odd))
    def _():
        scores(0, 0)

    def body(j, carry):
        ki = (qi & 1) + 2 * j
        scores(ki + 1, 1)
        softmax(0, False)
        values(jnp.maximum(ki - 1, 0), 1)
        scores(ki + 2, 0)
        softmax(1, False)
        values(ki, 0)
        return carry

    lax.fori_loop(0, qi >> 1, body, 0)
    softmax(0, True)
    values(jnp.maximum(qi - 1, 0), 1)
    values(qi, 0)

    lam = (jnp.exp(jnp.sum(lq1_ref[...] * lk1_ref[...])) - jnp.exp(jnp.sum(lq2_ref[...] * lk2_ref[...]))
           + lambda_init)
    for hs in range(nh):
        o1 = acc_ref[2 * hs, :hd, :] / acc_ref[2 * hs, hd:hd + 1, :]
        o2 = acc_ref[2 * hs + 1, :hd, :] / acc_ref[2 * hs + 1, hd:hd + 1, :]
        ot = o1 - lam * o2
        ms = jnp.mean(ot * ot, axis=0, keepdims=True)
        ot = ot * lax.rsqrt(ms + EPS)
        o_ref[0, :, hs * hd:(hs + 1) * hd] = (ot.T * (subg_ref[...] * (1.0 - lambda_init))).astype(o_ref.dtype)


def _diff_attention(qt, k3, vt, lq1, lk1, lq2, lk2, subg, lambda_init, nh=1):
    b, s, _ = k3.shape
    tk = vt.shape[-1]
    tq = tk
    nq = s // tq
    hd = 2 * DA_DH
    hde = hd + ONES_ROWS
    nc = 2 * nh
    vec = lambda n: pl.BlockSpec((1, n), lambda bi, h, qi: (0, 0))
    kern = functools.partial(_attn_kernel, tq=tq, tk=tk, nh=nh, lambda_init=lambda_init)
    return pl.pallas_call(
        kern,
        out_shape=jax.ShapeDtypeStruct((b, s, TOK_W), BF16),
        grid=(b, DA_HEADS // nh, nq),
        in_specs=[vec(DA_DH), vec(DA_DH), vec(DA_DH), vec(DA_DH), vec(hd),
                  pl.BlockSpec((1, nh, hd, tq), lambda bi, h, qi: (bi, h, 0, qi)),
                  pl.BlockSpec((1, s, nh * hd), lambda bi, h, qi: (bi, 0, h)),
                  pl.BlockSpec((1, nh, s // tk, hde, tk), lambda bi, h, qi: (bi, h, 0, 0, 0))],
        out_specs=pl.BlockSpec((1, tq, nh * hd), lambda bi, h, qi: (bi, qi, h)),
        scratch_shapes=[pltpu.VMEM((nc, hde, tq), F32), pltpu.VMEM((nc, 1, tq), F32),
                        pltpu.VMEM((2, nc, 1, tq), F32), pltpu.VMEM((2, nc, tk, tq), F32),
                        pltpu.VMEM((2, nc, tk, tq), BF16)],
        compiler_params=_cparams(("parallel", "parallel", "parallel")),
        name="diff_attn",
    )(lq1, lk1, lq2, lk2, subg, qt, k3, vt)


def _memkv_kernel(mem_ref, g_ref, wkv_ref, kng_ref, kv_ref):
    mn = _rms(mem_ref[0], g_ref[...]).astype(BF16)
    kv = jnp.dot(mn, wkv_ref[0], preferred_element_type=F32)
    k = kv[:, :MEM_W]
    kn = k * lax.rsqrt(_seg_mean_sq(k) + EPS) * kng_ref[0]
    kv_ref[0, 0, :, :MEM_W] = kn.astype(BF16)
    kv_ref[0, 0, :, MEM_W:] = kv[:, MEM_W:].astype(BF16)


def _memkv(mem, g, wkv, kng):
    depth = wkv.shape[0]
    b = mem.shape[0]
    return pl.pallas_call(
        _memkv_kernel,
        out_shape=jax.ShapeDtypeStruct((depth, b, MEM_LEN, 2 * MEM_W), BF16),
        grid=(depth, b),
        in_specs=[pl.BlockSpec((1, MEM_LEN, D_MODEL), lambda d, bi: (bi, 0, 0)),
                  pl.BlockSpec((1, D_MODEL), lambda d, bi: (0, 0)),
                  pl.BlockSpec((1, D_MODEL, 2 * MEM_W), lambda d, bi: (d, 0, 0)),
                  pl.BlockSpec((1, 1, MEM_W), lambda d, bi: (d, 0, 0))],
        out_specs=pl.BlockSpec((1, 1, MEM_LEN, 2 * MEM_W), lambda d, bi: (d, bi, 0, 0)),
        compiler_params=_cparams(("parallel", "parallel")),
        name="memkv",
    )(mem, g, wkv, kng)


def _mem_attention(mq, qg, kv):
    qn = (mq * lax.rsqrt(_seg_mean_sq(mq) + EPS) * qg).astype(BF16)
    k = kv[:, :MEM_W]
    v = kv[:, MEM_W:]
    lane = lax.broadcasted_iota(jnp.int32, qn.shape, 1) >> 6
    out = jnp.zeros(qn.shape, F32)
    for h in range(MEM_HEADS):
        sel = lane == h
        qh = jnp.where(sel, qn, jnp.zeros_like(qn))
        s = lax.dot_general(qh, k, (((1,), (1,)), ((), ())), preferred_element_type=F32)
        p = jnp.exp(s - jnp.max(s, axis=1, keepdims=True))
        l = jnp.sum(p, axis=1, keepdims=True)
        o = jnp.dot(p.astype(BF16), v, preferred_element_type=F32)
        out = jnp.where(sel, o / l, out)
    return out


R_E1, R_E2, R_G1, R_G2, R_RANK1, R_RANK2 = range(6)


def _top2_route(logits, count_ref):
    assert N_EXPERTS == SUBLANES
    tm = logits.shape[0]
    lt = logits.T[:SUBLANES, :]
    row = lax.broadcasted_iota(jnp.int32, lt.shape, 0)
    m1 = jnp.max(lt, axis=0, keepdims=True)
    i1 = jnp.min(jnp.where(lt == m1, row, N_EXPERTS), axis=0, keepdims=True)
    lt2 = jnp.where(row == i1, NEG, lt)
    m2 = jnp.max(lt2, axis=0, keepdims=True)
    i2 = jnp.min(jnp.where(lt2 == m2, row, N_EXPERTS), axis=0, keepdims=True)
    g1 = 1.0 / (1.0 + jnp.exp(m2 - m1))
    g2 = 1.0 - g1
    chosen = jnp.logical_or(row == i1, row == i2)
    onehot = jnp.where(chosen, 1.0, 0.0)
    r = lax.broadcasted_iota(jnp.int32, (tm, tm), 0)
    c = lax.broadcasted_iota(jnp.int32, (tm, tm), 1)
    earlier = jnp.where(r < c, 1.0, 0.0).astype(BF16)
    onehot16 = jnp.concatenate([onehot, jnp.zeros_like(onehot)], axis=0).astype(BF16)
    prefix = jnp.dot(onehot16, earlier, preferred_element_type=F32)[:SUBLANES] + count_ref[:, :1]
    count_ref[...] += jnp.sum(onehot, axis=1, keepdims=True)
    rank1 = jnp.sum(jnp.where(row == i1, prefix, 0.0), axis=0, keepdims=True)
    rank2 = jnp.sum(jnp.where(row == i2, prefix, 0.0), axis=0, keepdims=True)
    rec = jnp.zeros(lt.shape, F32)
    for idx, val in ((R_E1, i1.astype(F32)), (R_E2, i2.astype(F32)), (R_G1, g1), (R_G2, g2),
                     (R_RANK1, rank1), (R_RANK2, rank2)):
        rec = jnp.where(row == idx, val, rec)
    return rec


def _mixout_kernel(*refs, with_router):
    if with_router:
        (tok_ref, mq_ref, kv_ref, qg_ref, wo_ref, x_ref, g2_ref, wr_ref,
         x1_ref, hp_ref, route_ref, idx_ref, count_ref) = refs
    else:
        tok_ref, mq_ref, kv_ref, qg_ref, wo_ref, x_ref, g2_ref, x1_ref, h2_ref = refs
    mo = _mem_attention(mq_ref[...].astype(F32), qg_ref[...], kv_ref[0])
    y = jnp.dot(tok_ref[...], wo_ref[:TOK_W, :], preferred_element_type=F32)
    y = y + jnp.dot(mo.astype(BF16), wo_ref[TOK_W:, :], preferred_element_type=F32)
    x1 = x_ref[...] + y
    x1_ref[...] = x1
    h2 = _rms(x1, g2_ref[...])
    if with_router:
        @pl.when(pl.program_id(0) == 0)
        def _():
            count_ref[...] = jnp.zeros_like(count_ref)

        hp_ref[...] = h2
        wr = wr_ref[...]
        w_hi = wr.astype(BF16)
        w_lo = (wr - w_hi.astype(F32)).astype(BF16)
        h_hi = h2.astype(BF16)
        h_lo = (h2 - h_hi.astype(F32)).astype(BF16)
        logits = (jnp.dot(h_hi, w_hi, preferred_element_type=F32) + jnp.dot(h_lo, w_hi, preferred_element_type=F32)
                  + jnp.dot(h_hi, w_lo, preferred_element_type=F32))
        rec = _top2_route(logits, count_ref)
        idx_ref[0] = rec.astype(jnp.int32)
        route_ref[...] = jnp.concatenate([rec, jnp.zeros((LANES - SUBLANES, rec.shape[1]), F32)], axis=0).T
    else:
        h2_ref[...] = h2.astype(BF16)


def _mixout(tok, mq_src, mq_block, kv, qg, wo, x, g2, wr=None, tm=512):
    t = x.shape[0]
    tiles_per_batch = t // kv.shape[0] // tm
    with_router = wr is not None
    in_specs = [pl.BlockSpec((tm, TOK_W), lambda i: (i, 0)),
                pl.BlockSpec((tm, MEM_W), lambda i: (i, mq_block)),
                pl.BlockSpec((1, MEM_LEN, 2 * MEM_W), lambda i: (i // tiles_per_batch, 0, 0)),
                pl.BlockSpec((1, MEM_W), lambda i: (0, 0)),
                pl.BlockSpec((D_MODEL, D_MODEL), lambda i: (0, 0)),
                pl.BlockSpec((tm, D_MODEL), lambda i: (i, 0)),
                pl.BlockSpec((1, D_MODEL), lambda i: (0, 0))]
    args = [tok, mq_src, kv, qg, wo, x, g2]
    row_spec = lambda n: pl.BlockSpec((tm, n), lambda i: (i, 0))
    if with_router:
        in_specs.append(pl.BlockSpec((D_MODEL, LANES), lambda i: (0, 0)))
        args.append(wr)
        out_shape = [jax.ShapeDtypeStruct((t, D_MODEL), F32), jax.ShapeDtypeStruct((t, D_MODEL), F32),
                     jax.ShapeDtypeStruct((t, LANES), F32), jax.ShapeDtypeStruct((t // tm, SUBLANES, tm), jnp.int32),
                     jax.ShapeDtypeStruct((SUBLANES, LANES), F32)]
        out_specs = [row_spec(D_MODEL), row_spec(D_MODEL), row_spec(LANES),
                     pl.BlockSpec((1, SUBLANES, tm), lambda i: (i, 0, 0)),
                     pl.BlockSpec((SUBLANES, LANES), lambda i: (0, 0))]
    else:
        out_shape = [jax.ShapeDtypeStruct((t, D_MODEL), F32), jax.ShapeDtypeStruct((t, D_MODEL), BF16)]
        out_specs = [row_spec(D_MODEL), row_spec(D_MODEL)]
    return pl.pallas_call(
        functools.partial(_mixout_kernel, with_router=with_router),
        out_shape=out_shape,
        grid=(t // tm,),
        in_specs=in_specs,
        out_specs=out_specs,
        compiler_params=_cparams(("arbitrary",) if with_router else ("parallel",)),
        name="mixout_router" if with_router else "mixout",
    )(*args)


def _ffn_kernel(h_ref, x_ref, wg_ref, wu_ref, wd_ref, gn_ref, xo_ref, ho_ref, acc_ref):
    f = pl.program_id(1)

    @pl.when(f == 0)
    def _():
        acc_ref[...] = x_ref[...]

    h = h_ref[...]
    a = jnp.dot(h, wg_ref[...], preferred_element_type=F32)
    u = jnp.dot(h, wu_ref[...], preferred_element_type=F32)
    act = a * jax.nn.sigmoid(a) * u
    acc_ref[...] += jnp.dot(act.astype(BF16), wd_ref[...], preferred_element_type=F32)

    @pl.when(f == pl.num_programs(1) - 1)
    def _():
        xn = acc_ref[...]
        xo_ref[...] = xn
        ho_ref[...] = _rms(xn, gn_ref[...]).astype(BF16)


def _ffn(h, x, wg, wu, wd, gn, tm=512, tf=1408):
    t = x.shape[0]
    fdim = wg.shape[1]
    row = pl.BlockSpec((tm, D_MODEL), lambda i, f: (i, 0))
    return pl.pallas_call(
        _ffn_kernel,
        out_shape=[jax.ShapeDtypeStruct((t, D_MODEL), F32), jax.ShapeDtypeStruct((t, D_MODEL), BF16)],
        grid=(t // tm, fdim // tf),
        in_specs=[row, row,
                  pl.BlockSpec((D_MODEL, tf), lambda i, f: (0, f)),
                  pl.BlockSpec((D_MODEL, tf), lambda i, f: (0, f)),
                  pl.BlockSpec((tf, D_MODEL), lambda i, f: (f, 0)),
                  pl.BlockSpec((1, D_MODEL), lambda i, f: (0, 0))],
        out_specs=[row, row],
        scratch_shapes=[pltpu.VMEM((tm, D_MODEL), F32)],
        compiler_params=_cparams(("parallel", "arbitrary")),
        name="ffn",
    )(h, x, wg, wu, wd, gn)


def _row_copy(src_ref, src_row, dst_ref, dst_row, sem):
    return pltpu.make_async_copy(src_ref.at[pl.ds(src_row, 1), :], dst_ref.at[pl.ds(dst_row, 1), :], sem)


def _dispatch_kernel(pad_start_ref, pad_len_ref, pos_ref, hp_ref, xs_ref, zeros_ref, sem, zsem, *, tm, tmr):
    nbits = tmr.bit_length() - 1

    @pl.when(pl.program_id(0) == 0)
    def _():
        zeros_ref[...] = jnp.zeros_like(zeros_ref)
        for e in range(N_EXPERTS):
            start = pad_start_ref[e]
            length = pad_len_ref[e]
            singles = length & (SUBLANES - 1)
            for j in range(SUBLANES - 1):
                @pl.when(j < singles)
                def _(j=j, start=start):
                    cp = _row_copy(zeros_ref, 0, xs_ref, start + j, zsem)
                    cp.start()
                    cp.wait()

            done = start + singles
            for b in range(SUBLANES.bit_length() - 1, nbits):
                n = 1 << b
                bit = (length >> b) & 1

                @pl.when(bit == 1)
                def _(n=n, done=done):
                    cp = pltpu.make_async_copy(zeros_ref.at[pl.ds(0, n), :],
                                               xs_ref.at[pl.ds(pl.multiple_of(done, SUBLANES), n), :], zsem)
                    cp.start()
                    cp.wait()

                done = done + bit * n

        n_tiles = xs_ref.shape[0] // tmr
        half = tmr // 2
        for j in range(n_tiles - N_EXPERTS, n_tiles):
            @pl.when(j * tmr >= pad_start_ref[N_EXPERTS])
            def _(j=j):
                for c in range(2):
                    cp = pltpu.make_async_copy(zeros_ref, xs_ref.at[pl.ds(j * tmr + c * half, half), :], zsem)
                    cp.start()
                    cp.wait()

    def issue(r, carry):
        _row_copy(hp_ref, r, xs_ref, pos_ref[0, 0, r], sem).start(priority=0)
        _row_copy(hp_ref, r, xs_ref, pos_ref[0, 1, r], sem).start(priority=1)
        return carry

    lax.fori_loop(0, tm, issue, 0, unroll=ISSUE_UNROLL)
    for _ in range(2):
        pltpu.make_async_copy(hp_ref, xs_ref.at[pl.ds(0, tm), :], sem).wait()


def _dispatch(pad_start, pad_len, pos, hp, n_rows, tm, tmr):
    t, w = hp.shape
    return pl.pallas_call(
        functools.partial(_dispatch_kernel, tm=tm, tmr=tmr),
        out_shape=jax.ShapeDtypeStruct((n_rows, w), hp.dtype),
        grid_spec=pltpu.PrefetchScalarGridSpec(
            num_scalar_prefetch=2,
            grid=(t // tm,),
            in_specs=[pl.BlockSpec((1, 2, tm), lambda i, ps, pn: (i, 0, 0), memory_space=pltpu.SMEM),
                      pl.BlockSpec((tm, w), lambda i, ps, pn: (i, 0))],
            out_specs=pl.BlockSpec(memory_space=pl.ANY),
            scratch_shapes=[pltpu.VMEM((tmr // 2, w), hp.dtype), pltpu.SemaphoreType.DMA,
                            pltpu.SemaphoreType.DMA]),
        compiler_params=_cparams(("arbitrary",)),
        name="moe_dispatch",
    )(pad_start, pad_len, pos, hp)


def _experts_kernel(te_ref, tv_ref, xs_ref, wg_ref, wu_ref, wd_ref, y_ref, h_ref):
    i = pl.program_id(0)
    f = pl.program_id(1)

    @pl.when(tv_ref[i] == 1)
    def _():
        @pl.when(f == 0)
        def _():
            h_ref[...] = xs_ref[...].astype(BF16)

        h = h_ref[...]
        a = jnp.dot(h, wg_ref[0], preferred_element_type=F32)
        u = jnp.dot(h, wu_ref[0], preferred_element_type=F32)
        act = a * jax.nn.sigmoid(a) * u
        contrib = jnp.dot(act.astype(BF16), wd_ref[0], preferred_element_type=F32)

        @pl.when(f == 0)
        def _():
            y_ref[...] = contrib

        @pl.when(f != 0)
        def _():
            y_ref[...] += contrib

    @pl.when(jnp.logical_and(tv_ref[i] == 0, f == 0))
    def _():
        y_ref[...] = jnp.zeros_like(y_ref)


def _experts(tile_expert, tile_valid, xs, wg, wu, wd, tmr, tf):
    n_rows, w = xs.shape
    fdim = wg.shape[2]
    nf = fdim // tf
    fidx = lambda i, f, te, tv: jnp.where(tv[i] == 1, f, nf - 1)
    return pl.pallas_call(
        _experts_kernel,
        out_shape=jax.ShapeDtypeStruct((n_rows, D_MODEL), F32),
        grid_spec=pltpu.PrefetchScalarGridSpec(
            num_scalar_prefetch=2,
            grid=(n_rows // tmr, nf),
            in_specs=[pl.BlockSpec((tmr, w), lambda i, f, te, tv: (jnp.where(tv[i] == 1, i, 0), 0)),
                      pl.BlockSpec((1, D_MODEL, tf), lambda i, f, te, tv: (te[i], 0, fidx(i, f, te, tv))),
                      pl.BlockSpec((1, D_MODEL, tf), lambda i, f, te, tv: (te[i], 0, fidx(i, f, te, tv))),
                      pl.BlockSpec((1, tf, D_MODEL), lambda i, f, te, tv: (te[i], fidx(i, f, te, tv), 0))],
            out_specs=pl.BlockSpec((tmr, D_MODEL), lambda i, f, te, tv: (i, 0)),
            scratch_shapes=[pltpu.VMEM((tmr, D_MODEL), BF16)]),
        compiler_params=_cparams(("parallel", "arbitrary")),
        name="moe_experts",
    )(tile_expert, tile_valid, xs, wg, wu, wd)


def _combine_kernel(pos_ref, route_ref, x_ref, y_ref, o_ref, buf_ref, sem, *, tm):
    def issue(r, carry):
        _row_copy(y_ref, pos_ref[0, 0, r], buf_ref.at[0], r, sem).start(priority=0)
        _row_copy(y_ref, pos_ref[0, 1, r], buf_ref.at[1], r, sem).start(priority=1)
        return carry

    lax.fori_loop(0, tm, issue, 0, unroll=ISSUE_UNROLL)
    for k in range(2):
        pltpu.make_async_copy(y_ref.at[pl.ds(0, tm), :], buf_ref.at[k], sem).wait()
    route = route_ref[...]
    g1 = route[:, R_G1:R_G1 + 1]
    g2 = route[:, R_G2:R_G2 + 1]
    o_ref[...] = x_ref[...] + (g1 * buf_ref[0] + g2 * buf_ref[1])


def _combine(pos, route, x, y, tm):
    t = x.shape[0]
    return pl.pallas_call(
        functools.partial(_combine_kernel, tm=tm),
        out_shape=jax.ShapeDtypeStruct((t, D_MODEL), F32),
        grid=(t // tm,),
        in_specs=[pl.BlockSpec((1, 2, tm), lambda i: (i, 0, 0), memory_space=pltpu.SMEM),
                  pl.BlockSpec((tm, LANES), lambda i: (i, 0)),
                  pl.BlockSpec((tm, D_MODEL), lambda i: (i, 0)),
                  pl.BlockSpec(memory_space=pl.ANY)],
        out_specs=pl.BlockSpec((tm, D_MODEL), lambda i: (i, 0)),
        scratch_shapes=[pltpu.VMEM((2, tm, D_MODEL), F32), pltpu.SemaphoreType.DMA],
        compiler_params=_cparams(("arbitrary",)),
        name="moe_combine",
    )(pos, route, x, y)


def _moe(hp, route, idx, counts, x, wg, wu, wd, tmr=512, tf=1792):
    t = hp.shape[0]
    tm = idx.shape[2]
    n_tiles = 2 * t // tmr + N_EXPERTS
    n_rows = n_tiles * tmr
    cnt = counts[:, 0].astype(jnp.int32)
    padded = (cnt + tmr - 1) // tmr * tmr
    ends = jnp.cumsum(padded)
    offs = ends - padded
    def sorted_rows(e, rank):
        off = jnp.zeros_like(e)
        for k in range(N_EXPERTS):
            off = jnp.where(e == k, offs[k], off)
        return off + rank
    pos = jnp.stack([sorted_rows(idx[:, R_E1, :], idx[:, R_RANK1, :]),
                     sorted_rows(idx[:, R_E2, :], idx[:, R_RANK2, :])], axis=1)
    tile_start = jnp.arange(n_tiles, dtype=jnp.int32) * tmr
    tile_valid = (tile_start < ends[-1]).astype(jnp.int32)
    last_tile = ends[-1] // tmr - 1
    clamped = jnp.minimum(tile_start, last_tile * tmr)
    tile_expert = jnp.sum((clamped[:, None] >= ends[None, :]).astype(jnp.int32), axis=1)
    pad_start = jnp.concatenate([offs + cnt, ends[-1:]])
    xs = _dispatch(pad_start, padded - cnt, pos, hp, n_rows, tm, tmr)
    y = _experts(tile_expert, tile_valid, xs, wg, wu, wd, tmr, tf)
    return _combine(pos, route, x, y, tm)


def _inproj1_kernel(x_ref, w_ref, z_ref, xbc_ref, mq_ref, dt_ref):
    h = x_ref[...]
    c1 = TOK_W
    c2 = c1 + SSM_CONV_DIM
    c3 = c2 + MEM_W
    z_ref[...] = jnp.dot(h, w_ref[:, :c1], preferred_element_type=F32).astype(BF16)
    xbc_ref[...] = jnp.dot(h, w_ref[:, c1:c2], preferred_element_type=F32).astype(BF16)
    mq_ref[...] = jnp.dot(h, w_ref[:, c2:c3], preferred_element_type=F32).astype(BF16)
    dt_ref[...] = jnp.dot(h, w_ref[:, c3:], preferred_element_type=F32)


def _inproj1(h, w, tm=512):
    t = h.shape[0]
    row = lambda n: pl.BlockSpec((tm, n), lambda i: (i, 0))
    return pl.pallas_call(
        _inproj1_kernel,
        out_shape=[jax.ShapeDtypeStruct((t, TOK_W), BF16), jax.ShapeDtypeStruct((t, SSM_CONV_DIM), BF16),
                   jax.ShapeDtypeStruct((t, MEM_W), BF16), jax.ShapeDtypeStruct((t, LANES), F32)],
        grid=(t // tm,),
        in_specs=[row(D_MODEL), pl.BlockSpec(w.shape, lambda i: (0, 0))],
        out_specs=[row(TOK_W), row(SSM_CONV_DIM), row(MEM_W), row(LANES)],
        compiler_params=_cparams(("parallel",)),
        name="inproj1",
    )(h, w)


def _split_dot(a, b, terms, split_rhs):
    rem = b if split_rhs else a
    out = None
    for _ in range(terms):
        piece = rem.astype(BF16)
        part = (jnp.dot(a, piece, preferred_element_type=F32) if split_rhs
                else jnp.dot(piece, b, preferred_element_type=F32))
        out = part if out is None else out + part
        rem = rem - piece.astype(F32)
    return out


def _ssd_kernel(z_ref, xbc_ref, dt_ref, cw_ref, cb_ref, dtb_ref, alog_ref, dsk_ref, ng_ref, ex_ref,
                o_ref, xe_ref, st_ref, *, tl):
    j = pl.program_id(1)
    pad = 8

    @pl.when(j == 0)
    def _():
        xe_ref[0:pad, :] = jnp.zeros((pad, SSM_CONV_DIM), F32)
        st_ref[...] = jnp.zeros_like(st_ref)

    x = xbc_ref[0].astype(F32)
    xe_ref[pad:pad + tl, :] = x
    acc = cb_ref[...] + cw_ref[SSM_CONV_K - 1:SSM_CONV_K, :] * x
    for k in range(SSM_CONV_K - 1):
        sh = SSM_CONV_K - 1 - k
        acc = acc + cw_ref[k:k + 1, :] * xe_ref[pad - sh:pad - sh + tl, :]
    xe_ref[0:pad, :] = x[tl - pad:tl, :]
    xc = acc * jax.nn.sigmoid(acc)
    xs = xc[:, :TOK_W]

    dt_in = dt_ref[0] + dtb_ref[...]
    dt = jnp.maximum(dt_in, 0.0) + jnp.log1p(jnp.exp(-jnp.abs(dt_in)))
    a = -jnp.exp(alog_ref[...])
    dta = dt * a
    r = lax.broadcasted_iota(jnp.int32, (tl, tl), 0)
    c = lax.broadcasted_iota(jnp.int32, (tl, tl), 1)
    tril = r >= c
    ltri = jnp.where(tril, 1.0, 0.0).astype(BF16)
    cum = _split_dot(ltri, dta, terms=3, split_rhs=True)
    cum_t = cum.T
    dt_t = dt.T
    lane = lax.broadcasted_iota(jnp.int32, (tl, LANES), 1)

    hg = SSM_HEADS // SSM_GROUPS
    y_pairs = []
    cbs = []
    for g in range(SSM_GROUPS):
        bm = xc[:, TOK_W + g * SSM_STATE:TOK_W + (g + 1) * SSM_STATE].astype(BF16)
        cm = xc[:, TOK_W + (SSM_GROUPS + g) * SSM_STATE:TOK_W + (SSM_GROUPS + g + 1) * SSM_STATE].astype(BF16)
        cbs.append(lax.dot_general(cm, bm, (((1,), (1,)), ((), ())), preferred_element_type=F32))
    for pr in range(SSM_HEADS // 2):
        xp = xs[:, pr * LANES:(pr + 1) * LANES].astype(BF16)
        ys = []
        for hh in (2 * pr, 2 * pr + 1):
            g = hh // hg
            seg = cum[:, hh:hh + 1] - cum_t[hh:hh + 1, :]
            decay = jnp.exp(jnp.where(tril, seg, NEG))
            w = cbs[g] * decay * dt_t[hh:hh + 1, :]
            ys.append(jnp.dot(w.astype(BF16), xp, preferred_element_type=F32))
        y_pairs.append(jnp.where(lane < SSM_HD, ys[0], ys[1]))
    y = jnp.concatenate(y_pairs, axis=1)

    expcum = jnp.exp(cum)
    to_end = jnp.exp(cum[tl - 1:tl, :] - cum) * dt
    stacked = jnp.concatenate([expcum, to_end], axis=0)
    exd = _split_dot(stacked, ex_ref[...], terms=2, split_rhs=False)
    expcum_x = exd[:tl]
    xw = (xs * exd[tl:]).astype(BF16)
    gw = TOK_W // SSM_GROUPS
    y_off = []
    for g in range(SSM_GROUPS):
        cs = slice(g * gw, (g + 1) * gw)
        bm_t = xc[:, TOK_W + g * SSM_STATE:TOK_W + (g + 1) * SSM_STATE].T.astype(BF16)
        cm = xc[:, TOK_W + (SSM_GROUPS + g) * SSM_STATE:TOK_W + (SSM_GROUPS + g + 1) * SSM_STATE].astype(BF16)
        sg = st_ref[:, cs]
        y_off.append(jnp.dot(cm, sg.astype(BF16), preferred_element_type=F32) * expcum_x[:, cs])
        st_ref[:, cs] = sg * expcum_x[tl - 1:tl, cs] + jnp.dot(bm_t, xw[:, cs], preferred_element_type=F32)
    y = y + jnp.concatenate(y_off, axis=1) + dsk_ref[...] * xs
    zf = z_ref[0].astype(F32)
    y = y * (zf * jax.nn.sigmoid(zf))
    outs = []
    for g in range(SSM_GROUPS):
        cs = slice(g * gw, (g + 1) * gw)
        outs.append(_rms(y[:, cs], ng_ref[:, cs]))
    o_ref[0] = jnp.concatenate(outs, axis=1).astype(o_ref.dtype)


def _ssd(z3, xbc3, dt3, cw, cb, dtb, alog, dsk, ng, ex, tl=256):
    b, s, _ = z3.shape
    full = lambda a: pl.BlockSpec(a.shape, lambda bi, j: (0, 0))
    blk = lambda n: pl.BlockSpec((1, tl, n), lambda bi, j: (bi, j, 0))
    return pl.pallas_call(
        functools.partial(_ssd_kernel, tl=tl),
        out_shape=jax.ShapeDtypeStruct((b, s, TOK_W), BF16),
        grid=(b, s // tl),
        in_specs=[blk(TOK_W), blk(SSM_CONV_DIM), blk(LANES), full(cw), full(cb), full(dtb), full(alog),
                  full(dsk), full(ng), full(ex)],
        out_specs=blk(TOK_W),
        scratch_shapes=[pltpu.VMEM((tl + 8, SSM_CONV_DIM), F32), pltpu.VMEM((SSM_STATE, TOK_W), F32)],
        compiler_params=_cparams(("parallel", "arbitrary")),
        name="conv_ssd",
    )(z3, xbc3, dt3, cw, cb, dtb, alog, dsk, ng, ex)


def _pad_lanes(v, n=LANES):
    return jnp.pad(v, [(0, 0)] * (v.ndim - 1) + [(0, n - v.shape[-1])])


def kernel(x, mem, ln1_g, ln2_g, mem_norm_g, w_out, mem_w_kv, mem_qn_g, mem_kn_g, da_w_in, da_qn_g, da_kn_g,
           da_lq1, da_lk1, da_lq2, da_lk2, da_sub_g, ssm_w_in, ssm_conv_w, ssm_conv_b, ssm_dt_bias, ssm_a_log,
           ssm_d, ssm_norm_g, ffn_w_gate, ffn_w_up, ffn_w_down, moe_w_router, moe_w_gate, moe_w_up, moe_w_down):
    b, s, d = x.shape
    t = b * s
    row = lambda v: v.reshape(1, -1).astype(F32)
    xt = x.reshape(t, d)

    mem_qg = jnp.tile(mem_qn_g.astype(F32) * (MEM_HD ** -0.5), (1, MEM_HEADS))
    mem_kg = jnp.tile(mem_kn_g.astype(F32), (1, MEM_HEADS))[:, None, :]
    kv = _memkv(mem, row(mem_norm_g), mem_w_kv.astype(BF16), mem_kg)
    wo = w_out.astype(BF16)

    lambda_init = 0.8 - 0.6 * math.exp(-0.3 * 0)
    qkg = jnp.concatenate([jnp.tile(da_qn_g[0].astype(F32) * (DA_DH ** -0.5 * LOG2E), 2 * DA_HEADS),
                           jnp.tile(da_kn_g[0].astype(F32), 2 * DA_HEADS)]).reshape(1, -1)
    qt, kn, vt, mq0 = _inproj0(xt, row(ln1_g[0]), da_w_in[0].astype(BF16), qkg, b)
    tok = _diff_attention(qt, kn.reshape(b, s, TOK_W), vt, row(da_lq1[0]), row(da_lk1[0]), row(da_lq2[0]),
                          row(da_lk2[0]), row(da_sub_g[0]), lambda_init)
    x1, h2 = _mixout(tok.reshape(t, TOK_W), mq0, 0, kv[0], mem_qg[0:1], wo[0], xt, row(ln2_g[0]))
    x2, h3 = _ffn(h2, x1, ffn_w_gate[0].astype(BF16), ffn_w_up[0].astype(BF16), ffn_w_down[0].astype(BF16),
                  row(ln1_g[1]))

    w_in = ssm_w_in[0]
    o2 = TOK_W + SSM_CONV_DIM
    o3 = o2 + SSM_HEADS
    w1 = jnp.concatenate([w_in[:, :o2], w_in[:, o3:], _pad_lanes(w_in[:, o2:o3])], axis=1).astype(BF16)
    z, xbc, mq, dt = _inproj1(h3, w1)
    expand = jnp.repeat(jnp.eye(SSM_HEADS, dtype=F32), SSM_HD, axis=1)
    expand = jnp.pad(expand, ((0, LANES - SSM_HEADS), (0, 0))).astype(BF16)
    dsk = jnp.repeat(ssm_d[0].astype(F32), SSM_HD).reshape(1, -1)
    tok1 = _ssd(z.reshape(b, s, -1), xbc.reshape(b, s, -1), dt.reshape(b, s, -1),
                ssm_conv_w[0].astype(F32), row(ssm_conv_b[0]), _pad_lanes(row(ssm_dt_bias[0])),
                _pad_lanes(row(ssm_a_log[0])), dsk, row(ssm_norm_g[0]), expand)
    x3, hp, route, idx, counts = _mixout(tok1.reshape(t, TOK_W), mq, 0, kv[1], mem_qg[1:2], wo[1], x2,
                                         row(ln2_g[1]), wr=_pad_lanes(moe_w_router[0].astype(F32)))
    x4 = _moe(hp, route, idx, counts, x3, moe_w_gate[0].astype(BF16), moe_w_up[0].astype(BF16),
              moe_w_down[0].astype(BF16))
    return x4.reshape(b, s, d)
```

```python
import functools
import math

import jax
import jax.numpy as jnp
from jax import lax
from jax.experimental import pallas as pl
from jax.experimental.pallas import tpu as pltpu

F32 = jnp.float32
BF16 = jnp.bfloat16

D_MODEL = 1024
CHUNK = 64
MEM_LEN = 256
MEM_W = 256
MEM_HEADS = 4
MEM_HD = 64
TOK_W = 768
DA_DH = 64
DA_HEADS = 6
SSM_HD = 64
SSM_HEADS = 12
SSM_GROUPS = 2
SSM_STATE = 128
SSM_CONV_K = 4
SSM_CONV_DIM = 1280
N_EXPERTS = 8
EPS = 1e-6
LANES = 128
NEG = -1e30
SUBLANES = 8
ONES_ROWS = SUBLANES
LOG2E = math.log2(math.e)
ISSUE_UNROLL = True
VMEM_LIMIT = 56 * 1024 * 1024


def _cparams(sem):
    return pltpu.CompilerParams(dimension_semantics=sem, vmem_limit_bytes=VMEM_LIMIT)


def _rms(xf, g):
    ms = jnp.mean(xf * xf, axis=-1, keepdims=True)
    return xf * lax.rsqrt(ms + EPS) * g


def _seg_mean_matrix(n, seg_shift):
    r = lax.broadcasted_iota(jnp.int32, (n, n), 0) >> seg_shift
    c = lax.broadcasted_iota(jnp.int32, (n, n), 1) >> seg_shift
    return jnp.where(r == c, 1.0 / (1 << seg_shift), 0.0).astype(BF16)


def _seg_mean_sq(y, seg_shift=6):
    bd = _seg_mean_matrix(256, seg_shift)
    sq = (y * y).astype(BF16)
    parts = [jnp.dot(sq[:, c:c + 256], bd, preferred_element_type=F32)
             for c in range(0, y.shape[1], 256)]
    return parts[0] if len(parts) == 1 else jnp.concatenate(parts, axis=1)


def _inproj0_kernel(x_ref, g_ref, w_ref, qkg_ref, qt_ref, k_ref, vt_ref, mq_ref):
    h = _rms(x_ref[...], g_ref[...]).astype(BF16)
    u = jnp.dot(h, w_ref[...], preferred_element_type=F32)
    tm = u.shape[0]
    hd = 2 * DA_DH
    nqk = 2 * TOK_W
    qk = u[:, :nqk]
    qkn = qk * lax.rsqrt(_seg_mean_sq(qk) + EPS) * qkg_ref[...]
    qt_ref[0] = qkn[:, :TOK_W].T.reshape(DA_HEADS, hd, tm).astype(BF16)
    k_ref[...] = qkn[:, TOK_W:].astype(BF16)
    vt_ref[0, :, 0, :hd, :] = u[:, nqk:nqk + TOK_W].T.reshape(DA_HEADS, hd, tm).astype(BF16)
    vt_ref[0, :, 0, hd:, :] = jnp.ones((DA_HEADS, ONES_ROWS, tm), BF16)
    mq_ref[...] = u[:, nqk + TOK_W:].astype(BF16)


def _inproj0(x, g, w, qkg, b, tm=512):
    t = x.shape[0]
    n = w.shape[1]
    s = t // b
    nt = s // tm
    hd = 2 * DA_DH
    return pl.pallas_call(
        _inproj0_kernel,
        out_shape=[jax.ShapeDtypeStruct((b, DA_HEADS, hd, s), BF16),
                   jax.ShapeDtypeStruct((t, TOK_W), BF16),
                   jax.ShapeDtypeStruct((b, DA_HEADS, nt, hd + ONES_ROWS, tm), BF16),
                   jax.ShapeDtypeStruct((t, MEM_W), BF16)],
        grid=(t // tm,),
        in_specs=[pl.BlockSpec((tm, D_MODEL), lambda i: (i, 0)),
                  pl.BlockSpec((1, D_MODEL), lambda i: (0, 0)),
                  pl.BlockSpec((D_MODEL, n), lambda i: (0, 0)),
                  pl.BlockSpec((1, 2 * TOK_W), lambda i: (0, 0))],
        out_specs=[pl.BlockSpec((1, DA_HEADS, hd, tm), lambda i: (i // nt, 0, 0, i % nt)),
                   pl.BlockSpec((tm, TOK_W), lambda i: (i, 0)),
                   pl.BlockSpec((1, DA_HEADS, 1, hd + ONES_ROWS, tm), lambda i: (i // nt, 0, i % nt, 0, 0)),
                   pl.BlockSpec((tm, MEM_W), lambda i: (i, 0))],
        compiler_params=_cparams(("parallel",)),
        name="inproj0",
    )(x, g, w, qkg)


def _attn_kernel(lq1_ref, lk1_ref, lq2_ref, lk2_ref, subg_ref, qt_ref, k_ref, vt_ref, o_ref,
                 acc_ref, m_ref, alpha_ref, s_ref, p_ref, *, tq, tk, nh, lambda_init):
    qi = pl.program_id(2)
    hd = 2 * DA_DH
    qs = []
    for hs in range(nh):
        qt = qt_ref[0, hs]
        row = lax.broadcasted_iota(jnp.int32, qt.shape, 0)
        zero = jnp.zeros_like(qt)
        qs += [jnp.where(row < DA_DH, qt, zero), jnp.where(row >= DA_DH, qt, zero)]
    acc_ref[...] = jnp.zeros_like(acc_ref)
    m_ref[...] = jnp.full_like(m_ref, NEG)
    p_ref[1] = jnp.zeros(p_ref.shape[1:], p_ref.dtype)
    alpha_ref[1] = jnp.ones(alpha_ref.shape[1:], alpha_ref.dtype)

    def scores(ki, slot, c):
        start = pl.multiple_of(ki * tk, tk)
        hs = c // 2
        k = k_ref[0, pl.ds(start, tk), hs * hd:(hs + 1) * hd]
        s_ref[slot, c] = jnp.dot(k, qs[c], preferred_element_type=F32)

    def softmax(slot, masked, c):
        s = s_ref[slot, c]
        if masked:
            kc = lax.broadcasted_iota(jnp.int32, (tk, tq), 0) >> 6
            qc = lax.broadcasted_iota(jnp.int32, (tk, tq), 1) >> 6
            s = jnp.where(kc <= qc, s, NEG)
        m_old = m_ref[c]
        m_new = jnp.maximum(m_old, jnp.max(s, axis=0, keepdims=True))
        alpha_ref[slot, c] = jnp.exp2(m_old - m_new)
        p_ref[slot, c] = jnp.exp2(s - m_new).astype(BF16)
        m_ref[c] = m_new

    def values(ki, slot, c):
        vt = vt_ref[0, c // 2, ki]
        acc_ref[c] = alpha_ref[slot, c] * acc_ref[c] + jnp.dot(vt, p_ref[slot, c], preferred_element_type=F32)

    groups = [(2 * hs, 2 * hs + 1) for hs in range(nh)]
    odd = (qi & 1) == 1

    def stage(fn, g, *args):
        for c in g:
            fn(*args, c)

    @pl.when(odd)
    def _():
        for g in groups:
            stage(scores, g, 0, 1)
            stage(scores, g, 1, 0)
            stage(softmax, g, 1, False)

    @pl.when(jnp.logical_not(odd))
    def _():
        for g in groups:
            stage(scores, g, 0, 0)

    def body(j, carry):
        ki = (qi & 1) + 2 * j
        for g in groups:
            stage(scores, g, ki + 1, 1)
            stage(softmax, g, 0, False)
            stage(values, g, jnp.maximum(ki - 1, 0), 1)
            stage(scores, g, ki + 2, 0)
            stage(softmax, g, 1, False)
            stage(values, g, ki, 0)
        return carry

    lax.fori_loop(0, qi >> 1, body, 0)
    for g in groups:
        stage(softmax, g, 0, True)
        stage(values, g, jnp.maximum(qi - 1, 0), 1)
        stage(values, g, qi, 0)

    lam = (jnp.exp(jnp.sum(lq1_ref[...] * lk1_ref[...])) - jnp.exp(jnp.sum(lq2_ref[...] * lk2_ref[...]))
           + lambda_init)
    for hs in range(nh):
        o1 = acc_ref[2 * hs, :hd, :] / acc_ref[2 * hs, hd:hd + 1, :]
        o2 = acc_ref[2 * hs + 1, :hd, :] / acc_ref[2 * hs + 1, hd:hd + 1, :]
        ot = o1 - lam * o2
        ms = jnp.mean(ot * ot, axis=0, keepdims=True)
        ot = ot * lax.rsqrt(ms + EPS)
        o_ref[0, :, hs * hd:(hs + 1) * hd] = (ot.T * (subg_ref[...] * (1.0 - lambda_init))).astype(o_ref.dtype)


def _diff_attention(qt, k3, vt, lq1, lk1, lq2, lk2, subg, lambda_init, nh=3):
    b, s, _ = k3.shape
    tk = vt.shape[-1]
    tq = tk
    nq = s // tq
    hd = 2 * DA_DH
    hde = hd + ONES_ROWS
    nc = 2 * nh
    vec = lambda n: pl.BlockSpec((1, n), lambda bi, h, qi: (0, 0))
    kern = functools.partial(_attn_kernel, tq=tq, tk=tk, nh=nh, lambda_init=lambda_init)
    return pl.pallas_call(
        kern,
        out_shape=jax.ShapeDtypeStruct((b, s, TOK_W), BF16),
        grid=(b, DA_HEADS // nh, nq),
        in_specs=[vec(DA_DH), vec(DA_DH), vec(DA_DH), vec(DA_DH), vec(hd),
                  pl.BlockSpec((1, nh, hd, tq), lambda bi, h, qi: (bi, h, 0, qi)),
                  pl.BlockSpec((1, s, nh * hd), lambda bi, h, qi: (bi, 0, h)),
                  pl.BlockSpec((1, nh, s // tk, hde, tk), lambda bi, h, qi: (bi, h, 0, 0, 0))],
        out_specs=pl.BlockSpec((1, tq, nh * hd), lambda bi, h, qi: (bi, qi, h)),
        scratch_shapes=[pltpu.VMEM((nc, hde, tq), F32), pltpu.VMEM((nc, 1, tq), F32),
                        pltpu.VMEM((2, nc, 1, tq), F32), pltpu.VMEM((2, nc, tk, tq), F32),
                        pltpu.VMEM((2, nc, tk, tq), BF16)],
        compiler_params=_cparams(("parallel", "parallel", "parallel")),
        name="diff_attn",
    )(lq1, lk1, lq2, lk2, subg, qt, k3, vt)


def _memkv_kernel(mem_ref, g_ref, wkv_ref, kng_ref, kv_ref):
    mn = _rms(mem_ref[0], g_ref[...]).astype(BF16)
    kv = jnp.dot(mn, wkv_ref[0], preferred_element_type=F32)
    k = kv[:, :MEM_W]
    kn = k * lax.rsqrt(_seg_mean_sq(k) + EPS) * kng_ref[0]
    kv_ref[0, 0, :, :MEM_W] = kn.astype(BF16)
    kv_ref[0, 0, :, MEM_W:] = kv[:, MEM_W:].astype(BF16)


def _memkv(mem, g, wkv, kng):
    depth = wkv.shape[0]
    b = mem.shape[0]
    return pl.pallas_call(
        _memkv_kernel,
        out_shape=jax.ShapeDtypeStruct((depth, b, MEM_LEN, 2 * MEM_W), BF16),
        grid=(depth, b),
        in_specs=[pl.BlockSpec((1, MEM_LEN, D_MODEL), lambda d, bi: (bi, 0, 0)),
                  pl.BlockSpec((1, D_MODEL), lambda d, bi: (0, 0)),
                  pl.BlockSpec((1, D_MODEL, 2 * MEM_W), lambda d, bi: (d, 0, 0)),
                  pl.BlockSpec((1, 1, MEM_W), lambda d, bi: (d, 0, 0))],
        out_specs=pl.BlockSpec((1, 1, MEM_LEN, 2 * MEM_W), lambda d, bi: (d, bi, 0, 0)),
        compiler_params=_cparams(("parallel", "parallel")),
        name="memkv",
    )(mem, g, wkv, kng)


def _mem_attention(mq, qg, kv):
    qn = (mq * lax.rsqrt(_seg_mean_sq(mq) + EPS) * qg).astype(BF16)
    k = kv[:, :MEM_W]
    v = kv[:, MEM_W:]
    lane = lax.broadcasted_iota(jnp.int32, qn.shape, 1) >> 6
    out = jnp.zeros(qn.shape, F32)
    for h in range(MEM_HEADS):
        sel = lane == h
        qh = jnp.where(sel, qn, jnp.zeros_like(qn))
        s = lax.dot_general(qh, k, (((1,), (1,)), ((), ())), preferred_element_type=F32)
        p = jnp.exp(s - jnp.max(s, axis=1, keepdims=True))
        l = jnp.sum(p, axis=1, keepdims=True)
        o = jnp.dot(p.astype(BF16), v, preferred_element_type=F32)
        out = jnp.where(sel, o / l, out)
    return out


R_E1, R_E2, R_G1, R_G2, R_RANK1, R_RANK2 = range(6)


def _top2_route(logits, count_ref):
    assert N_EXPERTS == SUBLANES
    tm = logits.shape[0]
    lt = logits.T[:SUBLANES, :]
    row = lax.broadcasted_iota(jnp.int32, lt.shape, 0)
    m1 = jnp.max(lt, axis=0, keepdims=True)
    i1 = jnp.min(jnp.where(lt == m1, row, N_EXPERTS), axis=0, keepdims=True)
    lt2 = jnp.where(row == i1, NEG, lt)
    m2 = jnp.max(lt2, axis=0, keepdims=True)
    i2 = jnp.min(jnp.where(lt2 == m2, row, N_EXPERTS), axis=0, keepdims=True)
    g1 = 1.0 / (1.0 + jnp.exp(m2 - m1))
    g2 = 1.0 - g1
    chosen = jnp.logical_or(row == i1, row == i2)
    onehot = jnp.where(chosen, 1.0, 0.0)
    r = lax.broadcasted_iota(jnp.int32, (tm, tm), 0)
    c = lax.broadcasted_iota(jnp.int32, (tm, tm), 1)
    earlier = jnp.where(r < c, 1.0, 0.0).astype(BF16)
    onehot16 = jnp.concatenate([onehot, jnp.zeros_like(onehot)], axis=0).astype(BF16)
    prefix = jnp.dot(onehot16, earlier, preferred_element_type=F32)[:SUBLANES] + count_ref[:, :1]
    count_ref[...] += jnp.sum(onehot, axis=1, keepdims=True)
    rank1 = jnp.sum(jnp.where(row == i1, prefix, 0.0), axis=0, keepdims=True)
    rank2 = jnp.sum(jnp.where(row == i2, prefix, 0.0), axis=0, keepdims=True)
    rec = jnp.zeros(lt.shape, F32)
    for idx, val in ((R_E1, i1.astype(F32)), (R_E2, i2.astype(F32)), (R_G1, g1), (R_G2, g2),
                     (R_RANK1, rank1), (R_RANK2, rank2)):
        rec = jnp.where(row == idx, val, rec)
    return rec


def _mixout_kernel(*refs, with_router):
    if with_router:
        (tok_ref, mq_ref, kv_ref, qg_ref, wo_ref, x_ref, g2_ref, wr_ref,
         x1_ref, hp_ref, route_ref, idx_ref, count_ref) = refs
    else:
        tok_ref, mq_ref, kv_ref, qg_ref, wo_ref, x_ref, g2_ref, x1_ref, h2_ref = refs
    mo = _mem_attention(mq_ref[...].astype(F32), qg_ref[...], kv_ref[0])
    y = jnp.dot(tok_ref[...], wo_ref[:TOK_W, :], preferred_element_type=F32)
    y = y + jnp.dot(mo.astype(BF16), wo_ref[TOK_W:, :], preferred_element_type=F32)
    x1 = x_ref[...] + y
    x1_ref[...] = x1
    h2 = _rms(x1, g2_ref[...])
    if with_router:
        @pl.when(pl.program_id(0) == 0)
        def _():
            count_ref[...] = jnp.zeros_like(count_ref)

        hp_ref[...] = h2
        wr = wr_ref[...]
        w_hi = wr.astype(BF16)
        w_lo = (wr - w_hi.astype(F32)).astype(BF16)
        h_hi = h2.astype(BF16)
        h_lo = (h2 - h_hi.astype(F32)).astype(BF16)
        logits = (jnp.dot(h_hi, w_hi, preferred_element_type=F32) + jnp.dot(h_lo, w_hi, preferred_element_type=F32)
                  + jnp.dot(h_hi, w_lo, preferred_element_type=F32))
        rec = _top2_route(logits, count_ref)
        idx_ref[0] = rec.astype(jnp.int32)
        route_ref[...] = jnp.concatenate([rec, jnp.zeros((LANES - SUBLANES, rec.shape[1]), F32)], axis=0).T
    else:
        h2_ref[...] = h2.astype(BF16)


def _mixout(tok, mq_src, mq_block, kv, qg, wo, x, g2, wr=None, tm=512):
    t = x.shape[0]
    tiles_per_batch = t // kv.shape[0] // tm
    with_router = wr is not None
    in_specs = [pl.BlockSpec((tm, TOK_W), lambda i: (i, 0)),
                pl.BlockSpec((tm, MEM_W), lambda i: (i, mq_block)),
                pl.BlockSpec((1, MEM_LEN, 2 * MEM_W), lambda i: (i // tiles_per_batch, 0, 0)),
                pl.BlockSpec((1, MEM_W), lambda i: (0, 0)),
                pl.BlockSpec((D_MODEL, D_MODEL), lambda i: (0, 0)),
                pl.BlockSpec((tm, D_MODEL), lambda i: (i, 0)),
                pl.BlockSpec((1, D_MODEL), lambda i: (0, 0))]
    args = [tok, mq_src, kv, qg, wo, x, g2]
    row_spec = lambda n: pl.BlockSpec((tm, n), lambda i: (i, 0))
    if with_router:
        in_specs.append(pl.BlockSpec((D_MODEL, LANES), lambda i: (0, 0)))
        args.append(wr)
        out_shape = [jax.ShapeDtypeStruct((t, D_MODEL), F32), jax.ShapeDtypeStruct((t, D_MODEL), F32),
                     jax.ShapeDtypeStruct((t, LANES), F32), jax.ShapeDtypeStruct((t // tm, SUBLANES, tm), jnp.int32),
                     jax.ShapeDtypeStruct((SUBLANES, LANES), F32)]
        out_specs = [row_spec(D_MODEL), row_spec(D_MODEL), row_spec(LANES),
                     pl.BlockSpec((1, SUBLANES, tm), lambda i: (i, 0, 0)),
                     pl.BlockSpec((SUBLANES, LANES), lambda i: (0, 0))]
    else:
        out_shape = [jax.ShapeDtypeStruct((t, D_MODEL), F32), jax.ShapeDtypeStruct((t, D_MODEL), BF16)]
        out_specs = [row_spec(D_MODEL), row_spec(D_MODEL)]
    return pl.pallas_call(
        functools.partial(_mixout_kernel, with_router=with_router),
        out_shape=out_shape,
        grid=(t // tm,),
        in_specs=in_specs,
        out_specs=out_specs,
        compiler_params=_cparams(("arbitrary",) if with_router else ("parallel",)),
        name="mixout_router" if with_router else "mixout",
    )(*args)


def _ffn_kernel(h_ref, x_ref, wg_ref, wu_ref, wd_ref, gn_ref, xo_ref, ho_ref, acc_ref):
    f = pl.program_id(1)

    @pl.when(f == 0)
    def _():
        acc_ref[...] = x_ref[...]

    h = h_ref[...]
    a = jnp.dot(h, wg_ref[...], preferred_element_type=F32)
    u = jnp.dot(h, wu_ref[...], preferred_element_type=F32)
    act = a * jax.nn.sigmoid(a) * u
    acc_ref[...] += jnp.dot(act.astype(BF16), wd_ref[...], preferred_element_type=F32)

    @pl.when(f == pl.num_programs(1) - 1)
    def _():
        xn = acc_ref[...]
        xo_ref[...] = xn
        ho_ref[...] = _rms(xn, gn_ref[...]).astype(BF16)


def _ffn(h, x, wg, wu, wd, gn, tm=512, tf=1408):
    t = x.shape[0]
    fdim = wg.shape[1]
    row = pl.BlockSpec((tm, D_MODEL), lambda i, f: (i, 0))
    return pl.pallas_call(
        _ffn_kernel,
        out_shape=[jax.ShapeDtypeStruct((t, D_MODEL), F32), jax.ShapeDtypeStruct((t, D_MODEL), BF16)],
        grid=(t // tm, fdim // tf),
        in_specs=[row, row,
                  pl.BlockSpec((D_MODEL, tf), lambda i, f: (0, f)),
                  pl.BlockSpec((D_MODEL, tf), lambda i, f: (0, f)),
                  pl.BlockSpec((tf, D_MODEL), lambda i, f: (f, 0)),
                  pl.BlockSpec((1, D_MODEL), lambda i, f: (0, 0))],
        out_specs=[row, row],
        scratch_shapes=[pltpu.VMEM((tm, D_MODEL), F32)],
        compiler_params=_cparams(("parallel", "arbitrary")),
        name="ffn",
    )(h, x, wg, wu, wd, gn)


def _row_copy(src_ref, src_row, dst_ref, dst_row, sem):
    return pltpu.make_async_copy(src_ref.at[pl.ds(src_row, 1), :], dst_ref.at[pl.ds(dst_row, 1), :], sem)


def _dispatch_kernel(pad_start_ref, pad_len_ref, pos_ref, hp_ref, xs_ref, zeros_ref, sem, zsem, *, tm, tmr):
    nbits = tmr.bit_length() - 1

    @pl.when(pl.program_id(0) == 0)
    def _():
        zeros_ref[...] = jnp.zeros_like(zeros_ref)
        for e in range(N_EXPERTS):
            start = pad_start_ref[e]
            length = pad_len_ref[e]
            singles = length & (SUBLANES - 1)
            for j in range(SUBLANES - 1):
                @pl.when(j < singles)
                def _(j=j, start=start):
                    cp = _row_copy(zeros_ref, 0, xs_ref, start + j, zsem)
                    cp.start()
                    cp.wait()

            done = start + singles
            for b in range(SUBLANES.bit_length() - 1, nbits):
                n = 1 << b
                bit = (length >> b) & 1

                @pl.when(bit == 1)
                def _(n=n, done=done):
                    cp = pltpu.make_async_copy(zeros_ref.at[pl.ds(0, n), :],
                                               xs_ref.at[pl.ds(pl.multiple_of(done, SUBLANES), n), :], zsem)
                    cp.start()
                    cp.wait()

                done = done + bit * n

        n_tiles = xs_ref.shape[0] // tmr
        half = tmr // 2
        for j in range(n_tiles - N_EXPERTS, n_tiles):
            @pl.when(j * tmr >= pad_start_ref[N_EXPERTS])
            def _(j=j):
                for c in range(2):
                    cp = pltpu.make_async_copy(zeros_ref, xs_ref.at[pl.ds(j * tmr + c * half, half), :], zsem)
                    cp.start()
                    cp.wait()

    def issue(r, carry):
        _row_copy(hp_ref, r, xs_ref, pos_ref[0, 0, r], sem).start(priority=0)
        _row_copy(hp_ref, r, xs_ref, pos_ref[0, 1, r], sem).start(priority=1)
        return carry

    lax.fori_loop(0, tm, issue, 0, unroll=ISSUE_UNROLL)
    for _ in range(2):
        pltpu.make_async_copy(hp_ref, xs_ref.at[pl.ds(0, tm), :], sem).wait()


def _dispatch(pad_start, pad_len, pos, hp, n_rows, tm, tmr):
    t, w = hp.shape
    return pl.pallas_call(
        functools.partial(_dispatch_kernel, tm=tm, tmr=tmr),
        out_shape=jax.ShapeDtypeStruct((n_rows, w), hp.dtype),
        grid_spec=pltpu.PrefetchScalarGridSpec(
            num_scalar_prefetch=2,
            grid=(t // tm,),
            in_specs=[pl.BlockSpec((1, 2, tm), lambda i, ps, pn: (i, 0, 0), memory_space=pltpu.SMEM),
                      pl.BlockSpec((tm, w), lambda i, ps, pn: (i, 0))],
            out_specs=pl.BlockSpec(memory_space=pl.ANY),
            scratch_shapes=[pltpu.VMEM((tmr // 2, w), hp.dtype), pltpu.SemaphoreType.DMA,
                            pltpu.SemaphoreType.DMA]),
        compiler_params=_cparams(("arbitrary",)),
        name="moe_dispatch",
    )(pad_start, pad_len, pos, hp)


def _experts_kernel(te_ref, tv_ref, xs_ref, wg_ref, wu_ref, wd_ref, y_ref, h_ref):
    i = pl.program_id(0)
    f = pl.program_id(1)

    @pl.when(tv_ref[i] == 1)
    def _():
        @pl.when(f == 0)
        def _():
            h_ref[...] = xs_ref[...].astype(BF16)

        h = h_ref[...]
        a = jnp.dot(h, wg_ref[0], preferred_element_type=F32)
        u = jnp.dot(h, wu_ref[0], preferred_element_type=F32)
        act = a * jax.nn.sigmoid(a) * u
        contrib = jnp.dot(act.astype(BF16), wd_ref[0], preferred_element_type=F32)

        @pl.when(f == 0)
        def _():
            y_ref[...] = contrib

        @pl.when(f != 0)
        def _():
            y_ref[...] += contrib

    @pl.when(jnp.logical_and(tv_ref[i] == 0, f == 0))
    def _():
        y_ref[...] = jnp.zeros_like(y_ref)


def _experts(tile_expert, tile_valid, xs, wg, wu, wd, tmr, tf):
    n_rows, w = xs.shape
    fdim = wg.shape[2]
    nf = fdim // tf
    fidx = lambda i, f, te, tv: jnp.where(tv[i] == 1, f, nf - 1)
    return pl.pallas_call(
        _experts_kernel,
        out_shape=jax.ShapeDtypeStruct((n_rows, D_MODEL), F32),
        grid_spec=pltpu.PrefetchScalarGridSpec(
            num_scalar_prefetch=2,
            grid=(n_rows // tmr, nf),
            in_specs=[pl.BlockSpec((tmr, w), lambda i, f, te, tv: (jnp.where(tv[i] == 1, i, 0), 0)),
                      pl.BlockSpec((1, D_MODEL, tf), lambda i, f, te, tv: (te[i], 0, fidx(i, f, te, tv))),
                      pl.BlockSpec((1, D_MODEL, tf), lambda i, f, te, tv: (te[i], 0, fidx(i, f, te, tv))),
                      pl.BlockSpec((1, tf, D_MODEL), lambda i, f, te, tv: (te[i], fidx(i, f, te, tv), 0))],
            out_specs=pl.BlockSpec((tmr, D_MODEL), lambda i, f, te, tv: (i, 0)),
            scratch_shapes=[pltpu.VMEM((tmr, D_MODEL), BF16)]),
        compiler_params=_cparams(("parallel", "arbitrary")),
        name="moe_experts",
    )(tile_expert, tile_valid, xs, wg, wu, wd)


def _combine_kernel(pos_ref, route_ref, x_ref, y_ref, o_ref, buf_ref, sem, *, tm):
    def issue(r, carry):
        _row_copy(y_ref, pos_ref[0, 0, r], buf_ref.at[0], r, sem).start(priority=0)
        _row_copy(y_ref, pos_ref[0, 1, r], buf_ref.at[1], r, sem).start(priority=1)
        return carry

    lax.fori_loop(0, tm, issue, 0, unroll=ISSUE_UNROLL)
    for k in range(2):
        pltpu.make_async_copy(y_ref.at[pl.ds(0, tm), :], buf_ref.at[k], sem).wait()
    route = route_ref[...]
    g1 = route[:, R_G1:R_G1 + 1]
    g2 = route[:, R_G2:R_G2 + 1]
    o_ref[...] = x_ref[...] + (g1 * buf_ref[0] + g2 * buf_ref[1])


def _combine(pos, route, x, y, tm):
    t = x.shape[0]
    return pl.pallas_call(
        functools.partial(_combine_kernel, tm=tm),
        out_shape=jax.ShapeDtypeStruct((t, D_MODEL), F32),
        grid=(t // tm,),
        in_specs=[pl.BlockSpec((1, 2, tm), lambda i: (i, 0, 0), memory_space=pltpu.SMEM),
                  pl.BlockSpec((tm, LANES), lambda i: (i, 0)),
                  pl.BlockSpec((tm, D_MODEL), lambda i: (i, 0)),
                  pl.BlockSpec(memory_space=pl.ANY)],
        out_specs=pl.BlockSpec((tm, D_MODEL), lambda i: (i, 0)),
        scratch_shapes=[pltpu.VMEM((2, tm, D_MODEL), F32), pltpu.SemaphoreType.DMA],
        compiler_params=_cparams(("arbitrary",)),
        name="moe_combine",
    )(pos, route, x, y)


def _moe(hp, route, idx, counts, x, wg, wu, wd, tmr=512, tf=1792):
    t = hp.shape[0]
    tm = idx.shape[2]
    n_tiles = 2 * t // tmr + N_EXPERTS
    n_rows = n_tiles * tmr
    cnt = counts[:, 0].astype(jnp.int32)
    padded = (cnt + tmr - 1) // tmr * tmr
    ends = jnp.cumsum(padded)
    offs = ends - padded
    def sorted_rows(e, rank):
        off = jnp.zeros_like(e)
        for k in range(N_EXPERTS):
            off = jnp.where(e == k, offs[k], off)
        return off + rank
    pos = jnp.stack([sorted_rows(idx[:, R_E1, :], idx[:, R_RANK1, :]),
                     sorted_rows(idx[:, R_E2, :], idx[:, R_RANK2, :])], axis=1)
    tile_start = jnp.arange(n_tiles, dtype=jnp.int32) * tmr
    tile_valid = (tile_start < ends[-1]).astype(jnp.int32)
    last_tile = ends[-1] // tmr - 1
    clamped = jnp.minimum(tile_start, last_tile * tmr)
    tile_expert = jnp.sum((clamped[:, None] >= ends[None, :]).astype(jnp.int32), axis=1)
    pad_start = jnp.concatenate([offs + cnt, ends[-1:]])
    xs = _dispatch(pad_start, padded - cnt, pos, hp, n_rows, tm, tmr)
    y = _experts(tile_expert, tile_valid, xs, wg, wu, wd, tmr, tf)
    return _combine(pos, route, x, y, tm)


def _inproj1_kernel(x_ref, w_ref, z_ref, xbc_ref, mq_ref, dt_ref):
    h = x_ref[...]
    c1 = TOK_W
    c2 = c1 + SSM_CONV_DIM
    c3 = c2 + MEM_W
    z_ref[...] = jnp.dot(h, w_ref[:, :c1], preferred_element_type=F32).astype(BF16)
    xbc_ref[...] = jnp.dot(h, w_ref[:, c1:c2], preferred_element_type=F32).astype(BF16)
    mq_ref[...] = jnp.dot(h, w_ref[:, c2:c3], preferred_element_type=F32).astype(BF16)
    dt_ref[...] = jnp.dot(h, w_ref[:, c3:], preferred_element_type=F32)


def _inproj1(h, w, tm=512):
    t = h.shape[0]
    row = lambda n: pl.BlockSpec((tm, n), lambda i: (i, 0))
    return pl.pallas_call(
        _inproj1_kernel,
        out_shape=[jax.ShapeDtypeStruct((t, TOK_W), BF16), jax.ShapeDtypeStruct((t, SSM_CONV_DIM), BF16),
                   jax.ShapeDtypeStruct((t, MEM_W), BF16), jax.ShapeDtypeStruct((t, LANES), F32)],
        grid=(t // tm,),
        in_specs=[row(D_MODEL), pl.BlockSpec(w.shape, lambda i: (0, 0))],
        out_specs=[row(TOK_W), row(SSM_CONV_DIM), row(MEM_W), row(LANES)],
        compiler_params=_cparams(("parallel",)),
        name="inproj1",
    )(h, w)


def _split_dot(a, b, terms, split_rhs):
    rem = b if split_rhs else a
    out = None
    for _ in range(terms):
        piece = rem.astype(BF16)
        part = (jnp.dot(a, piece, preferred_element_type=F32) if split_rhs
                else jnp.dot(piece, b, preferred_element_type=F32))
        out = part if out is None else out + part
        rem = rem - piece.astype(F32)
    return out


def _ssd_kernel(z_ref, xbc_ref, dt_ref, cw_ref, cb_ref, dtb_ref, alog_ref, dsk_ref, ng_ref, ex_ref,
                o_ref, xe_ref, st_ref, *, tl):
    j = pl.program_id(1)
    pad = 8

    @pl.when(j == 0)
    def _():
        xe_ref[0:pad, :] = jnp.zeros((pad, SSM_CONV_DIM), F32)
        st_ref[...] = jnp.zeros_like(st_ref)

    x = xbc_ref[0].astype(F32)
    xe_ref[pad:pad + tl, :] = x
    acc = cb_ref[...] + cw_ref[SSM_CONV_K - 1:SSM_CONV_K, :] * x
    for k in range(SSM_CONV_K - 1):
        sh = SSM_CONV_K - 1 - k
        acc = acc + cw_ref[k:k + 1, :] * xe_ref[pad - sh:pad - sh + tl, :]
    xe_ref[0:pad, :] = x[tl - pad:tl, :]
    xc = acc * jax.nn.sigmoid(acc)
    xs = xc[:, :TOK_W]

    dt_in = dt_ref[0] + dtb_ref[...]
    dt = jnp.maximum(dt_in, 0.0) + jnp.log1p(jnp.exp(-jnp.abs(dt_in)))
    a = -jnp.exp(alog_ref[...])
    dta = dt * a
    r = lax.broadcasted_iota(jnp.int32, (tl, tl), 0)
    c = lax.broadcasted_iota(jnp.int32, (tl, tl), 1)
    tril = r >= c
    ltri = jnp.where(tril, 1.0, 0.0).astype(BF16)
    cum = _split_dot(ltri, dta, terms=3, split_rhs=True)
    cum_t = cum.T
    dt_t = dt.T
    lane = lax.broadcasted_iota(jnp.int32, (tl, LANES), 1)

    hg = SSM_HEADS // SSM_GROUPS
    y_pairs = []
    cbs = []
    for g in range(SSM_GROUPS):
        bm = xc[:, TOK_W + g * SSM_STATE:TOK_W + (g + 1) * SSM_STATE].astype(BF16)
        cm = xc[:, TOK_W + (SSM_GROUPS + g) * SSM_STATE:TOK_W + (SSM_GROUPS + g + 1) * SSM_STATE].astype(BF16)
        cbs.append(lax.dot_general(cm, bm, (((1,), (1,)), ((), ())), preferred_element_type=F32))
    for pr in range(SSM_HEADS // 2):
        xp = xs[:, pr * LANES:(pr + 1) * LANES].astype(BF16)
        ys = []
        for hh in (2 * pr, 2 * pr + 1):
            g = hh // hg
            seg = cum[:, hh:hh + 1] - cum_t[hh:hh + 1, :]
            decay = jnp.exp(jnp.where(tril, seg, NEG))
            w = cbs[g] * decay * dt_t[hh:hh + 1, :]
            ys.append(jnp.dot(w.astype(BF16), xp, preferred_element_type=F32))
        y_pairs.append(jnp.where(lane < SSM_HD, ys[0], ys[1]))
    y = jnp.concatenate(y_pairs, axis=1)

    expcum = jnp.exp(cum)
    to_end = jnp.exp(cum[tl - 1:tl, :] - cum) * dt
    stacked = jnp.concatenate([expcum, to_end], axis=0)
    exd = _split_dot(stacked, ex_ref[...], terms=2, split_rhs=False)
    expcum_x = exd[:tl]
    xw = (xs * exd[tl:]).astype(BF16)
    gw = TOK_W // SSM_GROUPS
    y_off = []
    for g in range(SSM_GROUPS):
        cs = slice(g * gw, (g + 1) * gw)
        bm_t = xc[:, TOK_W + g * SSM_STATE:TOK_W + (g + 1) * SSM_STATE].T.astype(BF16)
        cm = xc[:, TOK_W + (SSM_GROUPS + g) * SSM_STATE:TOK_W + (SSM_GROUPS + g + 1) * SSM_STATE].astype(BF16)
        sg = st_ref[:, cs]
        y_off.append(jnp.dot(cm, sg.astype(BF16), preferred_element_type=F32) * expcum_x[:, cs])
        st_ref[:, cs] = sg * expcum_x[tl - 1:tl, cs] + jnp.dot(bm_t, xw[:, cs], preferred_element_type=F32)
    y = y + jnp.concatenate(y_off, axis=1) + dsk_ref[...] * xs
    zf = z_ref[0].astype(F32)
    y = y * (zf * jax.nn.sigmoid(zf))
    outs = []
    for g in range(SSM_GROUPS):
        cs = slice(g * gw, (g + 1) * gw)
        outs.append(_rms(y[:, cs], ng_ref[:, cs]))
    o_ref[0] = jnp.concatenate(outs, axis=1).astype(o_ref.dtype)


def _ssd(z3, xbc3, dt3, cw, cb, dtb, alog, dsk, ng, ex, tl=256):
    b, s, _ = z3.shape
    full = lambda a: pl.BlockSpec(a.shape, lambda bi, j: (0, 0))
    blk = lambda n: pl.BlockSpec((1, tl, n), lambda bi, j: (bi, j, 0))
    return pl.pallas_call(
        functools.partial(_ssd_kernel, tl=tl),
        out_shape=jax.ShapeDtypeStruct((b, s, TOK_W), BF16),
        grid=(b, s // tl),
        in_specs=[blk(TOK_W), blk(SSM_CONV_DIM), blk(LANES), full(cw), full(cb), full(dtb), full(alog),
                  full(dsk), full(ng), full(ex)],
        out_specs=blk(TOK_W),
        scratch_shapes=[pltpu.VMEM((tl + 8, SSM_CONV_DIM), F32), pltpu.VMEM((SSM_STATE, TOK_W), F32)],
        compiler_params=_cparams(("parallel", "arbitrary")),
        name="conv_ssd",
    )(z3, xbc3, dt3, cw, cb, dtb, alog, dsk, ng, ex)


def _pad_lanes(v, n=LANES):
    return jnp.pad(v, [(0, 0)] * (v.ndim - 1) + [(0, n - v.shape[-1])])


def kernel(x, mem, ln1_g, ln2_g, mem_norm_g, w_out, mem_w_kv, mem_qn_g, mem_kn_g, da_w_in, da_qn_g, da_kn_g,
           da_lq1, da_lk1, da_lq2, da_lk2, da_sub_g, ssm_w_in, ssm_conv_w, ssm_conv_b, ssm_dt_bias, ssm_a_log,
           ssm_d, ssm_norm_g, ffn_w_gate, ffn_w_up, ffn_w_down, moe_w_router, moe_w_gate, moe_w_up, moe_w_down):
    b, s, d = x.shape
    t = b * s
    row = lambda v: v.reshape(1, -1).astype(F32)
    xt = x.reshape(t, d)

    mem_qg = jnp.tile(mem_qn_g.astype(F32) * (MEM_HD ** -0.5), (1, MEM_HEADS))
    mem_kg = jnp.tile(mem_kn_g.astype(F32), (1, MEM_HEADS))[:, None, :]
    kv = _memkv(mem, row(mem_norm_g), mem_w_kv.astype(BF16), mem_kg)
    wo = w_out.astype(BF16)

    lambda_init = 0.8 - 0.6 * math.exp(-0.3 * 0)
    qkg = jnp.concatenate([jnp.tile(da_qn_g[0].astype(F32) * (DA_DH ** -0.5 * LOG2E), 2 * DA_HEADS),
                           jnp.tile(da_kn_g[0].astype(F32), 2 * DA_HEADS)]).reshape(1, -1)
    qt, kn, vt, mq0 = _inproj0(xt, row(ln1_g[0]), da_w_in[0].astype(BF16), qkg, b)
    tok = _diff_attention(qt, kn.reshape(b, s, TOK_W), vt, row(da_lq1[0]), row(da_lk1[0]), row(da_lq2[0]),
                          row(da_lk2[0]), row(da_sub_g[0]), lambda_init)
    x1, h2 = _mixout(tok.reshape(t, TOK_W), mq0, 0, kv[0], mem_qg[0:1], wo[0], xt, row(ln2_g[0]))
    x2, h3 = _ffn(h2, x1, ffn_w_gate[0].astype(BF16), ffn_w_up[0].astype(BF16), ffn_w_down[0].astype(BF16),
                  row(ln1_g[1]))

    w_in = ssm_w_in[0]
    o2 = TOK_W + SSM_CONV_DIM
    o3 = o2 + SSM_HEADS
    w1 = jnp.concatenate([w_in[:, :o2], w_in[:, o3:], _pad_lanes(w_in[:, o2:o3])], axis=1).astype(BF16)
    z, xbc, mq, dt = _inproj1(h3, w1)
    expand = jnp.repeat(jnp.eye(SSM_HEADS, dtype=F32), SSM_HD, axis=1)
    expand = jnp.pad(expand, ((0, LANES - SSM_HEADS), (0, 0))).astype(BF16)
    dsk = jnp.repeat(ssm_d[0].astype(F32), SSM_HD).reshape(1, -1)
    tok1 = _ssd(z.reshape(b, s, -1), xbc.reshape(b, s, -1), dt.reshape(b, s, -1),
                ssm_conv_w[0].astype(F32), row(ssm_conv_b[0]), _pad_lanes(row(ssm_dt_bias[0])),
                _pad_lanes(row(ssm_a_log[0])), dsk, row(ssm_norm_g[0]), expand)
    x3, hp, route, idx, counts = _mixout(tok1.reshape(t, TOK_W), mq, 0, kv[1], mem_qg[1:2], wo[1], x2,
                                         row(ln2_g[1]), wr=_pad_lanes(moe_w_router[0].astype(F32)))
    x4 = _moe(hp, route, idx, counts, x3, moe_w_gate[0].astype(BF16), moe_w_up[0].astype(BF16),
              moe_w_down[0].astype(BF16))
    return x4.reshape(b, s, d)
```

```python
import functools
import math

import jax
import jax.numpy as jnp
from jax import lax
from jax.experimental import pallas as pl
from jax.experimental.pallas import tpu as pltpu

F32 = jnp.float32
BF16 = jnp.bfloat16

D_MODEL = 1024
CHUNK = 64
MEM_LEN = 256
MEM_W = 256
MEM_HEADS = 4
MEM_HD = 64
TOK_W = 768
DA_DH = 64
DA_HEADS = 6
SSM_HD = 64
SSM_HEADS = 12
SSM_GROUPS = 2
SSM_STATE = 128
SSM_CONV_K = 4
SSM_CONV_DIM = 1280
N_EXPERTS = 8
EPS = 1e-6
LANES = 128
NEG = -1e30
SUBLANES = 8
ONES_ROWS = SUBLANES
LOG2E = math.log2(math.e)
ISSUE_UNROLL = True
VMEM_LIMIT = 56 * 1024 * 1024


def _cparams(sem):
    return pltpu.CompilerParams(dimension_semantics=sem, vmem_limit_bytes=VMEM_LIMIT)


def _rms(xf, g):
    ms = jnp.mean(xf * xf, axis=-1, keepdims=True)
    return xf * lax.rsqrt(ms + EPS) * g


def _seg_mean_matrix(n, seg_shift):
    r = lax.broadcasted_iota(jnp.int32, (n, n), 0) >> seg_shift
    c = lax.broadcasted_iota(jnp.int32, (n, n), 1) >> seg_shift
    return jnp.where(r == c, 1.0 / (1 << seg_shift), 0.0).astype(BF16)


def _seg_mean_sq(y, seg_shift=6):
    bd = _seg_mean_matrix(256, seg_shift)
    sq = (y * y).astype(BF16)
    parts = [jnp.dot(sq[:, c:c + 256], bd, preferred_element_type=F32)
             for c in range(0, y.shape[1], 256)]
    return parts[0] if len(parts) == 1 else jnp.concatenate(parts, axis=1)


def _inproj0_kernel(x_ref, g_ref, w_ref, qkg_ref, qt_ref, k_ref, vt_ref, mq_ref):
    h = _rms(x_ref[...], g_ref[...]).astype(BF16)
    u = jnp.dot(h, w_ref[...], preferred_element_type=F32)
    tm = u.shape[0]
    hd = 2 * DA_DH
    nqk = 2 * TOK_W
    qk = u[:, :nqk]
    qkn = qk * lax.rsqrt(_seg_mean_sq(qk) + EPS) * qkg_ref[...]
    qt_ref[0] = qkn[:, :TOK_W].T.reshape(DA_HEADS, hd, tm).astype(BF16)
    k_ref[...] = qkn[:, TOK_W:].astype(BF16)
    vt_ref[0, :, 0, :hd, :] = u[:, nqk:nqk + TOK_W].T.reshape(DA_HEADS, hd, tm).astype(BF16)
    vt_ref[0, :, 0, hd:, :] = jnp.ones((DA_HEADS, ONES_ROWS, tm), BF16)
    mq_ref[...] = u[:, nqk + TOK_W:].astype(BF16)


def _inproj0(x, g, w, qkg, b, tm=512):
    t = x.shape[0]
    n = w.shape[1]
    s = t // b
    nt = s // tm
    hd = 2 * DA_DH
    return pl.pallas_call(
        _inproj0_kernel,
        out_shape=[jax.ShapeDtypeStruct((b, DA_HEADS, hd, s), BF16),
                   jax.ShapeDtypeStruct((t, TOK_W), BF16),
                   jax.ShapeDtypeStruct((b, DA_HEADS, nt, hd + ONES_ROWS, tm), BF16),
                   jax.ShapeDtypeStruct((t, MEM_W), BF16)],
        grid=(t // tm,),
        in_specs=[pl.BlockSpec((tm, D_MODEL), lambda i: (i, 0)),
                  pl.BlockSpec((1, D_MODEL), lambda i: (0, 0)),
                  pl.BlockSpec((D_MODEL, n), lambda i: (0, 0)),
                  pl.BlockSpec((1, 2 * TOK_W), lambda i: (0, 0))],
        out_specs=[pl.BlockSpec((1, DA_HEADS, hd, tm), lambda i: (i // nt, 0, 0, i % nt)),
                   pl.BlockSpec((tm, TOK_W), lambda i: (i, 0)),
                   pl.BlockSpec((1, DA_HEADS, 1, hd + ONES_ROWS, tm), lambda i: (i // nt, 0, i % nt, 0, 0)),
                   pl.BlockSpec((tm, MEM_W), lambda i: (i, 0))],
        compiler_params=_cparams(("parallel",)),
        name="inproj0",
    )(x, g, w, qkg)


def _attn_kernel(lq1_ref, lk1_ref, lq2_ref, lk2_ref, subg_ref, qt_ref, k_ref, vt_ref, o_ref,
                 acc_ref, m_ref, alpha_ref, s_ref, p_ref, *, tq, tk, nh, lambda_init):
    qi = pl.program_id(2)
    hd = 2 * DA_DH
    qs = []
    for hs in range(nh):
        qt = qt_ref[0, hs]
        row = lax.broadcasted_iota(jnp.int32, qt.shape, 0)
        zero = jnp.zeros_like(qt)
        qs += [jnp.where(row < DA_DH, qt, zero), jnp.where(row >= DA_DH, qt, zero)]
    acc_ref[...] = jnp.zeros_like(acc_ref)
    m_ref[...] = jnp.full_like(m_ref, NEG)
    p_ref[1] = jnp.zeros(p_ref.shape[1:], p_ref.dtype)
    alpha_ref[1] = jnp.ones(alpha_ref.shape[1:], alpha_ref.dtype)

    def scores(ki, slot, c):
        start = pl.multiple_of(ki * tk, tk)
        hs = c // 2
        k = k_ref[0, pl.ds(start, tk), hs * hd:(hs + 1) * hd]
        s_ref[slot, c] = jnp.dot(k, qs[c], preferred_element_type=F32)

    def softmax(slot, masked, c):
        s = s_ref[slot, c]
        if masked:
            kc = lax.broadcasted_iota(jnp.int32, (tk, tq), 0) >> 6
            qc = lax.broadcasted_iota(jnp.int32, (tk, tq), 1) >> 6
            s = jnp.where(kc <= qc, s, NEG)
        m_old = m_ref[c]
        m_new = jnp.maximum(m_old, jnp.max(s, axis=0, keepdims=True))
        alpha_ref[slot, c] = jnp.exp2(m_old - m_new)
        p_ref[slot, c] = jnp.exp2(s - m_new).astype(BF16)
        m_ref[c] = m_new

    def values(ki, slot, c):
        vt = vt_ref[0, c // 2, ki]
        acc_ref[c] = alpha_ref[slot, c] * acc_ref[c] + jnp.dot(vt, p_ref[slot, c], preferred_element_type=F32)

    groups = [(2 * hs, 2 * hs + 1) for hs in range(nh)]
    odd = (qi & 1) == 1

    def stage(fn, g, *args):
        for c in g:
            fn(*args, c)

    @pl.when(odd)
    def _():
        for g in groups:
            stage(scores, g, 0, 1)
            stage(scores, g, 1, 0)
            stage(softmax, g, 1, False)

    @pl.when(jnp.logical_not(odd))
    def _():
        for g in groups:
            stage(scores, g, 0, 0)

    def body(j, carry):
        ki = (qi & 1) + 2 * j
        for g in groups:
            stage(scores, g, ki + 1, 1)
            stage(softmax, g, 0, False)
            stage(values, g, jnp.maximum(ki - 1, 0), 1)
            stage(scores, g, ki + 2, 0)
            stage(softmax, g, 1, False)
            stage(values, g, ki, 0)
        return carry

    lax.fori_loop(0, qi >> 1, body, 0)
    for g in groups:
        stage(softmax, g, 0, True)
        stage(values, g, jnp.maximum(qi - 1, 0), 1)
        stage(values, g, qi, 0)

    lam = (jnp.exp(jnp.sum(lq1_ref[...] * lk1_ref[...])) - jnp.exp(jnp.sum(lq2_ref[...] * lk2_ref[...]))
           + lambda_init)
    for hs in range(nh):
        o1 = acc_ref[2 * hs, :hd, :] / acc_ref[2 * hs, hd:hd + 1, :]
        o2 = acc_ref[2 * hs + 1, :hd, :] / acc_ref[2 * hs + 1, hd:hd + 1, :]
        ot = o1 - lam * o2
        ms = jnp.mean(ot * ot, axis=0, keepdims=True)
        ot = ot * lax.rsqrt(ms + EPS)
        o_ref[0, :, hs * hd:(hs + 1) * hd] = (ot.T * (subg_ref[...] * (1.0 - lambda_init))).astype(o_ref.dtype)


def _diff_attention(qt, k3, vt, lq1, lk1, lq2, lk2, subg, lambda_init, nh=3):
    b, s, _ = k3.shape
    tk = vt.shape[-1]
    tq = tk
    nq = s // tq
    hd = 2 * DA_DH
    hde = hd + ONES_ROWS
    nc = 2 * nh
    vec = lambda n: pl.BlockSpec((1, n), lambda bi, h, qi: (0, 0))
    kern = functools.partial(_attn_kernel, tq=tq, tk=tk, nh=nh, lambda_init=lambda_init)
    return pl.pallas_call(
        kern,
        out_shape=jax.ShapeDtypeStruct((b, s, TOK_W), BF16),
        grid=(b, DA_HEADS // nh, nq),
        in_specs=[vec(DA_DH), vec(DA_DH), vec(DA_DH), vec(DA_DH), vec(hd),
                  pl.BlockSpec((1, nh, hd, tq), lambda bi, h, qi: (bi, h, 0, qi)),
                  pl.BlockSpec((1, s, nh * hd), lambda bi, h, qi: (bi, 0, h)),
                  pl.BlockSpec((1, nh, s // tk, hde, tk), lambda bi, h, qi: (bi, h, 0, 0, 0))],
        out_specs=pl.BlockSpec((1, tq, nh * hd), lambda bi, h, qi: (bi, qi, h)),
        scratch_shapes=[pltpu.VMEM((nc, hde, tq), F32), pltpu.VMEM((nc, 1, tq), F32),
                        pltpu.VMEM((2, nc, 1, tq), F32), pltpu.VMEM((2, nc, tk, tq), F32),
                        pltpu.VMEM((2, nc, tk, tq), BF16)],
        compiler_params=_cparams(("parallel", "parallel", "parallel")),
        name="diff_attn",
    )(lq1, lk1, lq2, lk2, subg, qt, k3, vt)


def _memkv_kernel(mem_ref, g_ref, wkv_ref, kng_ref, kv_ref):
    mn = _rms(mem_ref[0], g_ref[...]).astype(BF16)
    kv = jnp.dot(mn, wkv_ref[0], preferred_element_type=F32)
    k = kv[:, :MEM_W]
    kn = k * lax.rsqrt(_seg_mean_sq(k) + EPS) * kng_ref[0]
    kv_ref[0, 0, :, :MEM_W] = kn.astype(BF16)
    kv_ref[0, 0, :, MEM_W:] = kv[:, MEM_W:].astype(BF16)


def _memkv(mem, g, wkv, kng):
    depth = wkv.shape[0]
    b = mem.shape[0]
    return pl.pallas_call(
        _memkv_kernel,
        out_shape=jax.ShapeDtypeStruct((depth, b, MEM_LEN, 2 * MEM_W), BF16),
        grid=(depth, b),
        in_specs=[pl.BlockSpec((1, MEM_LEN, D_MODEL), lambda d, bi: (bi, 0, 0)),
                  pl.BlockSpec((1, D_MODEL), lambda d, bi: (0, 0)),
                  pl.BlockSpec((1, D_MODEL, 2 * MEM_W), lambda d, bi: (d, 0, 0)),
                  pl.BlockSpec((1, 1, MEM_W), lambda d, bi: (d, 0, 0))],
        out_specs=pl.BlockSpec((1, 1, MEM_LEN, 2 * MEM_W), lambda d, bi: (d, bi, 0, 0)),
        compiler_params=_cparams(("parallel", "parallel")),
        name="memkv",
    )(mem, g, wkv, kng)


def _mem_attention(mq, qg, kv):
    qn = (mq * lax.rsqrt(_seg_mean_sq(mq) + EPS) * qg).astype(BF16)
    k = kv[:, :MEM_W]
    v = kv[:, MEM_W:]
    lane = lax.broadcasted_iota(jnp.int32, qn.shape, 1) >> 6
    out = jnp.zeros(qn.shape, F32)
    for h in range(MEM_HEADS):
        sel = lane == h
        qh = jnp.where(sel, qn, jnp.zeros_like(qn))
        s = lax.dot_general(qh, k, (((1,), (1,)), ((), ())), preferred_element_type=F32)
        p = jnp.exp(s - jnp.max(s, axis=1, keepdims=True))
        l = jnp.sum(p, axis=1, keepdims=True)
        o = jnp.dot(p.astype(BF16), v, preferred_element_type=F32)
        out = jnp.where(sel, o / l, out)
    return out


R_E1, R_E2, R_G1, R_G2, R_RANK1, R_RANK2 = range(6)


def _top2_route(logits, count_ref):
    assert N_EXPERTS == SUBLANES
    tm = logits.shape[0]
    lt = logits.T[:SUBLANES, :]
    row = lax.broadcasted_iota(jnp.int32, lt.shape, 0)
    m1 = jnp.max(lt, axis=0, keepdims=True)
    i1 = jnp.min(jnp.where(lt == m1, row, N_EXPERTS), axis=0, keepdims=True)
    lt2 = jnp.where(row == i1, NEG, lt)
    m2 = jnp.max(lt2, axis=0, keepdims=True)
    i2 = jnp.min(jnp.where(lt2 == m2, row, N_EXPERTS), axis=0, keepdims=True)
    g1 = 1.0 / (1.0 + jnp.exp(m2 - m1))
    g2 = 1.0 - g1
    chosen = jnp.logical_or(row == i1, row == i2)
    onehot = jnp.where(chosen, 1.0, 0.0)
    r = lax.broadcasted_iota(jnp.int32, (tm, tm), 0)
    c = lax.broadcasted_iota(jnp.int32, (tm, tm), 1)
    earlier = jnp.where(r < c, 1.0, 0.0).astype(BF16)
    onehot16 = jnp.concatenate([onehot, jnp.zeros_like(onehot)], axis=0).astype(BF16)
    prefix = jnp.dot(onehot16, earlier, preferred_element_type=F32)[:SUBLANES] + count_ref[:, :1]
    count_ref[...] += jnp.sum(onehot, axis=1, keepdims=True)
    rank1 = jnp.sum(jnp.where(row == i1, prefix, 0.0), axis=0, keepdims=True)
    rank2 = jnp.sum(jnp.where(row == i2, prefix, 0.0), axis=0, keepdims=True)
    rec = jnp.zeros(lt.shape, F32)
    for idx, val in ((R_E1, i1.astype(F32)), (R_E2, i2.astype(F32)), (R_G1, g1), (R_G2, g2),
                     (R_RANK1, rank1), (R_RANK2, rank2)):
        rec = jnp.where(row == idx, val, rec)
    return rec


def _mixout_kernel(*refs, with_router):
    if with_router:
        (tok_ref, mq_ref, kv_ref, qg_ref, wo_ref, x_ref, g2_ref, wr_ref,
         x1_ref, hp_ref, route_ref, idx_ref, count_ref) = refs
    else:
        tok_ref, mq_ref, kv_ref, qg_ref, wo_ref, x_ref, g2_ref, x1_ref, h2_ref = refs
    mo = _mem_attention(mq_ref[...].astype(F32), qg_ref[...], kv_ref[0])
    y = jnp.dot(tok_ref[...], wo_ref[:TOK_W, :], preferred_element_type=F32)
    y = y + jnp.dot(mo.astype(BF16), wo_ref[TOK_W:, :], preferred_element_type=F32)
    x1 = x_ref[...] + y
    x1_ref[...] = x1
    h2 = _rms(x1, g2_ref[...])
    if with_router:
        @pl.when(pl.program_id(0) == 0)
        def _():
            count_ref[...] = jnp.zeros_like(count_ref)

        hp_ref[...] = h2
        wr = wr_ref[...]
        w_hi = wr.astype(BF16)
        w_lo = (wr - w_hi.astype(F32)).astype(BF16)
        h_hi = h2.astype(BF16)
        h_lo = (h2 - h_hi.astype(F32)).astype(BF16)
        logits = (jnp.dot(h_hi, w_hi, preferred_element_type=F32) + jnp.dot(h_lo, w_hi, preferred_element_type=F32)
                  + jnp.dot(h_hi, w_lo, preferred_element_type=F32))
        rec = _top2_route(logits, count_ref)
        idx_ref[0] = rec.astype(jnp.int32)
        route_ref[...] = jnp.concatenate([rec, jnp.zeros((LANES - SUBLANES, rec.shape[1]), F32)], axis=0).T
    else:
        h2_ref[...] = h2.astype(BF16)


def _mixout(tok, mq_src, mq_block, kv, qg, wo, x, g2, wr=None, tm=512):
    t = x.shape[0]
    tiles_per_batch = t // kv.shape[0] // tm
    with_router = wr is not None
    in_specs = [pl.BlockSpec((tm, TOK_W), lambda i: (i, 0)),
                pl.BlockSpec((tm, MEM_W), lambda i: (i, mq_block)),
                pl.BlockSpec((1, MEM_LEN, 2 * MEM_W), lambda i: (i // tiles_per_batch, 0, 0)),
                pl.BlockSpec((1, MEM_W), lambda i: (0, 0)),
                pl.BlockSpec((D_MODEL, D_MODEL), lambda i: (0, 0)),
                pl.BlockSpec((tm, D_MODEL), lambda i: (i, 0)),
                pl.BlockSpec((1, D_MODEL), lambda i: (0, 0))]
    args = [tok, mq_src, kv, qg, wo, x, g2]
    row_spec = lambda n: pl.BlockSpec((tm, n), lambda i: (i, 0))
    if with_router:
        in_specs.append(pl.BlockSpec((D_MODEL, LANES), lambda i: (0, 0)))
        args.append(wr)
        out_shape = [jax.ShapeDtypeStruct((t, D_MODEL), F32), jax.ShapeDtypeStruct((t, D_MODEL), F32),
                     jax.ShapeDtypeStruct((t, LANES), F32), jax.ShapeDtypeStruct((t // tm, SUBLANES, tm), jnp.int32),
                     jax.ShapeDtypeStruct((SUBLANES, LANES), F32)]
        out_specs = [row_spec(D_MODEL), row_spec(D_MODEL), row_spec(LANES),
                     pl.BlockSpec((1, SUBLANES, tm), lambda i: (i, 0, 0)),
                     pl.BlockSpec((SUBLANES, LANES), lambda i: (0, 0))]
    else:
        out_shape = [jax.ShapeDtypeStruct((t, D_MODEL), F32), jax.ShapeDtypeStruct((t, D_MODEL), BF16)]
        out_specs = [row_spec(D_MODEL), row_spec(D_MODEL)]
    return pl.pallas_call(
        functools.partial(_mixout_kernel, with_router=with_router),
        out_shape=out_shape,
        grid=(t // tm,),
        in_specs=in_specs,
        out_specs=out_specs,
        compiler_params=_cparams(("arbitrary",) if with_router else ("parallel",)),
        name="mixout_router" if with_router else "mixout",
    )(*args)


def _ffn_kernel(h_ref, x_ref, wg_ref, wu_ref, wd_ref, gn_ref, xo_ref, ho_ref, acc_ref):
    f = pl.program_id(1)
    h = h_ref[...]
    a = jnp.dot(h, wg_ref[...], preferred_element_type=F32)
    u = jnp.dot(h, wu_ref[...], preferred_element_type=F32)
    act = a * jax.nn.sigmoid(a) * u
    contrib = jnp.dot(act.astype(BF16), wd_ref[...], preferred_element_type=F32)
    acc_ref[...] = jnp.where(f == 0, x_ref[...], acc_ref[...]) + contrib

    @pl.when(f == pl.num_programs(1) - 1)
    def _():
        xn = acc_ref[...]
        xo_ref[...] = xn
        ho_ref[...] = _rms(xn, gn_ref[...]).astype(BF16)


def _ffn(h, x, wg, wu, wd, gn, tm=512, tf=1408):
    t = x.shape[0]
    fdim = wg.shape[1]
    row = pl.BlockSpec((tm, D_MODEL), lambda i, f: (i, 0))
    return pl.pallas_call(
        _ffn_kernel,
        out_shape=[jax.ShapeDtypeStruct((t, D_MODEL), F32), jax.ShapeDtypeStruct((t, D_MODEL), BF16)],
        grid=(t // tm, fdim // tf),
        in_specs=[row, row,
                  pl.BlockSpec((D_MODEL, tf), lambda i, f: (0, f)),
                  pl.BlockSpec((D_MODEL, tf), lambda i, f: (0, f)),
                  pl.BlockSpec((tf, D_MODEL), lambda i, f: (f, 0)),
                  pl.BlockSpec((1, D_MODEL), lambda i, f: (0, 0))],
        out_specs=[row, row],
        scratch_shapes=[pltpu.VMEM((tm, D_MODEL), F32)],
        compiler_params=_cparams(("parallel", "arbitrary")),
        name="ffn",
    )(h, x, wg, wu, wd, gn)


def _row_copy(src_ref, src_row, dst_ref, dst_row, sem):
    return pltpu.make_async_copy(src_ref.at[pl.ds(src_row, 1), :], dst_ref.at[pl.ds(dst_row, 1), :], sem)


def _dispatch_kernel(pad_start_ref, pad_len_ref, pos_ref, hp_ref, xs_ref, zeros_ref, sem, zsem, *, tm, tmr):
    nbits = tmr.bit_length() - 1

    @pl.when(pl.program_id(0) == 0)
    def _():
        zeros_ref[...] = jnp.zeros_like(zeros_ref)
        for e in range(N_EXPERTS):
            start = pad_start_ref[e]
            length = pad_len_ref[e]
            singles = length & (SUBLANES - 1)
            for j in range(SUBLANES - 1):
                @pl.when(j < singles)
                def _(j=j, start=start):
                    cp = _row_copy(zeros_ref, 0, xs_ref, start + j, zsem)
                    cp.start()
                    cp.wait()

            done = start + singles
            for b in range(SUBLANES.bit_length() - 1, nbits):
                n = 1 << b
                bit = (length >> b) & 1

                @pl.when(bit == 1)
                def _(n=n, done=done):
                    cp = pltpu.make_async_copy(zeros_ref.at[pl.ds(0, n), :],
                                               xs_ref.at[pl.ds(pl.multiple_of(done, SUBLANES), n), :], zsem)
                    cp.start()
                    cp.wait()

                done = done + bit * n

        n_tiles = xs_ref.shape[0] // tmr
        half = tmr // 2
        for j in range(n_tiles - N_EXPERTS, n_tiles):
            @pl.when(j * tmr >= pad_start_ref[N_EXPERTS])
            def _(j=j):
                for c in range(2):
                    cp = pltpu.make_async_copy(zeros_ref, xs_ref.at[pl.ds(j * tmr + c * half, half), :], zsem)
                    cp.start()
                    cp.wait()

    def issue(r, carry):
        _row_copy(hp_ref, r, xs_ref, pos_ref[0, 0, r], sem).start(priority=0)
        _row_copy(hp_ref, r, xs_ref, pos_ref[0, 1, r], sem).start(priority=1)
        return carry

    lax.fori_loop(0, tm, issue, 0, unroll=ISSUE_UNROLL)
    for _ in range(2):
        pltpu.make_async_copy(hp_ref, xs_ref.at[pl.ds(0, tm), :], sem).wait()


def _dispatch(pad_start, pad_len, pos, hp, n_rows, tm, tmr):
    t, w = hp.shape
    return pl.pallas_call(
        functools.partial(_dispatch_kernel, tm=tm, tmr=tmr),
        out_shape=jax.ShapeDtypeStruct((n_rows, w), hp.dtype),
        grid_spec=pltpu.PrefetchScalarGridSpec(
            num_scalar_prefetch=2,
            grid=(t // tm,),
            in_specs=[pl.BlockSpec((1, 2, tm), lambda i, ps, pn: (i, 0, 0), memory_space=pltpu.SMEM),
                      pl.BlockSpec((tm, w), lambda i, ps, pn: (i, 0))],
            out_specs=pl.BlockSpec(memory_space=pl.ANY),
            scratch_shapes=[pltpu.VMEM((tmr // 2, w), hp.dtype), pltpu.SemaphoreType.DMA,
                            pltpu.SemaphoreType.DMA]),
        compiler_params=_cparams(("arbitrary",)),
        name="moe_dispatch",
    )(pad_start, pad_len, pos, hp)


def _experts_kernel(te_ref, tv_ref, xs_ref, wg_ref, wu_ref, wd_ref, y_ref, h_ref):
    i = pl.program_id(0)
    f = pl.program_id(1)

    @pl.when(tv_ref[i] == 1)
    def _():
        @pl.when(f == 0)
        def _():
            h_ref[...] = xs_ref[...].astype(BF16)

        h = h_ref[...]
        a = jnp.dot(h, wg_ref[0], preferred_element_type=F32)
        u = jnp.dot(h, wu_ref[0], preferred_element_type=F32)
        act = a * jax.nn.sigmoid(a) * u
        contrib = jnp.dot(act.astype(BF16), wd_ref[0], preferred_element_type=F32)
        y_ref[...] = jnp.where(f == 0, 0.0, y_ref[...]) + contrib

    @pl.when(jnp.logical_and(tv_ref[i] == 0, f == 0))
    def _():
        y_ref[...] = jnp.zeros_like(y_ref)


def _experts(tile_expert, tile_valid, xs, wg, wu, wd, tmr, tf):
    n_rows, w = xs.shape
    fdim = wg.shape[2]
    nf = fdim // tf
    fidx = lambda i, f, te, tv: jnp.where(tv[i] == 1, f, nf - 1)
    return pl.pallas_call(
        _experts_kernel,
        out_shape=jax.ShapeDtypeStruct((n_rows, D_MODEL), F32),
        grid_spec=pltpu.PrefetchScalarGridSpec(
            num_scalar_prefetch=2,
            grid=(n_rows // tmr, nf),
            in_specs=[pl.BlockSpec((tmr, w), lambda i, f, te, tv: (jnp.where(tv[i] == 1, i, 0), 0)),
                      pl.BlockSpec((1, D_MODEL, tf), lambda i, f, te, tv: (te[i], 0, fidx(i, f, te, tv))),
                      pl.BlockSpec((1, D_MODEL, tf), lambda i, f, te, tv: (te[i], 0, fidx(i, f, te, tv))),
                      pl.BlockSpec((1, tf, D_MODEL), lambda i, f, te, tv: (te[i], fidx(i, f, te, tv), 0))],
            out_specs=pl.BlockSpec((tmr, D_MODEL), lambda i, f, te, tv: (i, 0)),
            scratch_shapes=[pltpu.VMEM((tmr, D_MODEL), BF16)]),
        compiler_params=_cparams(("parallel", "arbitrary")),
        name="moe_experts",
    )(tile_expert, tile_valid, xs, wg, wu, wd)


def _combine_kernel(pos_ref, route_ref, x_ref, y_ref, o_ref, buf_ref, sem, *, tm):
    def issue(r, carry):
        _row_copy(y_ref, pos_ref[0, 0, r], buf_ref.at[0], r, sem).start(priority=0)
        _row_copy(y_ref, pos_ref[0, 1, r], buf_ref.at[1], r, sem).start(priority=1)
        return carry

    lax.fori_loop(0, tm, issue, 0, unroll=ISSUE_UNROLL)
    for k in range(2):
        pltpu.make_async_copy(y_ref.at[pl.ds(0, tm), :], buf_ref.at[k], sem).wait()
    route = route_ref[...]
    g1 = route[:, R_G1:R_G1 + 1]
    g2 = route[:, R_G2:R_G2 + 1]
    o_ref[...] = x_ref[...] + (g1 * buf_ref[0] + g2 * buf_ref[1])


def _combine(pos, route, x, y, tm):
    t = x.shape[0]
    return pl.pallas_call(
        functools.partial(_combine_kernel, tm=tm),
        out_shape=jax.ShapeDtypeStruct((t, D_MODEL), F32),
        grid=(t // tm,),
        in_specs=[pl.BlockSpec((1, 2, tm), lambda i: (i, 0, 0), memory_space=pltpu.SMEM),
                  pl.BlockSpec((tm, LANES), lambda i: (i, 0)),
                  pl.BlockSpec((tm, D_MODEL), lambda i: (i, 0)),
                  pl.BlockSpec(memory_space=pl.ANY)],
        out_specs=pl.BlockSpec((tm, D_MODEL), lambda i: (i, 0)),
        scratch_shapes=[pltpu.VMEM((2, tm, D_MODEL), F32), pltpu.SemaphoreType.DMA],
        compiler_params=_cparams(("arbitrary",)),
        name="moe_combine",
    )(pos, route, x, y)


def _moe(hp, route, idx, counts, x, wg, wu, wd, tmr=512, tf=1792):
    t = hp.shape[0]
    tm = idx.shape[2]
    n_tiles = 2 * t // tmr + N_EXPERTS
    n_rows = n_tiles * tmr
    cnt = counts[:, 0].astype(jnp.int32)
    padded = (cnt + tmr - 1) // tmr * tmr
    ends = jnp.cumsum(padded)
    offs = ends - padded
    def sorted_rows(e, rank):
        off = jnp.zeros_like(e)
        for k in range(N_EXPERTS):
            off = jnp.where(e == k, offs[k], off)
        return off + rank
    pos = jnp.stack([sorted_rows(idx[:, R_E1, :], idx[:, R_RANK1, :]),
                     sorted_rows(idx[:, R_E2, :], idx[:, R_RANK2, :])], axis=1)
    tile_start = jnp.arange(n_tiles, dtype=jnp.int32) * tmr
    tile_valid = (tile_start < ends[-1]).astype(jnp.int32)
    last_tile = ends[-1] // tmr - 1
    clamped = jnp.minimum(tile_start, last_tile * tmr)
    tile_expert = jnp.sum((clamped[:, None] >= ends[None, :]).astype(jnp.int32), axis=1)
    pad_start = jnp.concatenate([offs + cnt, ends[-1:]])
    xs = _dispatch(pad_start, padded - cnt, pos, hp, n_rows, tm, tmr)
    y = _experts(tile_expert, tile_valid, xs, wg, wu, wd, tmr, tf)
    return _combine(pos, route, x, y, tm)


def _inproj1_kernel(x_ref, w_ref, z_ref, xbc_ref, mq_ref, dt_ref):
    h = x_ref[...]
    c1 = TOK_W
    c2 = c1 + SSM_CONV_DIM
    c3 = c2 + MEM_W
    z_ref[...] = jnp.dot(h, w_ref[:, :c1], preferred_element_type=F32).astype(BF16)
    xbc_ref[...] = jnp.dot(h, w_ref[:, c1:c2], preferred_element_type=F32).astype(BF16)
    mq_ref[...] = jnp.dot(h, w_ref[:, c2:c3], preferred_element_type=F32).astype(BF16)
    dt_ref[...] = jnp.dot(h, w_ref[:, c3:], preferred_element_type=F32)


def _inproj1(h, w, tm=512):
    t = h.shape[0]
    row = lambda n: pl.BlockSpec((tm, n), lambda i: (i, 0))
    return pl.pallas_call(
        _inproj1_kernel,
        out_shape=[jax.ShapeDtypeStruct((t, TOK_W), BF16), jax.ShapeDtypeStruct((t, SSM_CONV_DIM), BF16),
                   jax.ShapeDtypeStruct((t, MEM_W), BF16), jax.ShapeDtypeStruct((t, LANES), F32)],
        grid=(t // tm,),
        in_specs=[row(D_MODEL), pl.BlockSpec(w.shape, lambda i: (0, 0))],
        out_specs=[row(TOK_W), row(SSM_CONV_DIM), row(MEM_W), row(LANES)],
        compiler_params=_cparams(("parallel",)),
        name="inproj1",
    )(h, w)


def _split_dot(a, b, terms, split_rhs):
    rem = b if split_rhs else a
    out = None
    for _ in range(terms):
        piece = rem.astype(BF16)
        part = (jnp.dot(a, piece, preferred_element_type=F32) if split_rhs
                else jnp.dot(piece, b, preferred_element_type=F32))
        out = part if out is None else out + part
        rem = rem - piece.astype(F32)
    return out


def _ssd_kernel(z_ref, xbc_ref, dt_ref, cw_ref, cb_ref, dtb_ref, alog_ref, dsk_ref, ng_ref, ex_ref,
                o_ref, xe_ref, st_ref, *, tl):
    j = pl.program_id(1)
    pad = 8

    @pl.when(j == 0)
    def _():
        xe_ref[0:pad, :] = jnp.zeros((pad, SSM_CONV_DIM), F32)
        st_ref[...] = jnp.zeros_like(st_ref)

    x = xbc_ref[0].astype(F32)
    xe_ref[pad:pad + tl, :] = x
    acc = cb_ref[...] + cw_ref[SSM_CONV_K - 1:SSM_CONV_K, :] * x
    for k in range(SSM_CONV_K - 1):
        sh = SSM_CONV_K - 1 - k
        acc = acc + cw_ref[k:k + 1, :] * xe_ref[pad - sh:pad - sh + tl, :]
    xe_ref[0:pad, :] = x[tl - pad:tl, :]
    xc = acc * jax.nn.sigmoid(acc)
    xs = xc[:, :TOK_W]

    dt_in = dt_ref[0] + dtb_ref[...]
    dt = jnp.maximum(dt_in, 0.0) + jnp.log1p(jnp.exp(-jnp.abs(dt_in)))
    a = -jnp.exp(alog_ref[...])
    dta = dt * a
    r = lax.broadcasted_iota(jnp.int32, (tl, tl), 0)
    c = lax.broadcasted_iota(jnp.int32, (tl, tl), 1)
    tril = r >= c
    ltri = jnp.where(tril, 1.0, 0.0).astype(BF16)
    cum = _split_dot(ltri, dta, terms=3, split_rhs=True)
    cum_t = cum.T
    dt_t = dt.T
    lane = lax.broadcasted_iota(jnp.int32, (tl, LANES), 1)

    hg = SSM_HEADS // SSM_GROUPS
    y_pairs = []
    cbs = []
    for g in range(SSM_GROUPS):
        bm = xc[:, TOK_W + g * SSM_STATE:TOK_W + (g + 1) * SSM_STATE].astype(BF16)
        cm = xc[:, TOK_W + (SSM_GROUPS + g) * SSM_STATE:TOK_W + (SSM_GROUPS + g + 1) * SSM_STATE].astype(BF16)
        cbs.append(lax.dot_general(cm, bm, (((1,), (1,)), ((), ())), preferred_element_type=F32))
    for pr in range(SSM_HEADS // 2):
        xp = xs[:, pr * LANES:(pr + 1) * LANES].astype(BF16)
        ys = []
        for hh in (2 * pr, 2 * pr + 1):
            g = hh // hg
            seg = cum[:, hh:hh + 1] - cum_t[hh:hh + 1, :]
            decay = jnp.exp(jnp.where(tril, seg, NEG))
            w = cbs[g] * decay * dt_t[hh:hh + 1, :]
            ys.append(jnp.dot(w.astype(BF16), xp, preferred_element_type=F32))
        y_pairs.append(jnp.where(lane < SSM_HD, ys[0], ys[1]))
    y = jnp.concatenate(y_pairs, axis=1)

    expcum = jnp.exp(cum)
    to_end = jnp.exp(cum[tl - 1:tl, :] - cum) * dt
    stacked = jnp.concatenate([expcum, to_end], axis=0)
    exd = _split_dot(stacked, ex_ref[...], terms=2, split_rhs=False)
    expcum_x = exd[:tl]
    xw = (xs * exd[tl:]).astype(BF16)
    gw = TOK_W // SSM_GROUPS
    y_off = []
    for g in range(SSM_GROUPS):
        cs = slice(g * gw, (g + 1) * gw)
        bm_t = xc[:, TOK_W + g * SSM_STATE:TOK_W + (g + 1) * SSM_STATE].T.astype(BF16)
        cm = xc[:, TOK_W + (SSM_GROUPS + g) * SSM_STATE:TOK_W + (SSM_GROUPS + g + 1) * SSM_STATE].astype(BF16)
        sg = st_ref[:, cs]
        y_off.append(jnp.dot(cm, sg.astype(BF16), preferred_element_type=F32) * expcum_x[:, cs])
        st_ref[:, cs] = sg * expcum_x[tl - 1:tl, cs] + jnp.dot(bm_t, xw[:, cs], preferred_element_type=F32)
    y = y + jnp.concatenate(y_off, axis=1) + dsk_ref[...] * xs
    zf = z_ref[0].astype(F32)
    y = y * (zf * jax.nn.sigmoid(zf))
    outs = []
    for g in range(SSM_GROUPS):
        cs = slice(g * gw, (g + 1) * gw)
        outs.append(_rms(y[:, cs], ng_ref[:, cs]))
    o_ref[0] = jnp.concatenate(outs, axis=1).astype(o_ref.dtype)


def _ssd(z3, xbc3, dt3, cw, cb, dtb, alog, dsk, ng, ex, tl=256):
    b, s, _ = z3.shape
    full = lambda a: pl.BlockSpec(a.shape, lambda bi, j: (0, 0))
    blk = lambda n: pl.BlockSpec((1, tl, n), lambda bi, j: (bi, j, 0))
    return pl.pallas_call(
        functools.partial(_ssd_kernel, tl=tl),
        out_shape=jax.ShapeDtypeStruct((b, s, TOK_W), BF16),
        grid=(b, s // tl),
        in_specs=[blk(TOK_W), blk(SSM_CONV_DIM), blk(LANES), full(cw), full(cb), full(dtb), full(alog),
                  full(dsk), full(ng), full(ex)],
        out_specs=blk(TOK_W),
        scratch_shapes=[pltpu.VMEM((tl + 8, SSM_CONV_DIM), F32), pltpu.VMEM((SSM_STATE, TOK_W), F32)],
        compiler_params=_cparams(("parallel", "arbitrary")),
        name="conv_ssd",
    )(z3, xbc3, dt3, cw, cb, dtb, alog, dsk, ng, ex)


def _pad_lanes(v, n=LANES):
    return jnp.pad(v, [(0, 0)] * (v.ndim - 1) + [(0, n - v.shape[-1])])


def kernel(x, mem, ln1_g, ln2_g, mem_norm_g, w_out, mem_w_kv, mem_qn_g, mem_kn_g, da_w_in, da_qn_g, da_kn_g,
           da_lq1, da_lk1, da_lq2, da_lk2, da_sub_g, ssm_w_in, ssm_conv_w, ssm_conv_b, ssm_dt_bias, ssm_a_log,
           ssm_d, ssm_norm_g, ffn_w_gate, ffn_w_up, ffn_w_down, moe_w_router, moe_w_gate, moe_w_up, moe_w_down):
    b, s, d = x.shape
    t = b * s
    row = lambda v: v.reshape(1, -1).astype(F32)
    xt = x.reshape(t, d)

    mem_qg = jnp.tile(mem_qn_g.astype(F32) * (MEM_HD ** -0.5), (1, MEM_HEADS))
    mem_kg = jnp.tile(mem_kn_g.astype(F32), (1, MEM_HEADS))[:, None, :]
    kv = _memkv(mem, row(mem_norm_g), mem_w_kv.astype(BF16), mem_kg)
    wo = w_out.astype(BF16)

    lambda_init = 0.8 - 0.6 * math.exp(-0.3 * 0)
    qkg = jnp.concatenate([jnp.tile(da_qn_g[0].astype(F32) * (DA_DH ** -0.5 * LOG2E), 2 * DA_HEADS),
                           jnp.tile(da_kn_g[0].astype(F32), 2 * DA_HEADS)]).reshape(1, -1)
    qt, kn, vt, mq0 = _inproj0(xt, row(ln1_g[0]), da_w_in[0].astype(BF16), qkg, b)
    tok = _diff_attention(qt, kn.reshape(b, s, TOK_W), vt, row(da_lq1[0]), row(da_lk1[0]), row(da_lq2[0]),
                          row(da_lk2[0]), row(da_sub_g[0]), lambda_init)
    x1, h2 = _mixout(tok.reshape(t, TOK_W), mq0, 0, kv[0], mem_qg[0:1], wo[0], xt, row(ln2_g[0]))
    x2, h3 = _ffn(h2, x1, ffn_w_gate[0].astype(BF16), ffn_w_up[0].astype(BF16), ffn_w_down[0].astype(BF16),
                  row(ln1_g[1]))

    w_in = ssm_w_in[0]
    o2 = TOK_W + SSM_CONV_DIM
    o3 = o2 + SSM_HEADS
    w1 = jnp.concatenate([w_in[:, :o2], w_in[:, o3:], _pad_lanes(w_in[:, o2:o3])], axis=1).astype(BF16)
    z, xbc, mq, dt = _inproj1(h3, w1)
    expand = jnp.repeat(jnp.eye(SSM_HEADS, dtype=F32), SSM_HD, axis=1)
    expand = jnp.pad(expand, ((0, LANES - SSM_HEADS), (0, 0))).astype(BF16)
    dsk = jnp.repeat(ssm_d[0].astype(F32), SSM_HD).reshape(1, -1)
    tok1 = _ssd(z.reshape(b, s, -1), xbc.reshape(b, s, -1), dt.reshape(b, s, -1),
                ssm_conv_w[0].astype(F32), row(ssm_conv_b[0]), _pad_lanes(row(ssm_dt_bias[0])),
                _pad_lanes(row(ssm_a_log[0])), dsk, row(ssm_norm_g[0]), expand)
    x3, hp, route, idx, counts = _mixout(tok1.reshape(t, TOK_W), mq, 0, kv[1], mem_qg[1:2], wo[1], x2,
                                         row(ln2_g[1]), wr=_pad_lanes(moe_w_router[0].astype(F32)))
    x4 = _moe(hp, route, idx, counts, x3, moe_w_gate[0].astype(BF16), moe_w_up[0].astype(BF16),
              moe_w_down[0].astype(BF16))
    return x4.reshape(b, s, d)
```

```python
import functools
import math

import jax
import jax.numpy as jnp
from jax import lax
from jax.experimental import pallas as pl
from jax.experimental.pallas import tpu as pltpu

F32 = jnp.float32
BF16 = jnp.bfloat16

D_MODEL = 1024
CHUNK = 64
MEM_LEN = 256
MEM_W = 256
MEM_HEADS = 4
MEM_HD = 64
TOK_W = 768
DA_DH = 64
DA_HEADS = 6
SSM_HD = 64
SSM_HEADS = 12
SSM_GROUPS = 2
SSM_STATE = 128
SSM_CONV_K = 4
SSM_CONV_DIM = 1280
N_EXPERTS = 8
EPS = 1e-6
LANES = 128
NEG = -1e30
SUBLANES = 8
ONES_ROWS = SUBLANES
LOG2E = math.log2(math.e)
ISSUE_UNROLL = True
VMEM_LIMIT = 56 * 1024 * 1024


def _cparams(sem):
    return pltpu.CompilerParams(dimension_semantics=sem, vmem_limit_bytes=VMEM_LIMIT)


def _rms(xf, g):
    ms = jnp.mean(xf * xf, axis=-1, keepdims=True)
    return xf * lax.rsqrt(ms + EPS) * g


def _seg_mean_matrix(n, seg_shift):
    r = lax.broadcasted_iota(jnp.int32, (n, n), 0) >> seg_shift
    c = lax.broadcasted_iota(jnp.int32, (n, n), 1) >> seg_shift
    return jnp.where(r == c, 1.0 / (1 << seg_shift), 0.0).astype(BF16)


def _seg_mean_sq(y, seg_shift=6):
    bd = _seg_mean_matrix(256, seg_shift)
    sq = (y * y).astype(BF16)
    parts = [jnp.dot(sq[:, c:c + 256], bd, preferred_element_type=F32)
             for c in range(0, y.shape[1], 256)]
    return parts[0] if len(parts) == 1 else jnp.concatenate(parts, axis=1)


def _inproj0_kernel(x_ref, g_ref, w_ref, qkg_ref, qt_ref, k_ref, vt_ref, mq_ref):
    h = _rms(x_ref[...], g_ref[...]).astype(BF16)
    u = jnp.dot(h, w_ref[...], preferred_element_type=F32)
    tm = u.shape[0]
    hd = 2 * DA_DH
    nqk = 2 * TOK_W
    qk = u[:, :nqk]
    qkn = qk * lax.rsqrt(_seg_mean_sq(qk) + EPS) * qkg_ref[...]
    qt_ref[0] = qkn[:, :TOK_W].T.reshape(DA_HEADS, hd, tm).astype(BF16)
    k_ref[...] = qkn[:, TOK_W:].astype(BF16)
    vt_ref[0, :, 0, :hd, :] = u[:, nqk:nqk + TOK_W].T.reshape(DA_HEADS, hd, tm).astype(BF16)
    vt_ref[0, :, 0, hd:, :] = jnp.ones((DA_HEADS, ONES_ROWS, tm), BF16)
    mq_ref[...] = u[:, nqk + TOK_W:].astype(BF16)


def _inproj0(x, g, w, qkg, b, tm=512):
    t = x.shape[0]
    n = w.shape[1]
    s = t // b
    nt = s // tm
    hd = 2 * DA_DH
    return pl.pallas_call(
        _inproj0_kernel,
        out_shape=[jax.ShapeDtypeStruct((b, DA_HEADS, hd, s), BF16),
                   jax.ShapeDtypeStruct((t, TOK_W), BF16),
                   jax.ShapeDtypeStruct((b, DA_HEADS, nt, hd + ONES_ROWS, tm), BF16),
                   jax.ShapeDtypeStruct((t, MEM_W), BF16)],
        grid=(t // tm,),
        in_specs=[pl.BlockSpec((tm, D_MODEL), lambda i: (i, 0)),
                  pl.BlockSpec((1, D_MODEL), lambda i: (0, 0)),
                  pl.BlockSpec((D_MODEL, n), lambda i: (0, 0)),
                  pl.BlockSpec((1, 2 * TOK_W), lambda i: (0, 0))],
        out_specs=[pl.BlockSpec((1, DA_HEADS, hd, tm), lambda i: (i // nt, 0, 0, i % nt)),
                   pl.BlockSpec((tm, TOK_W), lambda i: (i, 0)),
                   pl.BlockSpec((1, DA_HEADS, 1, hd + ONES_ROWS, tm), lambda i: (i // nt, 0, i % nt, 0, 0)),
                   pl.BlockSpec((tm, MEM_W), lambda i: (i, 0))],
        compiler_params=_cparams(("parallel",)),
        name="inproj0",
    )(x, g, w, qkg)


def _attn_kernel(lq1_ref, lk1_ref, lq2_ref, lk2_ref, subg_ref, qt_ref, k_ref, vt_ref, o_ref,
                 acc_ref, m_ref, alpha_ref, s_ref, p_ref, *, tq, tk, nh, lambda_init):
    qi = pl.program_id(2)
    hd = 2 * DA_DH
    qs = []
    for hs in range(nh):
        qt = qt_ref[0, hs]
        row = lax.broadcasted_iota(jnp.int32, qt.shape, 0)
        zero = jnp.zeros_like(qt)
        qs += [jnp.where(row < DA_DH, qt, zero), jnp.where(row >= DA_DH, qt, zero)]
    acc_ref[...] = jnp.zeros_like(acc_ref)
    m_ref[...] = jnp.full_like(m_ref, NEG)
    p_ref[1] = jnp.zeros(p_ref.shape[1:], p_ref.dtype)
    alpha_ref[1] = jnp.ones(alpha_ref.shape[1:], alpha_ref.dtype)

    def scores(ki, slot, c):
        start = pl.multiple_of(ki * tk, tk)
        hs = c // 2
        k = k_ref[0, pl.ds(start, tk), hs * hd:(hs + 1) * hd]
        s_ref[slot, c] = jnp.dot(k, qs[c], preferred_element_type=F32)

    def softmax(slot, masked, c):
        s = s_ref[slot, c]
        if masked:
            kc = lax.broadcasted_iota(jnp.int32, (tk, tq), 0) >> 6
            qc = lax.broadcasted_iota(jnp.int32, (tk, tq), 1) >> 6
            s = jnp.where(kc <= qc, s, NEG)
        m_old = m_ref[c]
        m_new = jnp.maximum(m_old, jnp.max(s, axis=0, keepdims=True))
        alpha_ref[slot, c] = jnp.exp2(m_old - m_new)
        p_ref[slot, c] = jnp.exp2(s - m_new).astype(BF16)
        m_ref[c] = m_new

    def values(ki, slot, c):
        vt = vt_ref[0, c // 2, ki]
        acc_ref[c] = alpha_ref[slot, c] * acc_ref[c] + jnp.dot(vt, p_ref[slot, c], preferred_element_type=F32)

    groups = [(2 * hs, 2 * hs + 1) for hs in range(nh)]
    odd = (qi & 1) == 1

    def stage(fn, g, *args):
        for c in g:
            fn(*args, c)

    @pl.when(odd)
    def _():
        for g in groups:
            stage(scores, g, 0, 1)
            stage(scores, g, 1, 0)
            stage(softmax, g, 1, False)

    @pl.when(jnp.logical_not(odd))
    def _():
        for g in groups:
            stage(scores, g, 0, 0)

    def body(j, carry):
        ki = (qi & 1) + 2 * j
        for g in groups:
            stage(scores, g, ki + 1, 1)
            stage(softmax, g, 0, False)
            stage(values, g, jnp.maximum(ki - 1, 0), 1)
            stage(scores, g, ki + 2, 0)
            stage(softmax, g, 1, False)
            stage(values, g, ki, 0)
        return carry

    lax.fori_loop(0, qi >> 1, body, 0)
    for g in groups:
        stage(softmax, g, 0, True)
        stage(values, g, jnp.maximum(qi - 1, 0), 1)
        stage(values, g, qi, 0)

    lam = (jnp.exp(jnp.sum(lq1_ref[...] * lk1_ref[...])) - jnp.exp(jnp.sum(lq2_ref[...] * lk2_ref[...]))
           + lambda_init)
    for hs in range(nh):
        o1 = acc_ref[2 * hs, :hd, :] / acc_ref[2 * hs, hd:hd + 1, :]
        o2 = acc_ref[2 * hs + 1, :hd, :] / acc_ref[2 * hs + 1, hd:hd + 1, :]
        ot = o1 - lam * o2
        ms = jnp.mean(ot * ot, axis=0, keepdims=True)
        ot = ot * lax.rsqrt(ms + EPS)
        o_ref[0, :, hs * hd:(hs + 1) * hd] = (ot.T * (subg_ref[...] * (1.0 - lambda_init))).astype(o_ref.dtype)


def _diff_attention(qt, k3, vt, lq1, lk1, lq2, lk2, subg, lambda_init, nh=3):
    b, s, _ = k3.shape
    tk = vt.shape[-1]
    tq = tk
    nq = s // tq
    hd = 2 * DA_DH
    hde = hd + ONES_ROWS
    nc = 2 * nh
    vec = lambda n: pl.BlockSpec((1, n), lambda bi, h, qi: (0, 0))
    kern = functools.partial(_attn_kernel, tq=tq, tk=tk, nh=nh, lambda_init=lambda_init)
    return pl.pallas_call(
        kern,
        out_shape=jax.ShapeDtypeStruct((b, s, TOK_W), BF16),
        grid=(b, DA_HEADS // nh, nq),
        in_specs=[vec(DA_DH), vec(DA_DH), vec(DA_DH), vec(DA_DH), vec(hd),
                  pl.BlockSpec((1, nh, hd, tq), lambda bi, h, qi: (bi, h, 0, qi)),
                  pl.BlockSpec((1, s, nh * hd), lambda bi, h, qi: (bi, 0, h)),
                  pl.BlockSpec((1, nh, s // tk, hde, tk), lambda bi, h, qi: (bi, h, 0, 0, 0))],
        out_specs=pl.BlockSpec((1, tq, nh * hd), lambda bi, h, qi: (bi, qi, h)),
        scratch_shapes=[pltpu.VMEM((nc, hde, tq), F32), pltpu.VMEM((nc, 1, tq), F32),
                        pltpu.VMEM((2, nc, 1, tq), F32), pltpu.VMEM((2, nc, tk, tq), F32),
                        pltpu.VMEM((2, nc, tk, tq), BF16)],
        compiler_params=_cparams(("parallel", "parallel", "parallel")),
        name="diff_attn",
    )(lq1, lk1, lq2, lk2, subg, qt, k3, vt)


def _memkv_kernel(mem_ref, g_ref, wkv_ref, kng_ref, kv_ref):
    mn = _rms(mem_ref[0], g_ref[...]).astype(BF16)
    kv = jnp.dot(mn, wkv_ref[0], preferred_element_type=F32)
    k = kv[:, :MEM_W]
    kn = k * lax.rsqrt(_seg_mean_sq(k) + EPS) * kng_ref[0]
    kv_ref[0, 0, :, :MEM_W] = kn.astype(BF16)
    kv_ref[0, 0, :, MEM_W:] = kv[:, MEM_W:].astype(BF16)


def _memkv(mem, g, wkv, kng):
    depth = wkv.shape[0]
    b = mem.shape[0]
    return pl.pallas_call(
        _memkv_kernel,
        out_shape=jax.ShapeDtypeStruct((depth, b, MEM_LEN, 2 * MEM_W), BF16),
        grid=(depth, b),
        in_specs=[pl.BlockSpec((1, MEM_LEN, D_MODEL), lambda d, bi: (bi, 0, 0)),
                  pl.BlockSpec((1, D_MODEL), lambda d, bi: (0, 0)),
                  pl.BlockSpec((1, D_MODEL, 2 * MEM_W), lambda d, bi: (d, 0, 0)),
                  pl.BlockSpec((1, 1, MEM_W), lambda d, bi: (d, 0, 0))],
        out_specs=pl.BlockSpec((1, 1, MEM_LEN, 2 * MEM_W), lambda d, bi: (d, bi, 0, 0)),
        compiler_params=_cparams(("parallel", "parallel")),
        name="memkv",
    )(mem, g, wkv, kng)


def _mem_attention(mq, qg, kv):
    qn = (mq * lax.rsqrt(_seg_mean_sq(mq) + EPS) * qg).astype(BF16)
    k = kv[:, :MEM_W]
    v = kv[:, MEM_W:]
    lane = lax.broadcasted_iota(jnp.int32, qn.shape, 1) >> 6
    out = jnp.zeros(qn.shape, F32)
    for h in range(MEM_HEADS):
        sel = lane == h
        qh = jnp.where(sel, qn, jnp.zeros_like(qn))
        s = lax.dot_general(qh, k, (((1,), (1,)), ((), ())), preferred_element_type=F32)
        p = jnp.exp(s - jnp.max(s, axis=1, keepdims=True))
        l = jnp.sum(p, axis=1, keepdims=True)
        o = jnp.dot(p.astype(BF16), v, preferred_element_type=F32)
        out = jnp.where(sel, o / l, out)
    return out


R_E1, R_E2, R_G1, R_G2, R_RANK1, R_RANK2 = range(6)


def _top2_route(logits, count_ref):
    assert N_EXPERTS == SUBLANES
    tm = logits.shape[0]
    lt = logits.T[:SUBLANES, :]
    row = lax.broadcasted_iota(jnp.int32, lt.shape, 0)
    m1 = jnp.max(lt, axis=0, keepdims=True)
    i1 = jnp.min(jnp.where(lt == m1, row, N_EXPERTS), axis=0, keepdims=True)
    lt2 = jnp.where(row == i1, NEG, lt)
    m2 = jnp.max(lt2, axis=0, keepdims=True)
    i2 = jnp.min(jnp.where(lt2 == m2, row, N_EXPERTS), axis=0, keepdims=True)
    g1 = 1.0 / (1.0 + jnp.exp(m2 - m1))
    g2 = 1.0 - g1
    chosen = jnp.logical_or(row == i1, row == i2)
    onehot = jnp.where(chosen, 1.0, 0.0)
    r = lax.broadcasted_iota(jnp.int32, (tm, tm), 0)
    c = lax.broadcasted_iota(jnp.int32, (tm, tm), 1)
    earlier = jnp.where(r < c, 1.0, 0.0).astype(BF16)
    onehot16 = jnp.concatenate([onehot, jnp.zeros_like(onehot)], axis=0).astype(BF16)
    prefix = jnp.dot(onehot16, earlier, preferred_element_type=F32)[:SUBLANES] + count_ref[:, :1]
    count_ref[...] += jnp.sum(onehot, axis=1, keepdims=True)
    rank1 = jnp.sum(jnp.where(row == i1, prefix, 0.0), axis=0, keepdims=True)
    rank2 = jnp.sum(jnp.where(row == i2, prefix, 0.0), axis=0, keepdims=True)
    rec = jnp.zeros(lt.shape, F32)
    for idx, val in ((R_E1, i1.astype(F32)), (R_E2, i2.astype(F32)), (R_G1, g1), (R_G2, g2),
                     (R_RANK1, rank1), (R_RANK2, rank2)):
        rec = jnp.where(row == idx, val, rec)
    return rec


def _mixout_kernel(*refs, with_router):
    if with_router:
        (tok_ref, mq_ref, kv_ref, qg_ref, wo_ref, x_ref, g2_ref, wr_ref,
         x1_ref, hp_ref, route_ref, idx_ref, count_ref) = refs
    else:
        tok_ref, mq_ref, kv_ref, qg_ref, wo_ref, x_ref, g2_ref, x1_ref, h2_ref = refs
    mo = _mem_attention(mq_ref[...].astype(F32), qg_ref[...], kv_ref[0])
    y = jnp.dot(tok_ref[...], wo_ref[:TOK_W, :], preferred_element_type=F32)
    y = y + jnp.dot(mo.astype(BF16), wo_ref[TOK_W:, :], preferred_element_type=F32)
    x1 = x_ref[...] + y
    x1_ref[...] = x1
    h2 = _rms(x1, g2_ref[...])
    if with_router:
        @pl.when(pl.program_id(0) == 0)
        def _():
            count_ref[...] = jnp.zeros_like(count_ref)

        hp_ref[...] = h2
        wr = wr_ref[...]
        w_hi = wr.astype(BF16)
        w_lo = (wr - w_hi.astype(F32)).astype(BF16)
        h_hi = h2.astype(BF16)
        h_lo = (h2 - h_hi.astype(F32)).astype(BF16)
        both = jnp.dot(h_hi, jnp.concatenate([w_hi, w_lo], axis=1), preferred_element_type=F32)
        logits = both[:, :LANES] + both[:, LANES:] + jnp.dot(h_lo, w_hi, preferred_element_type=F32)
        rec = _top2_route(logits, count_ref)
        idx_ref[0] = rec.astype(jnp.int32)
        route_ref[...] = jnp.concatenate([rec, jnp.zeros((LANES - SUBLANES, rec.shape[1]), F32)], axis=0).T
    else:
        h2_ref[...] = h2.astype(BF16)


def _mixout(tok, mq_src, mq_block, kv, qg, wo, x, g2, wr=None, tm=512):
    t = x.shape[0]
    tiles_per_batch = t // kv.shape[0] // tm
    with_router = wr is not None
    in_specs = [pl.BlockSpec((tm, TOK_W), lambda i: (i, 0)),
                pl.BlockSpec((tm, MEM_W), lambda i: (i, mq_block)),
                pl.BlockSpec((1, MEM_LEN, 2 * MEM_W), lambda i: (i // tiles_per_batch, 0, 0)),
                pl.BlockSpec((1, MEM_W), lambda i: (0, 0)),
                pl.BlockSpec((D_MODEL, D_MODEL), lambda i: (0, 0)),
                pl.BlockSpec((tm, D_MODEL), lambda i: (i, 0)),
                pl.BlockSpec((1, D_MODEL), lambda i: (0, 0))]
    args = [tok, mq_src, kv, qg, wo, x, g2]
    row_spec = lambda n: pl.BlockSpec((tm, n), lambda i: (i, 0))
    if with_router:
        in_specs.append(pl.BlockSpec((D_MODEL, LANES), lambda i: (0, 0)))
        args.append(wr)
        out_shape = [jax.ShapeDtypeStruct((t, D_MODEL), F32), jax.ShapeDtypeStruct((t, D_MODEL), F32),
                     jax.ShapeDtypeStruct((t, LANES), F32), jax.ShapeDtypeStruct((t // tm, SUBLANES, tm), jnp.int32),
                     jax.ShapeDtypeStruct((SUBLANES, LANES), F32)]
        out_specs = [row_spec(D_MODEL), row_spec(D_MODEL), row_spec(LANES),
                     pl.BlockSpec((1, SUBLANES, tm), lambda i: (i, 0, 0)),
                     pl.BlockSpec((SUBLANES, LANES), lambda i: (0, 0))]
    else:
        out_shape = [jax.ShapeDtypeStruct((t, D_MODEL), F32), jax.ShapeDtypeStruct((t, D_MODEL), BF16)]
        out_specs = [row_spec(D_MODEL), row_spec(D_MODEL)]
    return pl.pallas_call(
        functools.partial(_mixout_kernel, with_router=with_router),
        out_shape=out_shape,
        grid=(t // tm,),
        in_specs=in_specs,
        out_specs=out_specs,
        compiler_params=_cparams(("arbitrary",) if with_router else ("parallel",)),
        name="mixout_router" if with_router else "mixout",
    )(*args)


def _ffn_kernel(h_ref, x_ref, wg_ref, wu_ref, wd_ref, gn_ref, xo_ref, ho_ref, acc_ref):
    f = pl.program_id(1)
    h = h_ref[...]
    a = jnp.dot(h, wg_ref[...], preferred_element_type=F32)
    u = jnp.dot(h, wu_ref[...], preferred_element_type=F32)
    act = a * jax.nn.sigmoid(a) * u
    contrib = jnp.dot(act.astype(BF16), wd_ref[...], preferred_element_type=F32)
    acc_ref[...] = jnp.where(f == 0, x_ref[...], acc_ref[...]) + contrib

    @pl.when(f == pl.num_programs(1) - 1)
    def _():
        xn = acc_ref[...]
        xo_ref[...] = xn
        ho_ref[...] = _rms(xn, gn_ref[...]).astype(BF16)


def _ffn(h, x, wg, wu, wd, gn, tm=512, tf=1408):
    t = x.shape[0]
    fdim = wg.shape[1]
    row = pl.BlockSpec((tm, D_MODEL), lambda i, f: (i, 0))
    return pl.pallas_call(
        _ffn_kernel,
        out_shape=[jax.ShapeDtypeStruct((t, D_MODEL), F32), jax.ShapeDtypeStruct((t, D_MODEL), BF16)],
        grid=(t // tm, fdim // tf),
        in_specs=[row, row,
                  pl.BlockSpec((D_MODEL, tf), lambda i, f: (0, f)),
                  pl.BlockSpec((D_MODEL, tf), lambda i, f: (0, f)),
                  pl.BlockSpec((tf, D_MODEL), lambda i, f: (f, 0)),
                  pl.BlockSpec((1, D_MODEL), lambda i, f: (0, 0))],
        out_specs=[row, row],
        scratch_shapes=[pltpu.VMEM((tm, D_MODEL), F32)],
        compiler_params=_cparams(("parallel", "arbitrary")),
        name="ffn",
    )(h, x, wg, wu, wd, gn)


def _row_copy(src_ref, src_row, dst_ref, dst_row, sem):
    return pltpu.make_async_copy(src_ref.at[pl.ds(src_row, 1), :], dst_ref.at[pl.ds(dst_row, 1), :], sem)


def _dispatch_kernel(pad_start_ref, pad_len_ref, pos_ref, hp_ref, xs_ref, zeros_ref, sem, zsem, *, tm, tmr):
    nbits = tmr.bit_length() - 1

    @pl.when(pl.program_id(0) == 0)
    def _():
        zeros_ref[...] = jnp.zeros_like(zeros_ref)
        for e in range(N_EXPERTS):
            start = pad_start_ref[e]
            length = pad_len_ref[e]
            singles = length & (SUBLANES - 1)
            for j in range(SUBLANES - 1):
                @pl.when(j < singles)
                def _(j=j, start=start):
                    cp = _row_copy(zeros_ref, 0, xs_ref, start + j, zsem)
                    cp.start()
                    cp.wait()

            done = start + singles
            for b in range(SUBLANES.bit_length() - 1, nbits):
                n = 1 << b
                bit = (length >> b) & 1

                @pl.when(bit == 1)
                def _(n=n, done=done):
                    cp = pltpu.make_async_copy(zeros_ref.at[pl.ds(0, n), :],
                                               xs_ref.at[pl.ds(pl.multiple_of(done, SUBLANES), n), :], zsem)
                    cp.start()
                    cp.wait()

                done = done + bit * n

        n_tiles = xs_ref.shape[0] // tmr
        half = tmr // 2
        for j in range(n_tiles - N_EXPERTS, n_tiles):
            @pl.when(j * tmr >= pad_start_ref[N_EXPERTS])
            def _(j=j):
                for c in range(2):
                    cp = pltpu.make_async_copy(zeros_ref, xs_ref.at[pl.ds(j * tmr + c * half, half), :], zsem)
                    cp.start()
                    cp.wait()

    def issue(r, carry):
        _row_copy(hp_ref, r, xs_ref, pos_ref[0, 0, r], sem).start(priority=0)
        _row_copy(hp_ref, r, xs_ref, pos_ref[0, 1, r], sem).start(priority=1)
        return carry

    lax.fori_loop(0, tm, issue, 0, unroll=ISSUE_UNROLL)
    for _ in range(2):
        pltpu.make_async_copy(hp_ref, xs_ref.at[pl.ds(0, tm), :], sem).wait()


def _dispatch(pad_start, pad_len, pos, hp, n_rows, tm, tmr):
    t, w = hp.shape
    return pl.pallas_call(
        functools.partial(_dispatch_kernel, tm=tm, tmr=tmr),
        out_shape=jax.ShapeDtypeStruct((n_rows, w), hp.dtype),
        grid_spec=pltpu.PrefetchScalarGridSpec(
            num_scalar_prefetch=2,
            grid=(t // tm,),
            in_specs=[pl.BlockSpec((1, 2, tm), lambda i, ps, pn: (i, 0, 0), memory_space=pltpu.SMEM),
                      pl.BlockSpec((tm, w), lambda i, ps, pn: (i, 0))],
            out_specs=pl.BlockSpec(memory_space=pl.ANY),
            scratch_shapes=[pltpu.VMEM((tmr // 2, w), hp.dtype), pltpu.SemaphoreType.DMA,
                            pltpu.SemaphoreType.DMA]),
        compiler_params=_cparams(("arbitrary",)),
        name="moe_dispatch",
    )(pad_start, pad_len, pos, hp)


def _experts_kernel(te_ref, tv_ref, xs_ref, wg_ref, wu_ref, wd_ref, y_ref):
    i = pl.program_id(0)
    f = pl.program_id(1)

    @pl.when(tv_ref[i] == 1)
    def _():
        h = xs_ref[...].astype(BF16)
        a = jnp.dot(h, wg_ref[0], preferred_element_type=F32)
        u = jnp.dot(h, wu_ref[0], preferred_element_type=F32)
        act = a * jax.nn.sigmoid(a) * u
        contrib = jnp.dot(act.astype(BF16), wd_ref[0], preferred_element_type=F32)
        y_ref[...] = jnp.where(f == 0, 0.0, y_ref[...]) + contrib

    @pl.when(jnp.logical_and(tv_ref[i] == 0, f == 0))
    def _():
        y_ref[...] = jnp.zeros_like(y_ref)


def _experts(tile_expert, tile_valid, xs, wg, wu, wd, tmr, tf):
    n_rows, w = xs.shape
    fdim = wg.shape[2]
    nf = fdim // tf
    fidx = lambda i, f, te, tv: jnp.where(tv[i] == 1, f, nf - 1)
    return pl.pallas_call(
        _experts_kernel,
        out_shape=jax.ShapeDtypeStruct((n_rows, D_MODEL), F32),
        grid_spec=pltpu.PrefetchScalarGridSpec(
            num_scalar_prefetch=2,
            grid=(n_rows // tmr, nf),
            in_specs=[pl.BlockSpec((tmr, w), lambda i, f, te, tv: (jnp.where(tv[i] == 1, i, 0), 0)),
                      pl.BlockSpec((1, D_MODEL, tf), lambda i, f, te, tv: (te[i], 0, fidx(i, f, te, tv))),
                      pl.BlockSpec((1, D_MODEL, tf), lambda i, f, te, tv: (te[i], 0, fidx(i, f, te, tv))),
                      pl.BlockSpec((1, tf, D_MODEL), lambda i, f, te, tv: (te[i], fidx(i, f, te, tv), 0))],
            out_specs=pl.BlockSpec((tmr, D_MODEL), lambda i, f, te, tv: (i, 0))),
        compiler_params=_cparams(("parallel", "arbitrary")),
        name="moe_experts",
    )(tile_expert, tile_valid, xs, wg, wu, wd)


def _combine_kernel(pos_ref, route_ref, x_ref, y_ref, o_ref, buf_ref, sem, *, tm):
    def issue(r, carry):
        _row_copy(y_ref, pos_ref[0, 0, r], buf_ref.at[0], r, sem).start(priority=0)
        _row_copy(y_ref, pos_ref[0, 1, r], buf_ref.at[1], r, sem).start(priority=1)
        return carry

    lax.fori_loop(0, tm, issue, 0, unroll=ISSUE_UNROLL)
    for k in range(2):
        pltpu.make_async_copy(y_ref.at[pl.ds(0, tm), :], buf_ref.at[k], sem).wait()
    route = route_ref[...]
    g1 = route[:, R_G1:R_G1 + 1]
    g2 = route[:, R_G2:R_G2 + 1]
    o_ref[...] = x_ref[...] + (g1 * buf_ref[0] + g2 * buf_ref[1])


def _combine(pos, route, x, y, tm):
    t = x.shape[0]
    return pl.pallas_call(
        functools.partial(_combine_kernel, tm=tm),
        out_shape=jax.ShapeDtypeStruct((t, D_MODEL), F32),
        grid=(t // tm,),
        in_specs=[pl.BlockSpec((1, 2, tm), lambda i: (i, 0, 0), memory_space=pltpu.SMEM),
                  pl.BlockSpec((tm, LANES), lambda i: (i, 0)),
                  pl.BlockSpec((tm, D_MODEL), lambda i: (i, 0)),
                  pl.BlockSpec(memory_space=pl.ANY)],
        out_specs=pl.BlockSpec((tm, D_MODEL), lambda i: (i, 0)),
        scratch_shapes=[pltpu.VMEM((2, tm, D_MODEL), F32), pltpu.SemaphoreType.DMA],
        compiler_params=_cparams(("arbitrary",)),
        name="moe_combine",
    )(pos, route, x, y)


def _moe(hp, route, idx, counts, x, wg, wu, wd, tmr=512, tf=1792):
    t = hp.shape[0]
    tm = idx.shape[2]
    n_tiles = 2 * t // tmr + N_EXPERTS
    n_rows = n_tiles * tmr
    cnt = counts[:, 0].astype(jnp.int32)
    padded = (cnt + tmr - 1) // tmr * tmr
    ends = jnp.cumsum(padded)
    offs = ends - padded
    def sorted_rows(e, rank):
        off = jnp.zeros_like(e)
        for k in range(N_EXPERTS):
            off = jnp.where(e == k, offs[k], off)
        return off + rank
    pos = jnp.stack([sorted_rows(idx[:, R_E1, :], idx[:, R_RANK1, :]),
                     sorted_rows(idx[:, R_E2, :], idx[:, R_RANK2, :])], axis=1)
    tile_start = jnp.arange(n_tiles, dtype=jnp.int32) * tmr
    tile_valid = (tile_start < ends[-1]).astype(jnp.int32)
    last_tile = ends[-1] // tmr - 1
    clamped = jnp.minimum(tile_start, last_tile * tmr)
    tile_expert = jnp.sum((clamped[:, None] >= ends[None, :]).astype(jnp.int32), axis=1)
    pad_start = jnp.concatenate([offs + cnt, ends[-1:]])
    xs = _dispatch(pad_start, padded - cnt, pos, hp, n_rows, tm, tmr)
    y = _experts(tile_expert, tile_valid, xs, wg, wu, wd, tmr, tf)
    return _combine(pos, route, x, y, tm)


def _inproj1_kernel(x_ref, w_ref, z_ref, xbc_ref, mq_ref, dt_ref):
    h = x_ref[...]
    c1 = TOK_W
    c2 = c1 + SSM_CONV_DIM
    c3 = c2 + MEM_W
    z_ref[...] = jnp.dot(h, w_ref[:, :c1], preferred_element_type=F32).astype(BF16)
    xbc_ref[...] = jnp.dot(h, w_ref[:, c1:c2], preferred_element_type=F32).astype(BF16)
    mq_ref[...] = jnp.dot(h, w_ref[:, c2:c3], preferred_element_type=F32).astype(BF16)
    dt_ref[...] = jnp.dot(h, w_ref[:, c3:], preferred_element_type=F32)


def _inproj1(h, w, tm=512):
    t = h.shape[0]
    row = lambda n: pl.BlockSpec((tm, n), lambda i: (i, 0))
    return pl.pallas_call(
        _inproj1_kernel,
        out_shape=[jax.ShapeDtypeStruct((t, TOK_W), BF16), jax.ShapeDtypeStruct((t, SSM_CONV_DIM), BF16),
                   jax.ShapeDtypeStruct((t, MEM_W), BF16), jax.ShapeDtypeStruct((t, LANES), F32)],
        grid=(t // tm,),
        in_specs=[row(D_MODEL), pl.BlockSpec(w.shape, lambda i: (0, 0))],
        out_specs=[row(TOK_W), row(SSM_CONV_DIM), row(MEM_W), row(LANES)],
        compiler_params=_cparams(("parallel",)),
        name="inproj1",
    )(h, w)


def _split_dot(a, b, terms, split_rhs):
    rem = b if split_rhs else a
    pieces = []
    for _ in range(terms):
        piece = rem.astype(BF16)
        pieces.append(piece)
        rem = rem - piece.astype(F32)
    if split_rhs:
        n = b.shape[1]
        wide = jnp.dot(a, jnp.concatenate(pieces, axis=1), preferred_element_type=F32)
        out = wide[:, :n]
        for t in range(1, terms):
            out = out + wide[:, t * n:(t + 1) * n]
        return out
    return jnp.dot(jnp.concatenate(pieces, axis=1), jnp.concatenate([b] * terms, axis=0),
                   preferred_element_type=F32)


def _ssd_kernel(z_ref, xbc_ref, dt_ref, cw_ref, cb_ref, dtb_ref, alog_ref, dsk_ref, ng_ref, ex_ref,
                o_ref, xe_ref, st_ref, *, tl):
    j = pl.program_id(1)
    pad = 8

    @pl.when(j == 0)
    def _():
        xe_ref[0:pad, :] = jnp.zeros((pad, SSM_CONV_DIM), F32)
        st_ref[...] = jnp.zeros_like(st_ref)

    x = xbc_ref[0].astype(F32)
    xe_ref[pad:pad + tl, :] = x
    acc = cb_ref[...] + cw_ref[SSM_CONV_K - 1:SSM_CONV_K, :] * x
    for k in range(SSM_CONV_K - 1):
        sh = SSM_CONV_K - 1 - k
        acc = acc + cw_ref[k:k + 1, :] * xe_ref[pad - sh:pad - sh + tl, :]
    xe_ref[0:pad, :] = x[tl - pad:tl, :]
    xc = acc * jax.nn.sigmoid(acc)
    xs = xc[:, :TOK_W]

    dt_in = dt_ref[0] + dtb_ref[...]
    dt = jnp.maximum(dt_in, 0.0) + jnp.log1p(jnp.exp(-jnp.abs(dt_in)))
    a = -jnp.exp(alog_ref[...])
    dta = dt * a
    r = lax.broadcasted_iota(jnp.int32, (tl, tl), 0)
    c = lax.broadcasted_iota(jnp.int32, (tl, tl), 1)
    tril = r >= c
    ltri = jnp.where(tril, 1.0, 0.0).astype(BF16)
    cum = _split_dot(ltri, dta, terms=3, split_rhs=True)
    cum_t = cum.T
    dt_t = dt.T
    lane = lax.broadcasted_iota(jnp.int32, (tl, LANES), 1)

    hg = SSM_HEADS // SSM_GROUPS
    y_pairs = []
    cbs = []
    for g in range(SSM_GROUPS):
        bm = xc[:, TOK_W + g * SSM_STATE:TOK_W + (g + 1) * SSM_STATE].astype(BF16)
        cm = xc[:, TOK_W + (SSM_GROUPS + g) * SSM_STATE:TOK_W + (SSM_GROUPS + g + 1) * SSM_STATE].astype(BF16)
        cbs.append(lax.dot_general(cm, bm, (((1,), (1,)), ((), ())), preferred_element_type=F32))
    for pr in range(SSM_HEADS // 2):
        xp = xs[:, pr * LANES:(pr + 1) * LANES].astype(BF16)
        ys = []
        for hh in (2 * pr, 2 * pr + 1):
            g = hh // hg
            seg = cum[:, hh:hh + 1] - cum_t[hh:hh + 1, :]
            decay = jnp.exp(jnp.where(tril, seg, NEG))
            w = cbs[g] * decay * dt_t[hh:hh + 1, :]
            ys.append(jnp.dot(w.astype(BF16), xp, preferred_element_type=F32))
        y_pairs.append(jnp.where(lane < SSM_HD, ys[0], ys[1]))
    y = jnp.concatenate(y_pairs, axis=1)

    expcum = jnp.exp(cum)
    to_end = jnp.exp(cum[tl - 1:tl, :] - cum) * dt
    stacked = jnp.concatenate([expcum, to_end], axis=0)
    exd = _split_dot(stacked, ex_ref[...], terms=2, split_rhs=False)
    expcum_x = exd[:tl]
    xw = (xs * exd[tl:]).astype(BF16)
    gw = TOK_W // SSM_GROUPS
    y_off = []
    for g in range(SSM_GROUPS):
        cs = slice(g * gw, (g + 1) * gw)
        bm_t = xc[:, TOK_W + g * SSM_STATE:TOK_W + (g + 1) * SSM_STATE].T.astype(BF16)
        cm = xc[:, TOK_W + (SSM_GROUPS + g) * SSM_STATE:TOK_W + (SSM_GROUPS + g + 1) * SSM_STATE].astype(BF16)
        sg = st_ref[:, cs]
        y_off.append(jnp.dot(cm, sg.astype(BF16), preferred_element_type=F32) * expcum_x[:, cs])
        st_ref[:, cs] = sg * expcum_x[tl - 1:tl, cs] + jnp.dot(bm_t, xw[:, cs], preferred_element_type=F32)
    y = y + jnp.concatenate(y_off, axis=1) + dsk_ref[...] * xs
    zf = z_ref[0].astype(F32)
    y = y * (zf * jax.nn.sigmoid(zf))
    outs = []
    for g in range(SSM_GROUPS):
        cs = slice(g * gw, (g + 1) * gw)
        outs.append(_rms(y[:, cs], ng_ref[:, cs]))
    o_ref[0] = jnp.concatenate(outs, axis=1).astype(o_ref.dtype)


def _ssd(z3, xbc3, dt3, cw, cb, dtb, alog, dsk, ng, ex, tl=256):
    b, s, _ = z3.shape
    full = lambda a: pl.BlockSpec(a.shape, lambda bi, j: (0, 0))
    blk = lambda n: pl.BlockSpec((1, tl, n), lambda bi, j: (bi, j, 0))
    return pl.pallas_call(
        functools.partial(_ssd_kernel, tl=tl),
        out_shape=jax.ShapeDtypeStruct((b, s, TOK_W), BF16),
        grid=(b, s // tl),
        in_specs=[blk(TOK_W), blk(SSM_CONV_DIM), blk(LANES), full(cw), full(cb), full(dtb), full(alog),
                  full(dsk), full(ng), full(ex)],
        out_specs=blk(TOK_W),
        scratch_shapes=[pltpu.VMEM((tl + 8, SSM_CONV_DIM), F32), pltpu.VMEM((SSM_STATE, TOK_W), F32)],
        compiler_params=_cparams(("parallel", "arbitrary")),
        name="conv_ssd",
    )(z3, xbc3, dt3, cw, cb, dtb, alog, dsk, ng, ex)


def _pad_lanes(v, n=LANES):
    return jnp.pad(v, [(0, 0)] * (v.ndim - 1) + [(0, n - v.shape[-1])])


def kernel(x, mem, ln1_g, ln2_g, mem_norm_g, w_out, mem_w_kv, mem_qn_g, mem_kn_g, da_w_in, da_qn_g, da_kn_g,
           da_lq1, da_lk1, da_lq2, da_lk2, da_sub_g, ssm_w_in, ssm_conv_w, ssm_conv_b, ssm_dt_bias, ssm_a_log,
           ssm_d, ssm_norm_g, ffn_w_gate, ffn_w_up, ffn_w_down, moe_w_router, moe_w_gate, moe_w_up, moe_w_down):
    b, s, d = x.shape
    t = b * s
    row = lambda v: v.reshape(1, -1).astype(F32)
    xt = x.reshape(t, d)

    mem_qg = jnp.tile(mem_qn_g.astype(F32) * (MEM_HD ** -0.5), (1, MEM_HEADS))
    mem_kg = jnp.tile(mem_kn_g.astype(F32), (1, MEM_HEADS))[:, None, :]
    kv = _memkv(mem, row(mem_norm_g), mem_w_kv.astype(BF16), mem_kg)
    wo = w_out.astype(BF16)

    lambda_init = 0.8 - 0.6 * math.exp(-0.3 * 0)
    qkg = jnp.concatenate([jnp.tile(da_qn_g[0].astype(F32) * (DA_DH ** -0.5 * LOG2E), 2 * DA_HEADS),
                           jnp.tile(da_kn_g[0].astype(F32), 2 * DA_HEADS)]).reshape(1, -1)
    qt, kn, vt, mq0 = _inproj0(xt, row(ln1_g[0]), da_w_in[0].astype(BF16), qkg, b)
    tok = _diff_attention(qt, kn.reshape(b, s, TOK_W), vt, row(da_lq1[0]), row(da_lk1[0]), row(da_lq2[0]),
                          row(da_lk2[0]), row(da_sub_g[0]), lambda_init)
    x1, h2 = _mixout(tok.reshape(t, TOK_W), mq0, 0, kv[0], mem_qg[0:1], wo[0], xt, row(ln2_g[0]))
    x2, h3 = _ffn(h2, x1, ffn_w_gate[0].astype(BF16), ffn_w_up[0].astype(BF16), ffn_w_down[0].astype(BF16),
                  row(ln1_g[1]))

    w_in = ssm_w_in[0]
    o2 = TOK_W + SSM_CONV_DIM
    o3 = o2 + SSM_HEADS
    w1 = jnp.concatenate([w_in[:, :o2], w_in[:, o3:], _pad_lanes(w_in[:, o2:o3])], axis=1).astype(BF16)
    z, xbc, mq, dt = _inproj1(h3, w1)
    expand = jnp.repeat(jnp.eye(SSM_HEADS, dtype=F32), SSM_HD, axis=1)
    expand = jnp.pad(expand, ((0, LANES - SSM_HEADS), (0, 0))).astype(BF16)
    dsk = jnp.repeat(ssm_d[0].astype(F32), SSM_HD).reshape(1, -1)
    tok1 = _ssd(z.reshape(b, s, -1), xbc.reshape(b, s, -1), dt.reshape(b, s, -1),
                ssm_conv_w[0].astype(F32), row(ssm_conv_b[0]), _pad_lanes(row(ssm_dt_bias[0])),
                _pad_lanes(row(ssm_a_log[0])), dsk, row(ssm_norm_g[0]), expand)
    x3, hp, route, idx, counts = _mixout(tok1.reshape(t, TOK_W), mq, 0, kv[1], mem_qg[1:2], wo[1], x2,
                                         row(ln2_g[1]), wr=_pad_lanes(moe_w_router[0].astype(F32)))
    x4 = _moe(hp, route, idx, counts, x3, moe_w_gate[0].astype(BF16), moe_w_up[0].astype(BF16),
              moe_w_down[0].astype(BF16))
    return x4.reshape(b, s, d)
```

```python
import functools
import math

import jax
import jax.numpy as jnp
from jax import lax
from jax.experimental import pallas as pl
from jax.experimental.pallas import tpu as pltpu

F32 = jnp.float32
BF16 = jnp.bfloat16

D_MODEL = 1024
CHUNK = 64
MEM_LEN = 256
MEM_W = 256
MEM_HEADS = 4
MEM_HD = 64
TOK_W = 768
DA_DH = 64
DA_HEADS = 6
SSM_HD = 64
SSM_HEADS = 12
SSM_GROUPS = 2
SSM_STATE = 128
SSM_CONV_K = 4
SSM_CONV_DIM = 1280
N_EXPERTS = 8
EPS = 1e-6
LANES = 128
NEG = -1e30
SUBLANES = 8
ONES_ROWS = SUBLANES
LOG2E = math.log2(math.e)
CAST_CHUNKS = 64
ISSUE_UNROLL = True
VMEM_LIMIT = 56 * 1024 * 1024


def _cparams(sem):
    return pltpu.CompilerParams(dimension_semantics=sem, vmem_limit_bytes=VMEM_LIMIT)


def _rms(xf, g):
    ms = jnp.mean(xf * xf, axis=-1, keepdims=True)
    return xf * lax.rsqrt(ms + EPS) * g


def _seg_mean_matrix(n, seg_shift):
    r = lax.broadcasted_iota(jnp.int32, (n, n), 0) >> seg_shift
    c = lax.broadcasted_iota(jnp.int32, (n, n), 1) >> seg_shift
    return jnp.where(r == c, 1.0 / (1 << seg_shift), 0.0).astype(BF16)


def _seg_mean_sq(y, seg_shift=6):
    bd = _seg_mean_matrix(256, seg_shift)
    sq = (y * y).astype(BF16)
    parts = [jnp.dot(sq[:, c:c + 256], bd, preferred_element_type=F32)
             for c in range(0, y.shape[1], 256)]
    return parts[0] if len(parts) == 1 else jnp.concatenate(parts, axis=1)


def _inproj0_kernel(x_ref, g_ref, w_ref, qkg_ref, qt_ref, k_ref, vt_ref, mq_ref):
    h = _rms(x_ref[...], g_ref[...]).astype(BF16)
    u = jnp.dot(h, w_ref[...], preferred_element_type=F32)
    tm = u.shape[0]
    hd = 2 * DA_DH
    nqk = 2 * TOK_W
    qk = u[:, :nqk]
    qkn = qk * lax.rsqrt(_seg_mean_sq(qk) + EPS) * qkg_ref[...]
    qt_ref[0] = qkn[:, :TOK_W].T.reshape(DA_HEADS, hd, tm).astype(BF16)
    k_ref[...] = qkn[:, TOK_W:].astype(BF16)
    vt_ref[0, :, 0, :hd, :] = u[:, nqk:nqk + TOK_W].T.reshape(DA_HEADS, hd, tm).astype(BF16)
    vt_ref[0, :, 0, hd:, :] = jnp.ones((DA_HEADS, ONES_ROWS, tm), BF16)
    mq_ref[...] = u[:, nqk + TOK_W:].astype(BF16)


def _inproj0(x, g, w, qkg, b, tm=512):
    t = x.shape[0]
    n = w.shape[1]
    s = t // b
    nt = s // tm
    hd = 2 * DA_DH
    return pl.pallas_call(
        _inproj0_kernel,
        out_shape=[jax.ShapeDtypeStruct((b, DA_HEADS, hd, s), BF16),
                   jax.ShapeDtypeStruct((t, TOK_W), BF16),
                   jax.ShapeDtypeStruct((b, DA_HEADS, nt, hd + ONES_ROWS, tm), BF16),
                   jax.ShapeDtypeStruct((t, MEM_W), BF16)],
        grid=(t // tm,),
        in_specs=[pl.BlockSpec((tm, D_MODEL), lambda i: (i, 0)),
                  pl.BlockSpec((1, D_MODEL), lambda i: (0, 0)),
                  pl.BlockSpec((D_MODEL, n), lambda i: (0, 0)),
                  pl.BlockSpec((1, 2 * TOK_W), lambda i: (0, 0))],
        out_specs=[pl.BlockSpec((1, DA_HEADS, hd, tm), lambda i: (i // nt, 0, 0, i % nt)),
                   pl.BlockSpec((tm, TOK_W), lambda i: (i, 0)),
                   pl.BlockSpec((1, DA_HEADS, 1, hd + ONES_ROWS, tm), lambda i: (i // nt, 0, i % nt, 0, 0)),
                   pl.BlockSpec((tm, MEM_W), lambda i: (i, 0))],
        compiler_params=_cparams(("parallel",)),
        name="inproj0",
    )(x, g, w, qkg)


def _attn_kernel(*refs, tq, tk, nh, n_riders, lambda_init):
    lq1_ref, lk1_ref, lq2_ref, lk2_ref, subg_ref, qt_ref, k_ref, vt_ref = refs[:8]
    rider_in = refs[8:8 + n_riders]
    o_ref = refs[8 + n_riders]
    rider_out = refs[9 + n_riders:9 + 2 * n_riders]
    acc_ref, m_ref, alpha_ref, s_ref, p_ref = refs[9 + 2 * n_riders:]
    for src, dst in zip(rider_in, rider_out):
        dst[...] = src[...].astype(dst.dtype)
    qi = pl.program_id(2)
    hd = 2 * DA_DH
    qs = []
    for hs in range(nh):
        qt = qt_ref[0, hs]
        row = lax.broadcasted_iota(jnp.int32, qt.shape, 0)
        zero = jnp.zeros_like(qt)
        qs += [jnp.where(row < DA_DH, qt, zero), jnp.where(row >= DA_DH, qt, zero)]
    acc_ref[...] = jnp.zeros_like(acc_ref)
    m_ref[...] = jnp.full_like(m_ref, NEG)
    p_ref[1] = jnp.zeros(p_ref.shape[1:], p_ref.dtype)
    alpha_ref[1] = jnp.ones(alpha_ref.shape[1:], alpha_ref.dtype)

    def scores(ki, slot, c):
        start = pl.multiple_of(ki * tk, tk)
        hs = c // 2
        k = k_ref[0, pl.ds(start, tk), hs * hd:(hs + 1) * hd]
        s_ref[slot, c] = jnp.dot(k, qs[c], preferred_element_type=F32)

    def softmax(slot, masked, c):
        s = s_ref[slot, c]
        if masked:
            kc = lax.broadcasted_iota(jnp.int32, (tk, tq), 0) >> 6
            qc = lax.broadcasted_iota(jnp.int32, (tk, tq), 1) >> 6
            s = jnp.where(kc <= qc, s, NEG)
        m_old = m_ref[c]
        m_new = jnp.maximum(m_old, jnp.max(s, axis=0, keepdims=True))
        alpha_ref[slot, c] = jnp.exp2(m_old - m_new)
        p_ref[slot, c] = jnp.exp2(s - m_new).astype(BF16)
        m_ref[c] = m_new

    def values(ki, slot, c):
        vt = vt_ref[0, c // 2, ki]
        acc_ref[c] = alpha_ref[slot, c] * acc_ref[c] + jnp.dot(vt, p_ref[slot, c], preferred_element_type=F32)

    groups = [(2 * hs, 2 * hs + 1) for hs in range(nh)]
    odd = (qi & 1) == 1

    def stage(fn, g, *args):
        for c in g:
            fn(*args, c)

    @pl.when(odd)
    def _():
        for g in groups:
            stage(scores, g, 0, 1)
            stage(scores, g, 1, 0)
            stage(softmax, g, 1, False)

    @pl.when(jnp.logical_not(odd))
    def _():
        for g in groups:
            stage(scores, g, 0, 0)

    def body(j, carry):
        ki = (qi & 1) + 2 * j
        for g in groups:
            stage(scores, g, ki + 1, 1)
            stage(softmax, g, 0, False)
            stage(values, g, jnp.maximum(ki - 1, 0), 1)
            stage(scores, g, ki + 2, 0)
            stage(softmax, g, 1, False)
            stage(values, g, ki, 0)
        return carry

    lax.fori_loop(0, qi >> 1, body, 0)
    for g in groups:
        stage(softmax, g, 0, True)
        stage(values, g, jnp.maximum(qi - 1, 0), 1)
        stage(values, g, qi, 0)

    lam = (jnp.exp(jnp.sum(lq1_ref[...] * lk1_ref[...])) - jnp.exp(jnp.sum(lq2_ref[...] * lk2_ref[...]))
           + lambda_init)
    for hs in range(nh):
        o1 = acc_ref[2 * hs, :hd, :] / acc_ref[2 * hs, hd:hd + 1, :]
        o2 = acc_ref[2 * hs + 1, :hd, :] / acc_ref[2 * hs + 1, hd:hd + 1, :]
        ot = o1 - lam * o2
        ms = jnp.mean(ot * ot, axis=0, keepdims=True)
        ot = ot * lax.rsqrt(ms + EPS)
        o_ref[0, :, hs * hd:(hs + 1) * hd] = (ot.T * (subg_ref[...] * (1.0 - lambda_init))).astype(o_ref.dtype)


def _diff_attention(qt, k3, vt, lq1, lk1, lq2, lk2, subg, lambda_init, riders=(), nh=2):
    b, s, _ = k3.shape
    tk = vt.shape[-1]
    tq = tk
    nq = s // tq
    ng = DA_HEADS // nh
    hd = 2 * DA_DH
    hde = hd + ONES_ROWS
    nc = 2 * nh
    vec = lambda n: pl.BlockSpec((1, n), lambda bi, h, qi: (0, 0))

    def rider_spec(a):
        assert a.shape[0] <= b * ng * nq
        last = a.shape[0] - 1
        return pl.BlockSpec((1,) + a.shape[1:], lambda bi, h, qi: (jnp.minimum((bi * ng + h) * nq + qi, last), 0, 0))

    kern = functools.partial(_attn_kernel, tq=tq, tk=tk, nh=nh, n_riders=len(riders), lambda_init=lambda_init)
    res = pl.pallas_call(
        kern,
        out_shape=[jax.ShapeDtypeStruct((b, s, TOK_W), BF16)] + [jax.ShapeDtypeStruct(a.shape, BF16) for a in riders],
        grid=(b, ng, nq),
        in_specs=[vec(DA_DH), vec(DA_DH), vec(DA_DH), vec(DA_DH), vec(hd),
                  pl.BlockSpec((1, nh, hd, tq), lambda bi, h, qi: (bi, h, 0, qi)),
                  pl.BlockSpec((1, s, nh * hd), lambda bi, h, qi: (bi, 0, h)),
                  pl.BlockSpec((1, nh, s // tk, hde, tk), lambda bi, h, qi: (bi, h, 0, 0, 0))]
                 + [rider_spec(a) for a in riders],
        out_specs=[pl.BlockSpec((1, tq, nh * hd), lambda bi, h, qi: (bi, qi, h))] + [rider_spec(a) for a in riders],
        scratch_shapes=[pltpu.VMEM((nc, hde, tq), F32), pltpu.VMEM((nc, 1, tq), F32),
                        pltpu.VMEM((2, nc, 1, tq), F32), pltpu.VMEM((2, nc, tk, tq), F32),
                        pltpu.VMEM((2, nc, tk, tq), BF16)],
        compiler_params=_cparams(("arbitrary", "arbitrary", "arbitrary")),
        name="diff_attn",
    )(lq1, lk1, lq2, lk2, subg, qt, k3, vt, *riders)
    return res[0], res[1:]


def _memkv_kernel(mem_ref, g_ref, wkv_ref, kng_ref, kv_ref):
    mn = _rms(mem_ref[0], g_ref[...]).astype(BF16)
    kv = jnp.dot(mn, wkv_ref[0], preferred_element_type=F32)
    k = kv[:, :MEM_W]
    kn = k * lax.rsqrt(_seg_mean_sq(k) + EPS) * kng_ref[0]
    kv_ref[0, 0, :, :MEM_W] = kn.astype(BF16)
    kv_ref[0, 0, :, MEM_W:] = kv[:, MEM_W:].astype(BF16)


def _memkv(mem, g, wkv, kng):
    depth = wkv.shape[0]
    b = mem.shape[0]
    return pl.pallas_call(
        _memkv_kernel,
        out_shape=jax.ShapeDtypeStruct((depth, b, MEM_LEN, 2 * MEM_W), BF16),
        grid=(depth, b),
        in_specs=[pl.BlockSpec((1, MEM_LEN, D_MODEL), lambda d, bi: (bi, 0, 0)),
                  pl.BlockSpec((1, D_MODEL), lambda d, bi: (0, 0)),
                  pl.BlockSpec((1, D_MODEL, 2 * MEM_W), lambda d, bi: (d, 0, 0)),
                  pl.BlockSpec((1, 1, MEM_W), lambda d, bi: (d, 0, 0))],
        out_specs=pl.BlockSpec((1, 1, MEM_LEN, 2 * MEM_W), lambda d, bi: (d, bi, 0, 0)),
        compiler_params=_cparams(("parallel", "parallel")),
        name="memkv",
    )(mem, g, wkv, kng)


def _mem_attention(mq, qg, kv):
    qn = (mq * lax.rsqrt(_seg_mean_sq(mq) + EPS) * qg).astype(BF16)
    k = kv[:, :MEM_W]
    v = kv[:, MEM_W:]
    lane = lax.broadcasted_iota(jnp.int32, qn.shape, 1) >> 6
    out = jnp.zeros(qn.shape, F32)
    for h in range(MEM_HEADS):
        sel = lane == h
        qh = jnp.where(sel, qn, jnp.zeros_like(qn))
        s = lax.dot_general(qh, k, (((1,), (1,)), ((), ())), preferred_element_type=F32)
        p = jnp.exp(s - jnp.max(s, axis=1, keepdims=True))
        l = jnp.sum(p, axis=1, keepdims=True)
        o = jnp.dot(p.astype(BF16), v, preferred_element_type=F32)
        out = jnp.where(sel, o / l, out)
    return out


R_E1, R_E2, R_G1, R_G2, R_RANK1, R_RANK2 = range(6)


def _top2_route(logits, count_ref):
    assert N_EXPERTS == SUBLANES
    tm = logits.shape[0]
    lt = logits.T[:SUBLANES, :]
    row = lax.broadcasted_iota(jnp.int32, lt.shape, 0)
    m1 = jnp.max(lt, axis=0, keepdims=True)
    i1 = jnp.min(jnp.where(lt == m1, row, N_EXPERTS), axis=0, keepdims=True)
    lt2 = jnp.where(row == i1, NEG, lt)
    m2 = jnp.max(lt2, axis=0, keepdims=True)
    i2 = jnp.min(jnp.where(lt2 == m2, row, N_EXPERTS), axis=0, keepdims=True)
    g1 = 1.0 / (1.0 + jnp.exp(m2 - m1))
    g2 = 1.0 - g1
    chosen = jnp.logical_or(row == i1, row == i2)
    onehot = jnp.where(chosen, 1.0, 0.0)
    r = lax.broadcasted_iota(jnp.int32, (tm, tm), 0)
    c = lax.broadcasted_iota(jnp.int32, (tm, tm), 1)
    earlier = jnp.where(r < c, 1.0, 0.0).astype(BF16)
    onehot16 = jnp.concatenate([onehot, jnp.zeros_like(onehot)], axis=0).astype(BF16)
    prefix = jnp.dot(onehot16, earlier, preferred_element_type=F32)[:SUBLANES] + count_ref[:, :1]
    count_ref[...] += jnp.sum(onehot, axis=1, keepdims=True)
    rank1 = jnp.sum(jnp.where(row == i1, prefix, 0.0), axis=0, keepdims=True)
    rank2 = jnp.sum(jnp.where(row == i2, prefix, 0.0), axis=0, keepdims=True)
    rec = jnp.zeros(lt.shape, F32)
    for idx, val in ((R_E1, i1.astype(F32)), (R_E2, i2.astype(F32)), (R_G1, g1), (R_G2, g2),
                     (R_RANK1, rank1), (R_RANK2, rank2)):
        rec = jnp.where(row == idx, val, rec)
    return rec


def _mixout_kernel(*refs, with_router):
    if with_router:
        (tok_ref, mq_ref, kv_ref, qg_ref, wo_ref, x_ref, g2_ref, wr_ref,
         x1_ref, hp_ref, route_ref, idx_ref, count_ref) = refs
    else:
        tok_ref, mq_ref, kv_ref, qg_ref, wo_ref, x_ref, g2_ref, x1_ref, h2_ref = refs
    mo = _mem_attention(mq_ref[...].astype(F32), qg_ref[...], kv_ref[0])
    y = jnp.dot(tok_ref[...], wo_ref[:TOK_W, :], preferred_element_type=F32)
    y = y + jnp.dot(mo.astype(BF16), wo_ref[TOK_W:, :], preferred_element_type=F32)
    x1 = x_ref[...] + y
    x1_ref[...] = x1
    h2 = _rms(x1, g2_ref[...])
    if with_router:
        @pl.when(pl.program_id(0) == 0)
        def _():
            count_ref[...] = jnp.zeros_like(count_ref)

        hp_ref[...] = h2
        wr = wr_ref[...]
        w_hi = wr.astype(BF16)
        w_lo = (wr - w_hi.astype(F32)).astype(BF16)
        h_hi = h2.astype(BF16)
        h_lo = (h2 - h_hi.astype(F32)).astype(BF16)
        both = jnp.dot(h_hi, jnp.concatenate([w_hi, w_lo], axis=1), preferred_element_type=F32)
        logits = both[:, :LANES] + both[:, LANES:] + jnp.dot(h_lo, w_hi, preferred_element_type=F32)
        rec = _top2_route(logits, count_ref)
        idx_ref[0] = rec.astype(jnp.int32)
        route_ref[...] = jnp.concatenate([rec, jnp.zeros((LANES - SUBLANES, rec.shape[1]), F32)], axis=0).T
    else:
        h2_ref[...] = h2.astype(BF16)


def _mixout(tok, mq_src, mq_block, kv, qg, wo, x, g2, wr=None, tm=512):
    t = x.shape[0]
    tiles_per_batch = t // kv.shape[0] // tm
    with_router = wr is not None
    in_specs = [pl.BlockSpec((tm, TOK_W), lambda i: (i, 0)),
                pl.BlockSpec((tm, MEM_W), lambda i: (i, mq_block)),
                pl.BlockSpec((1, MEM_LEN, 2 * MEM_W), lambda i: (i // tiles_per_batch, 0, 0)),
                pl.BlockSpec((1, MEM_W), lambda i: (0, 0)),
                pl.BlockSpec((D_MODEL, D_MODEL), lambda i: (0, 0)),
                pl.BlockSpec((tm, D_MODEL), lambda i: (i, 0)),
                pl.BlockSpec((1, D_MODEL), lambda i: (0, 0))]
    args = [tok, mq_src, kv, qg, wo, x, g2]
    row_spec = lambda n: pl.BlockSpec((tm, n), lambda i: (i, 0))
    if with_router:
        in_specs.append(pl.BlockSpec((D_MODEL, LANES), lambda i: (0, 0)))
        args.append(wr)
        out_shape = [jax.ShapeDtypeStruct((t, D_MODEL), F32), jax.ShapeDtypeStruct((t, D_MODEL), F32),
                     jax.ShapeDtypeStruct((t, LANES), F32), jax.ShapeDtypeStruct((t // tm, SUBLANES, tm), jnp.int32),
                     jax.ShapeDtypeStruct((SUBLANES, LANES), F32)]
        out_specs = [row_spec(D_MODEL), row_spec(D_MODEL), row_spec(LANES),
                     pl.BlockSpec((1, SUBLANES, tm), lambda i: (i, 0, 0)),
                     pl.BlockSpec((SUBLANES, LANES), lambda i: (0, 0))]
    else:
        out_shape = [jax.ShapeDtypeStruct((t, D_MODEL), F32), jax.ShapeDtypeStruct((t, D_MODEL), BF16)]
        out_specs = [row_spec(D_MODEL), row_spec(D_MODEL)]
    return pl.pallas_call(
        functools.partial(_mixout_kernel, with_router=with_router),
        out_shape=out_shape,
        grid=(t // tm,),
        in_specs=in_specs,
        out_specs=out_specs,
        compiler_params=_cparams(("arbitrary",) if with_router else ("parallel",)),
        name="mixout_router" if with_router else "mixout",
    )(*args)


def _ffn_kernel(h_ref, x_ref, wg_ref, wu_ref, wd_ref, gn_ref, xo_ref, ho_ref, acc_ref):
    f = pl.program_id(1)
    h = h_ref[...]
    a = jnp.dot(h, wg_ref[...], preferred_element_type=F32)
    u = jnp.dot(h, wu_ref[...], preferred_element_type=F32)
    act = a * jax.nn.sigmoid(a) * u
    contrib = jnp.dot(act.astype(BF16), wd_ref[...], preferred_element_type=F32)
    acc_ref[...] = jnp.where(f == 0, x_ref[...], acc_ref[...]) + contrib

    @pl.when(f == pl.num_programs(1) - 1)
    def _():
        xn = acc_ref[...]
        xo_ref[...] = xn
        ho_ref[...] = _rms(xn, gn_ref[...]).astype(BF16)


def _ffn(h, x, wg, wu, wd, gn, tm=512, tf=1408):
    t = x.shape[0]
    fdim = wg.shape[1]
    row = pl.BlockSpec((tm, D_MODEL), lambda i, f: (i, 0))
    return pl.pallas_call(
        _ffn_kernel,
        out_shape=[jax.ShapeDtypeStruct((t, D_MODEL), F32), jax.ShapeDtypeStruct((t, D_MODEL), BF16)],
        grid=(t // tm, fdim // tf),
        in_specs=[row, row,
                  pl.BlockSpec((D_MODEL, tf), lambda i, f: (0, f)),
                  pl.BlockSpec((D_MODEL, tf), lambda i, f: (0, f)),
                  pl.BlockSpec((tf, D_MODEL), lambda i, f: (f, 0)),
                  pl.BlockSpec((1, D_MODEL), lambda i, f: (0, 0))],
        out_specs=[row, row],
        scratch_shapes=[pltpu.VMEM((tm, D_MODEL), F32)],
        compiler_params=_cparams(("parallel", "arbitrary")),
        name="ffn",
    )(h, x, wg, wu, wd, gn)


def _row_copy(src_ref, src_row, dst_ref, dst_row, sem):
    return pltpu.make_async_copy(src_ref.at[pl.ds(src_row, 1), :], dst_ref.at[pl.ds(dst_row, 1), :], sem)


def _dispatch_kernel(pad_start_ref, pad_len_ref, pos_ref, hp_ref, xs_ref, zeros_ref, sem, zsem, *, tm, tmr):
    nbits = tmr.bit_length() - 1

    @pl.when(pl.program_id(0) == 0)
    def _():
        zeros_ref[...] = jnp.zeros_like(zeros_ref)
        for e in range(N_EXPERTS):
            start = pad_start_ref[e]
            length = pad_len_ref[e]
            singles = length & (SUBLANES - 1)
            for j in range(SUBLANES - 1):
                @pl.when(j < singles)
                def _(j=j, start=start):
                    cp = _row_copy(zeros_ref, 0, xs_ref, start + j, zsem)
                    cp.start()
                    cp.wait()

            done = start + singles
            for b in range(SUBLANES.bit_length() - 1, nbits):
                n = 1 << b
                bit = (length >> b) & 1

                @pl.when(bit == 1)
                def _(n=n, done=done):
                    cp = pltpu.make_async_copy(zeros_ref.at[pl.ds(0, n), :],
                                               xs_ref.at[pl.ds(pl.multiple_of(done, SUBLANES), n), :], zsem)
                    cp.start()
                    cp.wait()

                done = done + bit * n

        n_tiles = xs_ref.shape[0] // tmr
        half = tmr // 2
        for j in range(n_tiles - N_EXPERTS, n_tiles):
            @pl.when(j * tmr >= pad_start_ref[N_EXPERTS])
            def _(j=j):
                for c in range(2):
                    cp = pltpu.make_async_copy(zeros_ref, xs_ref.at[pl.ds(j * tmr + c * half, half), :], zsem)
                    cp.start()
                    cp.wait()

    def issue(r, carry):
        _row_copy(hp_ref, r, xs_ref, pos_ref[0, 0, r], sem).start(priority=0)
        _row_copy(hp_ref, r, xs_ref, pos_ref[0, 1, r], sem).start(priority=1)
        return carry

    lax.fori_loop(0, tm, issue, 0, unroll=ISSUE_UNROLL)
    for _ in range(2):
        pltpu.make_async_copy(hp_ref, xs_ref.at[pl.ds(0, tm), :], sem).wait()


def _dispatch(pad_start, pad_len, pos, hp, n_rows, tm, tmr):
    t, w = hp.shape
    return pl.pallas_call(
        functools.partial(_dispatch_kernel, tm=tm, tmr=tmr),
        out_shape=jax.ShapeDtypeStruct((n_rows, w), hp.dtype),
        grid_spec=pltpu.PrefetchScalarGridSpec(
            num_scalar_prefetch=2,
            grid=(t // tm,),
            in_specs=[pl.BlockSpec((1, 2, tm), lambda i, ps, pn: (i, 0, 0), memory_space=pltpu.SMEM),
                      pl.BlockSpec((tm, w), lambda i, ps, pn: (i, 0))],
            out_specs=pl.BlockSpec(memory_space=pl.ANY),
            scratch_shapes=[pltpu.VMEM((tmr // 2, w), hp.dtype), pltpu.SemaphoreType.DMA,
                            pltpu.SemaphoreType.DMA]),
        compiler_params=_cparams(("arbitrary",)),
        name="moe_dispatch",
    )(pad_start, pad_len, pos, hp)


def _experts_kernel(te_ref, tv_ref, xs_ref, wg_ref, wu_ref, wd_ref, y_ref):
    i = pl.program_id(0)
    f = pl.program_id(1)

    @pl.when(tv_ref[i] == 1)
    def _():
        h = xs_ref[...].astype(BF16)
        a = jnp.dot(h, wg_ref[0], preferred_element_type=F32)
        u = jnp.dot(h, wu_ref[0], preferred_element_type=F32)
        act = a * jax.nn.sigmoid(a) * u
        contrib = jnp.dot(act.astype(BF16), wd_ref[0], preferred_element_type=F32)
        y_ref[...] = jnp.where(f == 0, 0.0, y_ref[...]) + contrib

    @pl.when(jnp.logical_and(tv_ref[i] == 0, f == 0))
    def _():
        y_ref[...] = jnp.zeros_like(y_ref)


def _experts(tile_expert, tile_valid, xs, wg, wu, wd, tmr, tf):
    n_rows, w = xs.shape
    fdim = wg.shape[2]
    nf = fdim // tf
    fidx = lambda i, f, te, tv: jnp.where(tv[i] == 1, f, nf - 1)
    return pl.pallas_call(
        _experts_kernel,
        out_shape=jax.ShapeDtypeStruct((n_rows, D_MODEL), F32),
        grid_spec=pltpu.PrefetchScalarGridSpec(
            num_scalar_prefetch=2,
            grid=(n_rows // tmr, nf),
            in_specs=[pl.BlockSpec((tmr, w), lambda i, f, te, tv: (jnp.where(tv[i] == 1, i, 0), 0)),
                      pl.BlockSpec((1, D_MODEL, tf), lambda i, f, te, tv: (te[i], 0, fidx(i, f, te, tv))),
                      pl.BlockSpec((1, D_MODEL, tf), lambda i, f, te, tv: (te[i], 0, fidx(i, f, te, tv))),
                      pl.BlockSpec((1, tf, D_MODEL), lambda i, f, te, tv: (te[i], fidx(i, f, te, tv), 0))],
            out_specs=pl.BlockSpec((tmr, D_MODEL), lambda i, f, te, tv: (i, 0))),
        compiler_params=_cparams(("parallel", "arbitrary")),
        name="moe_experts",
    )(tile_expert, tile_valid, xs, wg, wu, wd)


def _combine_kernel(pos_ref, route_ref, x_ref, y_ref, o_ref, buf_ref, sem, *, tm):
    def issue(r, carry):
        _row_copy(y_ref, pos_ref[0, 0, r], buf_ref.at[0], r, sem).start(priority=0)
        _row_copy(y_ref, pos_ref[0, 1, r], buf_ref.at[1], r, sem).start(priority=1)
        return carry

    lax.fori_loop(0, tm, issue, 0, unroll=ISSUE_UNROLL)
    for k in range(2):
        pltpu.make_async_copy(y_ref.at[pl.ds(0, tm), :], buf_ref.at[k], sem).wait()
    route = route_ref[...]
    g1 = route[:, R_G1:R_G1 + 1]
    g2 = route[:, R_G2:R_G2 + 1]
    o_ref[...] = x_ref[...] + (g1 * buf_ref[0] + g2 * buf_ref[1])


def _combine(pos, route, x, y, tm):
    t = x.shape[0]
    return pl.pallas_call(
        functools.partial(_combine_kernel, tm=tm),
        out_shape=jax.ShapeDtypeStruct((t, D_MODEL), F32),
        grid=(t // tm,),
        in_specs=[pl.BlockSpec((1, 2, tm), lambda i: (i, 0, 0), memory_space=pltpu.SMEM),
                  pl.BlockSpec((tm, LANES), lambda i: (i, 0)),
                  pl.BlockSpec((tm, D_MODEL), lambda i: (i, 0)),
                  pl.BlockSpec(memory_space=pl.ANY)],
        out_specs=pl.BlockSpec((tm, D_MODEL), lambda i: (i, 0)),
        scratch_shapes=[pltpu.VMEM((2, tm, D_MODEL), F32), pltpu.SemaphoreType.DMA],
        compiler_params=_cparams(("arbitrary",)),
        name="moe_combine",
    )(pos, route, x, y)


def _moe(hp, route, idx, counts, x, wg, wu, wd, tmr=512, tf=1792):
    t = hp.shape[0]
    tm = idx.shape[2]
    n_tiles = 2 * t // tmr + N_EXPERTS
    n_rows = n_tiles * tmr
    cnt = counts[:, 0].astype(jnp.int32)
    padded = (cnt + tmr - 1) // tmr * tmr
    ends = jnp.cumsum(padded)
    offs = ends - padded
    def sorted_rows(e, rank):
        off = jnp.zeros_like(e)
        for k in range(N_EXPERTS):
            off = jnp.where(e == k, offs[k], off)
        return off + rank
    pos = jnp.stack([sorted_rows(idx[:, R_E1, :], idx[:, R_RANK1, :]),
                     sorted_rows(idx[:, R_E2, :], idx[:, R_RANK2, :])], axis=1)
    tile_start = jnp.arange(n_tiles, dtype=jnp.int32) * tmr
    tile_valid = (tile_start < ends[-1]).astype(jnp.int32)
    last_tile = ends[-1] // tmr - 1
    clamped = jnp.minimum(tile_start, last_tile * tmr)
    tile_expert = jnp.sum((clamped[:, None] >= ends[None, :]).astype(jnp.int32), axis=1)
    pad_start = jnp.concatenate([offs + cnt, ends[-1:]])
    xs = _dispatch(pad_start, padded - cnt, pos, hp, n_rows, tm, tmr)
    y = _experts(tile_expert, tile_valid, xs, wg, wu, wd, tmr, tf)
    return _combine(pos, route, x, y, tm)


def _inproj1_kernel(x_ref, w_ref, z_ref, xbc_ref, mq_ref, dt_ref):
    h = x_ref[...]
    c1 = TOK_W
    c2 = c1 + SSM_CONV_DIM
    c3 = c2 + MEM_W
    z_ref[...] = jnp.dot(h, w_ref[:, :c1], preferred_element_type=F32).astype(BF16)
    xbc_ref[...] = jnp.dot(h, w_ref[:, c1:c2], preferred_element_type=F32).astype(BF16)
    mq_ref[...] = jnp.dot(h, w_ref[:, c2:c3], preferred_element_type=F32).astype(BF16)
    dt_ref[...] = jnp.dot(h, w_ref[:, c3:], preferred_element_type=F32)


def _inproj1(h, w, tm=512):
    t = h.shape[0]
    row = lambda n: pl.BlockSpec((tm, n), lambda i: (i, 0))
    return pl.pallas_call(
        _inproj1_kernel,
        out_shape=[jax.ShapeDtypeStruct((t, TOK_W), BF16), jax.ShapeDtypeStruct((t, SSM_CONV_DIM), BF16),
                   jax.ShapeDtypeStruct((t, MEM_W), BF16), jax.ShapeDtypeStruct((t, LANES), F32)],
        grid=(t // tm,),
        in_specs=[row(D_MODEL), pl.BlockSpec(w.shape, lambda i: (0, 0))],
        out_specs=[row(TOK_W), row(SSM_CONV_DIM), row(MEM_W), row(LANES)],
        compiler_params=_cparams(("parallel",)),
        name="inproj1",
    )(h, w)


def _split_dot(a, b, terms, split_rhs):
    rem = b if split_rhs else a
    pieces = []
    for _ in range(terms):
        piece = rem.astype(BF16)
        pieces.append(piece)
        rem = rem - piece.astype(F32)
    if split_rhs:
        n = b.shape[1]
        wide = jnp.dot(a, jnp.concatenate(pieces, axis=1), preferred_element_type=F32)
        out = wide[:, :n]
        for t in range(1, terms):
            out = out + wide[:, t * n:(t + 1) * n]
        return out
    return jnp.dot(jnp.concatenate(pieces, axis=1), jnp.concatenate([b] * terms, axis=0),
                   preferred_element_type=F32)


def _ssd_kernel(z_ref, xbc_ref, dt_ref, cw_ref, cb_ref, dtb_ref, alog_ref, dsk_ref, ng_ref, ex_ref,
                o_ref, xe_ref, st_ref, *, tl):
    j = pl.program_id(1)
    pad = 8

    @pl.when(j == 0)
    def _():
        xe_ref[0:pad, :] = jnp.zeros((pad, SSM_CONV_DIM), F32)
        st_ref[...] = jnp.zeros_like(st_ref)

    x = xbc_ref[0].astype(F32)
    xe_ref[pad:pad + tl, :] = x
    acc = cb_ref[...] + cw_ref[SSM_CONV_K - 1:SSM_CONV_K, :] * x
    for k in range(SSM_CONV_K - 1):
        sh = SSM_CONV_K - 1 - k
        acc = acc + cw_ref[k:k + 1, :] * xe_ref[pad - sh:pad - sh + tl, :]
    xe_ref[0:pad, :] = x[tl - pad:tl, :]
    xc = acc * jax.nn.sigmoid(acc)
    xs = xc[:, :TOK_W]

    dt_in = dt_ref[0] + dtb_ref[...]
    dt = jnp.maximum(dt_in, 0.0) + jnp.log1p(jnp.exp(-jnp.abs(dt_in)))
    a = -jnp.exp(alog_ref[...])
    dta = dt * a
    r = lax.broadcasted_iota(jnp.int32, (tl, tl), 0)
    c = lax.broadcasted_iota(jnp.int32, (tl, tl), 1)
    tril = r >= c
    ltri = jnp.where(tril, 1.0, 0.0).astype(BF16)
    cum = _split_dot(ltri, dta, terms=3, split_rhs=True)
    cum_t = cum.T
    dt_t = dt.T
    lane = lax.broadcasted_iota(jnp.int32, (tl, LANES), 1)

    hg = SSM_HEADS // SSM_GROUPS
    y_pairs = []
    cbs = []
    for g in range(SSM_GROUPS):
        bm = xc[:, TOK_W + g * SSM_STATE:TOK_W + (g + 1) * SSM_STATE].astype(BF16)
        cm = xc[:, TOK_W + (SSM_GROUPS + g) * SSM_STATE:TOK_W + (SSM_GROUPS + g + 1) * SSM_STATE].astype(BF16)
        cbs.append(lax.dot_general(cm, bm, (((1,), (1,)), ((), ())), preferred_element_type=F32))
    for pr in range(SSM_HEADS // 2):
        xp = xs[:, pr * LANES:(pr + 1) * LANES].astype(BF16)
        ys = []
        for hh in (2 * pr, 2 * pr + 1):
            g = hh // hg
            seg = cum[:, hh:hh + 1] - cum_t[hh:hh + 1, :]
            decay = jnp.exp(jnp.where(tril, seg, NEG))
            w = cbs[g] * decay * dt_t[hh:hh + 1, :]
            ys.append(jnp.dot(w.astype(BF16), xp, preferred_element_type=F32))
        y_pairs.append(jnp.where(lane < SSM_HD, ys[0], ys[1]))
    y = jnp.concatenate(y_pairs, axis=1)

    expcum = jnp.exp(cum)
    to_end = jnp.exp(cum[tl - 1:tl, :] - cum) * dt
    stacked = jnp.concatenate([expcum, to_end], axis=0)
    exd = _split_dot(stacked, ex_ref[...], terms=2, split_rhs=False)
    expcum_x = exd[:tl]
    xw = (xs * exd[tl:]).astype(BF16)
    gw = TOK_W // SSM_GROUPS
    y_off = []
    for g in range(SSM_GROUPS):
        cs = slice(g * gw, (g + 1) * gw)
        bm_t = xc[:, TOK_W + g * SSM_STATE:TOK_W + (g + 1) * SSM_STATE].T.astype(BF16)
        cm = xc[:, TOK_W + (SSM_GROUPS + g) * SSM_STATE:TOK_W + (SSM_GROUPS + g + 1) * SSM_STATE].astype(BF16)
        sg = st_ref[:, cs]
        y_off.append(jnp.dot(cm, sg.astype(BF16), preferred_element_type=F32) * expcum_x[:, cs])
        st_ref[:, cs] = sg * expcum_x[tl - 1:tl, cs] + jnp.dot(bm_t, xw[:, cs], preferred_element_type=F32)
    y = y + jnp.concatenate(y_off, axis=1) + dsk_ref[...] * xs
    zf = z_ref[0].astype(F32)
    y = y * (zf * jax.nn.sigmoid(zf))
    outs = []
    for g in range(SSM_GROUPS):
        cs = slice(g * gw, (g + 1) * gw)
        outs.append(_rms(y[:, cs], ng_ref[:, cs]))
    o_ref[0] = jnp.concatenate(outs, axis=1).astype(o_ref.dtype)


def _ssd(z3, xbc3, dt3, cw, cb, dtb, alog, dsk, ng, ex, tl=256):
    b, s, _ = z3.shape
    full = lambda a: pl.BlockSpec(a.shape, lambda bi, j: (0, 0))
    blk = lambda n: pl.BlockSpec((1, tl, n), lambda bi, j: (bi, j, 0))
    return pl.pallas_call(
        functools.partial(_ssd_kernel, tl=tl),
        out_shape=jax.ShapeDtypeStruct((b, s, TOK_W), BF16),
        grid=(b, s // tl),
        in_specs=[blk(TOK_W), blk(SSM_CONV_DIM), blk(LANES), full(cw), full(cb), full(dtb), full(alog),
                  full(dsk), full(ng), full(ex)],
        out_specs=blk(TOK_W),
        scratch_shapes=[pltpu.VMEM((tl + 8, SSM_CONV_DIM), F32), pltpu.VMEM((SSM_STATE, TOK_W), F32)],
        compiler_params=_cparams(("parallel", "arbitrary")),
        name="conv_ssd",
    )(z3, xbc3, dt3, cw, cb, dtb, alog, dsk, ng, ex)


def _pad_lanes(v, n=LANES):
    return jnp.pad(v, [(0, 0)] * (v.ndim - 1) + [(0, n - v.shape[-1])])


def kernel(x, mem, ln1_g, ln2_g, mem_norm_g, w_out, mem_w_kv, mem_qn_g, mem_kn_g, da_w_in, da_qn_g, da_kn_g,
           da_lq1, da_lk1, da_lq2, da_lk2, da_sub_g, ssm_w_in, ssm_conv_w, ssm_conv_b, ssm_dt_bias, ssm_a_log,
           ssm_d, ssm_norm_g, ffn_w_gate, ffn_w_up, ffn_w_down, moe_w_router, moe_w_gate, moe_w_up, moe_w_down):
    b, s, d = x.shape
    t = b * s
    row = lambda v: v.reshape(1, -1).astype(F32)
    xt = x.reshape(t, d)

    mem_qg = jnp.tile(mem_qn_g.astype(F32) * (MEM_HD ** -0.5), (1, MEM_HEADS))
    mem_kg = jnp.tile(mem_kn_g.astype(F32), (1, MEM_HEADS))[:, None, :]
    kv = _memkv(mem, row(mem_norm_g), mem_w_kv.astype(BF16), mem_kg)

    lambda_init = 0.8 - 0.6 * math.exp(-0.3 * 0)
    qkg = jnp.concatenate([jnp.tile(da_qn_g[0].astype(F32) * (DA_DH ** -0.5 * LOG2E), 2 * DA_HEADS),
                           jnp.tile(da_kn_g[0].astype(F32), 2 * DA_HEADS)]).reshape(1, -1)
    qt, kn, vt, mq0 = _inproj0(xt, row(ln1_g[0]), da_w_in[0].astype(BF16), qkg, b)
    late_w = ((moe_w_gate[0], CAST_CHUNKS), (moe_w_up[0], CAST_CHUNKS), (moe_w_down[0], CAST_CHUNKS),
              (ffn_w_gate[0], CAST_CHUNKS), (ffn_w_up[0], CAST_CHUNKS), (ffn_w_down[0], CAST_CHUNKS // 4),
              (w_out, CAST_CHUNKS))
    riders = tuple(w.reshape(n, -1, w.shape[-1]) for w, n in late_w)
    tok, w16 = _diff_attention(qt, kn.reshape(b, s, TOK_W), vt, row(da_lq1[0]), row(da_lk1[0]),
                               row(da_lq2[0]), row(da_lk2[0]), row(da_sub_g[0]), lambda_init, riders)
    moe_wg, moe_wu, moe_wd, ffn_wg, ffn_wu, ffn_wd, wo = (a.reshape(w.shape) for a, (w, _) in zip(w16, late_w))
    x1, h2 = _mixout(tok.reshape(t, TOK_W), mq0, 0, kv[0], mem_qg[0:1], wo[0], xt, row(ln2_g[0]))
    x2, h3 = _ffn(h2, x1, ffn_wg, ffn_wu, ffn_wd, row(ln1_g[1]))

    w_in = ssm_w_in[0]
    o2 = TOK_W + SSM_CONV_DIM
    o3 = o2 + SSM_HEADS
    w1 = jnp.concatenate([w_in[:, :o2], w_in[:, o3:], _pad_lanes(w_in[:, o2:o3])], axis=1).astype(BF16)
    z, xbc, mq, dt = _inproj1(h3, w1)
    expand = jnp.repeat(jnp.eye(SSM_HEADS, dtype=F32), SSM_HD, axis=1)
    expand = jnp.pad(expand, ((0, LANES - SSM_HEADS), (0, 0))).astype(BF16)
    dsk = jnp.repeat(ssm_d[0].astype(F32), SSM_HD).reshape(1, -1)
    tok1 = _ssd(z.reshape(b, s, -1), xbc.reshape(b, s, -1), dt.reshape(b, s, -1),
                ssm_conv_w[0].astype(F32), row(ssm_conv_b[0]), _pad_lanes(row(ssm_dt_bias[0])),
                _pad_lanes(row(ssm_a_log[0])), dsk, row(ssm_norm_g[0]), expand)
    x3, hp, route, idx, counts = _mixout(tok1.reshape(t, TOK_W), mq, 0, kv[1], mem_qg[1:2], wo[1], x2,
                                         row(ln2_g[1]), wr=_pad_lanes(moe_w_router[0].astype(F32)))
    x4 = _moe(hp, route, idx, counts, x3, moe_wg, moe_wu, moe_wd)
    return x4.reshape(b, s, d)
```

```python
import functools
import math

import jax
import jax.numpy as jnp
from jax import lax
from jax.experimental import pallas as pl
from jax.experimental.pallas import tpu as pltpu

F32 = jnp.float32
BF16 = jnp.bfloat16

D_MODEL = 1024
CHUNK = 64
MEM_LEN = 256
MEM_W = 256
MEM_HEADS = 4
MEM_HD = 64
TOK_W = 768
DA_DH = 64
DA_HEADS = 6
SSM_HD = 64
SSM_HEADS = 12
SSM_GROUPS = 2
SSM_STATE = 128
SSM_CONV_K = 4
SSM_CONV_DIM = 1280
N_EXPERTS = 8
EPS = 1e-6
LANES = 128
NEG = -1e30
SUBLANES = 8
ONES_ROWS = SUBLANES
LOG2E = math.log2(math.e)
COPY_TILES = 2
CAST_CHUNKS = 64
ISSUE_UNROLL = True
VMEM_LIMIT = 56 * 1024 * 1024


def _cparams(sem):
    return pltpu.CompilerParams(dimension_semantics=sem, vmem_limit_bytes=VMEM_LIMIT)


def _rms(xf, g):
    ms = jnp.mean(xf * xf, axis=-1, keepdims=True)
    return xf * lax.rsqrt(ms + EPS) * g


def _seg_mean_matrix(n, seg_shift):
    r = lax.broadcasted_iota(jnp.int32, (n, n), 0) >> seg_shift
    c = lax.broadcasted_iota(jnp.int32, (n, n), 1) >> seg_shift
    return jnp.where(r == c, 1.0 / (1 << seg_shift), 0.0).astype(BF16)


def _seg_mean_sq(y, seg_shift=6):
    bd = _seg_mean_matrix(256, seg_shift)
    sq = (y * y).astype(BF16)
    parts = [jnp.dot(sq[:, c:c + 256], bd, preferred_element_type=F32)
             for c in range(0, y.shape[1], 256)]
    return parts[0] if len(parts) == 1 else jnp.concatenate(parts, axis=1)


def _inproj0_kernel(x_ref, g_ref, w_ref, qkg_ref, qt_ref, k_ref, vt_ref, mq_ref):
    h = _rms(x_ref[...], g_ref[...]).astype(BF16)
    u = jnp.dot(h, w_ref[...], preferred_element_type=F32)
    tm = u.shape[0]
    hd = 2 * DA_DH
    nqk = 2 * TOK_W
    qk = u[:, :nqk]
    qkn = qk * lax.rsqrt(_seg_mean_sq(qk) + EPS) * qkg_ref[...]
    qt_ref[0] = qkn[:, :TOK_W].T.reshape(DA_HEADS, hd, tm).astype(BF16)
    k_ref[...] = qkn[:, TOK_W:].astype(BF16)
    vt_ref[0, :, 0, :hd, :] = u[:, nqk:nqk + TOK_W].T.reshape(DA_HEADS, hd, tm).astype(BF16)
    vt_ref[0, :, 0, hd:, :] = jnp.ones((DA_HEADS, ONES_ROWS, tm), BF16)
    mq_ref[...] = u[:, nqk + TOK_W:].astype(BF16)


def _inproj0(x, g, w, qkg, b, tm=512):
    t = x.shape[0]
    n = w.shape[1]
    s = t // b
    nt = s // tm
    hd = 2 * DA_DH
    return pl.pallas_call(
        _inproj0_kernel,
        out_shape=[jax.ShapeDtypeStruct((b, DA_HEADS, hd, s), BF16),
                   jax.ShapeDtypeStruct((t, TOK_W), BF16),
                   jax.ShapeDtypeStruct((b, DA_HEADS, nt, hd + ONES_ROWS, tm), BF16),
                   jax.ShapeDtypeStruct((t, MEM_W), BF16)],
        grid=(t // tm,),
        in_specs=[pl.BlockSpec((tm, D_MODEL), lambda i: (i, 0)),
                  pl.BlockSpec((1, D_MODEL), lambda i: (0, 0)),
                  pl.BlockSpec((D_MODEL, n), lambda i: (0, 0)),
                  pl.BlockSpec((1, 2 * TOK_W), lambda i: (0, 0))],
        out_specs=[pl.BlockSpec((1, DA_HEADS, hd, tm), lambda i: (i // nt, 0, 0, i % nt)),
                   pl.BlockSpec((tm, TOK_W), lambda i: (i, 0)),
                   pl.BlockSpec((1, DA_HEADS, 1, hd + ONES_ROWS, tm), lambda i: (i // nt, 0, i % nt, 0, 0)),
                   pl.BlockSpec((tm, MEM_W), lambda i: (i, 0))],
        compiler_params=_cparams(("parallel",)),
        name="inproj0",
    )(x, g, w, qkg)


def _attn_kernel(*refs, tq, tk, nh, n_riders, lambda_init):
    lq1_ref, lk1_ref, lq2_ref, lk2_ref, subg_ref, qt_ref, k_ref, vt_ref = refs[:8]
    rider_in = refs[8:8 + n_riders]
    o_ref = refs[8 + n_riders]
    rider_out = refs[9 + n_riders:9 + 2 * n_riders]
    acc_ref, m_ref, alpha_ref, s_ref, p_ref = refs[9 + 2 * n_riders:]
    for src, dst in zip(rider_in, rider_out):
        dst[...] = src[...].astype(dst.dtype)
    qi = pl.program_id(2)
    hd = 2 * DA_DH
    qs = []
    for hs in range(nh):
        qt = qt_ref[0, hs]
        row = lax.broadcasted_iota(jnp.int32, qt.shape, 0)
        zero = jnp.zeros_like(qt)
        qs += [jnp.where(row < DA_DH, qt, zero), jnp.where(row >= DA_DH, qt, zero)]
    acc_ref[...] = jnp.zeros_like(acc_ref)
    m_ref[...] = jnp.full_like(m_ref, NEG)
    p_ref[1] = jnp.zeros(p_ref.shape[1:], p_ref.dtype)
    alpha_ref[1] = jnp.ones(alpha_ref.shape[1:], alpha_ref.dtype)

    def scores(ki, slot, c):
        start = pl.multiple_of(ki * tk, tk)
        hs = c // 2
        k = k_ref[0, pl.ds(start, tk), hs * hd:(hs + 1) * hd]
        s_ref[slot, c] = jnp.dot(k, qs[c], preferred_element_type=F32)

    def softmax(slot, masked, c):
        s = s_ref[slot, c]
        if masked:
            kc = lax.broadcasted_iota(jnp.int32, (tk, tq), 0) >> 6
            qc = lax.broadcasted_iota(jnp.int32, (tk, tq), 1) >> 6
            s = jnp.where(kc <= qc, s, NEG)
        m_old = m_ref[c]
        m_new = jnp.maximum(m_old, jnp.max(s, axis=0, keepdims=True))
        alpha_ref[slot, c] = jnp.exp2(m_old - m_new)
        p_ref[slot, c] = jnp.exp2(s - m_new).astype(BF16)
        m_ref[c] = m_new

    def values(ki, slot, c):
        vt = vt_ref[0, c // 2, ki]
        acc_ref[c] = alpha_ref[slot, c] * acc_ref[c] + jnp.dot(vt, p_ref[slot, c], preferred_element_type=F32)

    groups = [(2 * hs, 2 * hs + 1) for hs in range(nh)]
    odd = (qi & 1) == 1

    def stage(fn, g, *args):
        for c in g:
            fn(*args, c)

    @pl.when(odd)
    def _():
        for g in groups:
            stage(scores, g, 0, 1)
            stage(scores, g, 1, 0)
            stage(softmax, g, 1, False)

    @pl.when(jnp.logical_not(odd))
    def _():
        for g in groups:
            stage(scores, g, 0, 0)

    def body(j, carry):
        ki = (qi & 1) + 2 * j
        for g in groups:
            stage(scores, g, ki + 1, 1)
            stage(softmax, g, 0, False)
            stage(values, g, jnp.maximum(ki - 1, 0), 1)
            stage(scores, g, ki + 2, 0)
            stage(softmax, g, 1, False)
            stage(values, g, ki, 0)
        return carry

    lax.fori_loop(0, qi >> 1, body, 0)
    for g in groups:
        stage(softmax, g, 0, True)
        stage(values, g, jnp.maximum(qi - 1, 0), 1)
        stage(values, g, qi, 0)

    lam = (jnp.exp(jnp.sum(lq1_ref[...] * lk1_ref[...])) - jnp.exp(jnp.sum(lq2_ref[...] * lk2_ref[...]))
           + lambda_init)
    for hs in range(nh):
        o1 = acc_ref[2 * hs, :hd, :] / acc_ref[2 * hs, hd:hd + 1, :]
        o2 = acc_ref[2 * hs + 1, :hd, :] / acc_ref[2 * hs + 1, hd:hd + 1, :]
        ot = o1 - lam * o2
        ms = jnp.mean(ot * ot, axis=0, keepdims=True)
        ot = ot * lax.rsqrt(ms + EPS)
        o_ref[0, :, hs * hd:(hs + 1) * hd] = (ot.T * (subg_ref[...] * (1.0 - lambda_init))).astype(o_ref.dtype)


def _diff_attention(qt, k3, vt, lq1, lk1, lq2, lk2, subg, lambda_init, riders=(), nh=2):
    b, s, _ = k3.shape
    tk = vt.shape[-1]
    tq = tk
    nq = s // tq
    ng = DA_HEADS // nh
    hd = 2 * DA_DH
    hde = hd + ONES_ROWS
    nc = 2 * nh
    vec = lambda n: pl.BlockSpec((1, n), lambda bi, h, qi: (0, 0))

    def rider_spec(a):
        assert a.shape[0] <= b * ng * nq
        last = a.shape[0] - 1
        return pl.BlockSpec((1,) + a.shape[1:], lambda bi, h, qi: (jnp.minimum((bi * ng + h) * nq + qi, last), 0, 0))

    kern = functools.partial(_attn_kernel, tq=tq, tk=tk, nh=nh, n_riders=len(riders), lambda_init=lambda_init)
    res = pl.pallas_call(
        kern,
        out_shape=[jax.ShapeDtypeStruct((b, s, TOK_W), BF16)] + [jax.ShapeDtypeStruct(a.shape, BF16) for a in riders],
        grid=(b, ng, nq),
        in_specs=[vec(DA_DH), vec(DA_DH), vec(DA_DH), vec(DA_DH), vec(hd),
                  pl.BlockSpec((1, nh, hd, tq), lambda bi, h, qi: (bi, h, 0, qi)),
                  pl.BlockSpec((1, s, nh * hd), lambda bi, h, qi: (bi, 0, h)),
                  pl.BlockSpec((1, nh, s // tk, hde, tk), lambda bi, h, qi: (bi, h, 0, 0, 0))]
                 + [rider_spec(a) for a in riders],
        out_specs=[pl.BlockSpec((1, tq, nh * hd), lambda bi, h, qi: (bi, qi, h))] + [rider_spec(a) for a in riders],
        scratch_shapes=[pltpu.VMEM((nc, hde, tq), F32), pltpu.VMEM((nc, 1, tq), F32),
                        pltpu.VMEM((2, nc, 1, tq), F32), pltpu.VMEM((2, nc, tk, tq), F32),
                        pltpu.VMEM((2, nc, tk, tq), BF16)],
        compiler_params=_cparams(("arbitrary", "arbitrary", "arbitrary")),
        name="diff_attn",
    )(lq1, lk1, lq2, lk2, subg, qt, k3, vt, *riders)
    return res[0], res[1:]


def _memkv_kernel(mem_ref, g_ref, wkv_ref, kng_ref, kv_ref):
    mn = _rms(mem_ref[0], g_ref[...]).astype(BF16)
    kv = jnp.dot(mn, wkv_ref[0], preferred_element_type=F32)
    k = kv[:, :MEM_W]
    kn = k * lax.rsqrt(_seg_mean_sq(k) + EPS) * kng_ref[0]
    kv_ref[0, 0, :, :MEM_W] = kn.astype(BF16)
    kv_ref[0, 0, :, MEM_W:] = kv[:, MEM_W:].astype(BF16)


def _memkv(mem, g, wkv, kng):
    depth = wkv.shape[0]
    b = mem.shape[0]
    return pl.pallas_call(
        _memkv_kernel,
        out_shape=jax.ShapeDtypeStruct((depth, b, MEM_LEN, 2 * MEM_W), BF16),
        grid=(depth, b),
        in_specs=[pl.BlockSpec((1, MEM_LEN, D_MODEL), lambda d, bi: (bi, 0, 0)),
                  pl.BlockSpec((1, D_MODEL), lambda d, bi: (0, 0)),
                  pl.BlockSpec((1, D_MODEL, 2 * MEM_W), lambda d, bi: (d, 0, 0)),
                  pl.BlockSpec((1, 1, MEM_W), lambda d, bi: (d, 0, 0))],
        out_specs=pl.BlockSpec((1, 1, MEM_LEN, 2 * MEM_W), lambda d, bi: (d, bi, 0, 0)),
        compiler_params=_cparams(("parallel", "parallel")),
        name="memkv",
    )(mem, g, wkv, kng)


def _mem_attention(mq, qg, kv):
    qn = (mq * lax.rsqrt(_seg_mean_sq(mq) + EPS) * qg).astype(BF16)
    k = kv[:, :MEM_W]
    v = kv[:, MEM_W:]
    lane = lax.broadcasted_iota(jnp.int32, qn.shape, 1) >> 6
    out = jnp.zeros(qn.shape, F32)
    for h in range(MEM_HEADS):
        sel = lane == h
        qh = jnp.where(sel, qn, jnp.zeros_like(qn))
        s = lax.dot_general(qh, k, (((1,), (1,)), ((), ())), preferred_element_type=F32)
        p = jnp.exp(s - jnp.max(s, axis=1, keepdims=True))
        l = jnp.sum(p, axis=1, keepdims=True)
        o = jnp.dot(p.astype(BF16), v, preferred_element_type=F32)
        out = jnp.where(sel, o / l, out)
    return out


R_E1, R_E2, R_G1, R_G2, R_RANK1, R_RANK2 = range(6)


def _top2_route(logits, count_ref):
    assert N_EXPERTS == SUBLANES
    tm = logits.shape[0]
    lt = logits.T[:SUBLANES, :]
    row = lax.broadcasted_iota(jnp.int32, lt.shape, 0)
    m1 = jnp.max(lt, axis=0, keepdims=True)
    i1 = jnp.min(jnp.where(lt == m1, row, N_EXPERTS), axis=0, keepdims=True)
    lt2 = jnp.where(row == i1, NEG, lt)
    m2 = jnp.max(lt2, axis=0, keepdims=True)
    i2 = jnp.min(jnp.where(lt2 == m2, row, N_EXPERTS), axis=0, keepdims=True)
    g1 = 1.0 / (1.0 + jnp.exp(m2 - m1))
    g2 = 1.0 - g1
    chosen = jnp.logical_or(row == i1, row == i2)
    onehot = jnp.where(chosen, 1.0, 0.0)
    r = lax.broadcasted_iota(jnp.int32, (tm, tm), 0)
    c = lax.broadcasted_iota(jnp.int32, (tm, tm), 1)
    earlier = jnp.where(r < c, 1.0, 0.0).astype(BF16)
    onehot16 = jnp.concatenate([onehot, jnp.zeros_like(onehot)], axis=0).astype(BF16)
    prefix = jnp.dot(onehot16, earlier, preferred_element_type=F32)[:SUBLANES] + count_ref[:, :1]
    count_ref[...] += jnp.sum(onehot, axis=1, keepdims=True)
    rank1 = jnp.sum(jnp.where(row == i1, prefix, 0.0), axis=0, keepdims=True)
    rank2 = jnp.sum(jnp.where(row == i2, prefix, 0.0), axis=0, keepdims=True)
    rec = jnp.zeros(lt.shape, F32)
    for idx, val in ((R_E1, i1.astype(F32)), (R_E2, i2.astype(F32)), (R_G1, g1), (R_G2, g2),
                     (R_RANK1, rank1), (R_RANK2, rank2)):
        rec = jnp.where(row == idx, val, rec)
    return rec


def _mixout_kernel(*refs, with_router):
    if with_router:
        (tok_ref, mq_ref, kv_ref, qg_ref, wo_ref, x_ref, g2_ref, wr_ref,
         x1_ref, hp_ref, route_ref, idx_ref, count_ref) = refs
    else:
        tok_ref, mq_ref, kv_ref, qg_ref, wo_ref, x_ref, g2_ref, x1_ref, h2_ref = refs
    mo = _mem_attention(mq_ref[...].astype(F32), qg_ref[...], kv_ref[0])
    y = jnp.dot(tok_ref[...], wo_ref[:TOK_W, :], preferred_element_type=F32)
    y = y + jnp.dot(mo.astype(BF16), wo_ref[TOK_W:, :], preferred_element_type=F32)
    x1 = x_ref[...] + y
    x1_ref[...] = x1
    h2 = _rms(x1, g2_ref[...])
    if with_router:
        @pl.when(pl.program_id(0) == 0)
        def _():
            count_ref[...] = jnp.zeros_like(count_ref)

        hp_ref[...] = h2
        wr = wr_ref[...]
        w_hi = wr.astype(BF16)
        w_lo = (wr - w_hi.astype(F32)).astype(BF16)
        h_hi = h2.astype(BF16)
        h_lo = (h2 - h_hi.astype(F32)).astype(BF16)
        both = jnp.dot(h_hi, jnp.concatenate([w_hi, w_lo], axis=1), preferred_element_type=F32)
        logits = both[:, :LANES] + both[:, LANES:] + jnp.dot(h_lo, w_hi, preferred_element_type=F32)
        rec = _top2_route(logits, count_ref)
        idx_ref[0] = rec.astype(jnp.int32)
        route_ref[...] = jnp.concatenate([rec, jnp.zeros((LANES - SUBLANES, rec.shape[1]), F32)], axis=0).T
    else:
        h2_ref[...] = h2.astype(BF16)


def _mixout(tok, mq_src, mq_block, kv, qg, wo, x, g2, wr=None, tm=512):
    t = x.shape[0]
    tiles_per_batch = t // kv.shape[0] // tm
    with_router = wr is not None
    in_specs = [pl.BlockSpec((tm, TOK_W), lambda i: (i, 0)),
                pl.BlockSpec((tm, MEM_W), lambda i: (i, mq_block)),
                pl.BlockSpec((1, MEM_LEN, 2 * MEM_W), lambda i: (i // tiles_per_batch, 0, 0)),
                pl.BlockSpec((1, MEM_W), lambda i: (0, 0)),
                pl.BlockSpec((D_MODEL, D_MODEL), lambda i: (0, 0)),
                pl.BlockSpec((tm, D_MODEL), lambda i: (i, 0)),
                pl.BlockSpec((1, D_MODEL), lambda i: (0, 0))]
    args = [tok, mq_src, kv, qg, wo, x, g2]
    row_spec = lambda n: pl.BlockSpec((tm, n), lambda i: (i, 0))
    if with_router:
        in_specs.append(pl.BlockSpec((D_MODEL, LANES), lambda i: (0, 0)))
        args.append(wr)
        out_shape = [jax.ShapeDtypeStruct((t, D_MODEL), F32), jax.ShapeDtypeStruct((t, D_MODEL), F32),
                     jax.ShapeDtypeStruct((t, LANES), F32), jax.ShapeDtypeStruct((t // tm, SUBLANES, tm), jnp.int32),
                     jax.ShapeDtypeStruct((SUBLANES, LANES), F32)]
        out_specs = [row_spec(D_MODEL), row_spec(D_MODEL), row_spec(LANES),
                     pl.BlockSpec((1, SUBLANES, tm), lambda i: (i, 0, 0)),
                     pl.BlockSpec((SUBLANES, LANES), lambda i: (0, 0))]
    else:
        out_shape = [jax.ShapeDtypeStruct((t, D_MODEL), F32), jax.ShapeDtypeStruct((t, D_MODEL), BF16)]
        out_specs = [row_spec(D_MODEL), row_spec(D_MODEL)]
    return pl.pallas_call(
        functools.partial(_mixout_kernel, with_router=with_router),
        out_shape=out_shape,
        grid=(t // tm,),
        in_specs=in_specs,
        out_specs=out_specs,
        compiler_params=_cparams(("arbitrary",) if with_router else ("parallel",)),
        name="mixout_router" if with_router else "mixout",
    )(*args)


def _ffn_kernel(h_ref, x_ref, wg_ref, wu_ref, wd_ref, gn_ref, xo_ref, ho_ref, acc_ref):
    f = pl.program_id(1)
    h = h_ref[...]
    a = jnp.dot(h, wg_ref[...], preferred_element_type=F32)
    u = jnp.dot(h, wu_ref[...], preferred_element_type=F32)
    act = a * jax.nn.sigmoid(a) * u
    contrib = jnp.dot(act.astype(BF16), wd_ref[...], preferred_element_type=F32)
    acc_ref[...] = jnp.where(f == 0, x_ref[...], acc_ref[...]) + contrib

    @pl.when(f == pl.num_programs(1) - 1)
    def _():
        xn = acc_ref[...]
        xo_ref[...] = xn
        ho_ref[...] = _rms(xn, gn_ref[...]).astype(BF16)


def _ffn(h, x, wg, wu, wd, gn, tm=512, tf=1408):
    t = x.shape[0]
    fdim = wg.shape[1]
    row = pl.BlockSpec((tm, D_MODEL), lambda i, f: (i, 0))
    return pl.pallas_call(
        _ffn_kernel,
        out_shape=[jax.ShapeDtypeStruct((t, D_MODEL), F32), jax.ShapeDtypeStruct((t, D_MODEL), BF16)],
        grid=(t // tm, fdim // tf),
        in_specs=[row, row,
                  pl.BlockSpec((D_MODEL, tf), lambda i, f: (0, f)),
                  pl.BlockSpec((D_MODEL, tf), lambda i, f: (0, f)),
                  pl.BlockSpec((tf, D_MODEL), lambda i, f: (f, 0)),
                  pl.BlockSpec((1, D_MODEL), lambda i, f: (0, 0))],
        out_specs=[row, row],
        scratch_shapes=[pltpu.VMEM((tm, D_MODEL), F32)],
        compiler_params=_cparams(("parallel", "arbitrary")),
        name="ffn",
    )(h, x, wg, wu, wd, gn)


def _row_copy(src_ref, src_row, dst_ref, dst_row, sem):
    return pltpu.make_async_copy(src_ref.at[pl.ds(src_row, 1), :], dst_ref.at[pl.ds(dst_row, 1), :], sem)


def _dispatch_kernel(pad_start_ref, pad_len_ref, pos_ref, hp_ref, xs_ref, zeros_ref, sem, zsem, *, tm, tmr):
    nbits = tmr.bit_length() - 1

    @pl.when(pl.program_id(0) == 0)
    def _():
        zeros_ref[...] = jnp.zeros_like(zeros_ref)
        for e in range(N_EXPERTS):
            start = pad_start_ref[e]
            length = pad_len_ref[e]
            singles = length & (SUBLANES - 1)
            for j in range(SUBLANES - 1):
                @pl.when(j < singles)
                def _(j=j, start=start):
                    cp = _row_copy(zeros_ref, 0, xs_ref, start + j, zsem)
                    cp.start()
                    cp.wait()

            done = start + singles
            for b in range(SUBLANES.bit_length() - 1, nbits):
                n = 1 << b
                bit = (length >> b) & 1

                @pl.when(bit == 1)
                def _(n=n, done=done):
                    cp = pltpu.make_async_copy(zeros_ref.at[pl.ds(0, n), :],
                                               xs_ref.at[pl.ds(pl.multiple_of(done, SUBLANES), n), :], zsem)
                    cp.start()
                    cp.wait()

                done = done + bit * n

        n_tiles = xs_ref.shape[0] // tmr
        half = tmr // 2
        for j in range(n_tiles - N_EXPERTS, n_tiles):
            @pl.when(j * tmr >= pad_start_ref[N_EXPERTS])
            def _(j=j):
                for c in range(2):
                    cp = pltpu.make_async_copy(zeros_ref, xs_ref.at[pl.ds(j * tmr + c * half, half), :], zsem)
                    cp.start()
                    cp.wait()

    def issue(r, carry):
        _row_copy(hp_ref, r, xs_ref, pos_ref[0, 0, r], sem).start(priority=0)
        _row_copy(hp_ref, r, xs_ref, pos_ref[0, 1, r], sem).start(priority=1)
        return carry

    lax.fori_loop(0, tm, issue, 0, unroll=ISSUE_UNROLL)
    for _ in range(2):
        pltpu.make_async_copy(hp_ref, xs_ref.at[pl.ds(0, tm), :], sem).wait()


def _dispatch(pad_start, pad_len, pos, hp, n_rows, tm, tmr):
    t, w = hp.shape
    return pl.pallas_call(
        functools.partial(_dispatch_kernel, tm=tm, tmr=tmr),
        out_shape=jax.ShapeDtypeStruct((n_rows, w), hp.dtype),
        grid_spec=pltpu.PrefetchScalarGridSpec(
            num_scalar_prefetch=2,
            grid=(t // tm,),
            in_specs=[pl.BlockSpec((1, 2, tm), lambda i, ps, pn: (i, 0, 0), memory_space=pltpu.SMEM),
                      pl.BlockSpec((tm, w), lambda i, ps, pn: (i, 0))],
            out_specs=pl.BlockSpec(memory_space=pl.ANY),
            scratch_shapes=[pltpu.VMEM((tmr // 2, w), hp.dtype), pltpu.SemaphoreType.DMA,
                            pltpu.SemaphoreType.DMA]),
        compiler_params=_cparams(("arbitrary",)),
        name="moe_dispatch",
    )(pad_start, pad_len, pos, hp)


def _experts_kernel(te_ref, tv_ref, xs_ref, wg_ref, wu_ref, wd_ref, y_ref):
    i = pl.program_id(0)
    f = pl.program_id(1)

    @pl.when(tv_ref[i] == 1)
    def _():
        h = xs_ref[...].astype(BF16)
        a = jnp.dot(h, wg_ref[0], preferred_element_type=F32)
        u = jnp.dot(h, wu_ref[0], preferred_element_type=F32)
        act = a * jax.nn.sigmoid(a) * u
        contrib = jnp.dot(act.astype(BF16), wd_ref[0], preferred_element_type=F32)
        y_ref[...] = jnp.where(f == 0, 0.0, y_ref[...]) + contrib

    @pl.when(jnp.logical_and(tv_ref[i] == 0, f == 0))
    def _():
        y_ref[...] = jnp.zeros_like(y_ref)


def _experts(tile_expert, tile_valid, xs, wg, wu, wd, tmr, tf):
    n_rows, w = xs.shape
    fdim = wg.shape[2]
    nf = fdim // tf
    fidx = lambda i, f, te, tv: jnp.where(tv[i] == 1, f, nf - 1)
    return pl.pallas_call(
        _experts_kernel,
        out_shape=jax.ShapeDtypeStruct((n_rows, D_MODEL), F32),
        grid_spec=pltpu.PrefetchScalarGridSpec(
            num_scalar_prefetch=2,
            grid=(n_rows // tmr, nf),
            in_specs=[pl.BlockSpec((tmr, w), lambda i, f, te, tv: (jnp.where(tv[i] == 1, i, 0), 0)),
                      pl.BlockSpec((1, D_MODEL, tf), lambda i, f, te, tv: (te[i], 0, fidx(i, f, te, tv))),
                      pl.BlockSpec((1, D_MODEL, tf), lambda i, f, te, tv: (te[i], 0, fidx(i, f, te, tv))),
                      pl.BlockSpec((1, tf, D_MODEL), lambda i, f, te, tv: (te[i], fidx(i, f, te, tv), 0))],
            out_specs=pl.BlockSpec((tmr, D_MODEL), lambda i, f, te, tv: (i, 0))),
        compiler_params=_cparams(("parallel", "arbitrary")),
        name="moe_experts",
    )(tile_expert, tile_valid, xs, wg, wu, wd)


def _combine_kernel(pos_ref, route_ref, x_ref, y_ref, o_ref, buf_ref, sem, *, tm):
    def issue(r, carry):
        _row_copy(y_ref, pos_ref[0, 0, r], buf_ref.at[0], r, sem).start(priority=0)
        _row_copy(y_ref, pos_ref[0, 1, r], buf_ref.at[1], r, sem).start(priority=1)
        return carry

    lax.fori_loop(0, tm, issue, 0, unroll=ISSUE_UNROLL)
    for k in range(2):
        pltpu.make_async_copy(y_ref.at[pl.ds(0, tm), :], buf_ref.at[k], sem).wait()
    route = route_ref[...]
    g1 = route[:, R_G1:R_G1 + 1]
    g2 = route[:, R_G2:R_G2 + 1]
    o_ref[...] = x_ref[...] + (g1 * buf_ref[0] + g2 * buf_ref[1])


def _combine(pos, route, x, y, tm):
    t = x.shape[0]
    return pl.pallas_call(
        functools.partial(_combine_kernel, tm=tm),
        out_shape=jax.ShapeDtypeStruct((t, D_MODEL), F32),
        grid=(t // tm,),
        in_specs=[pl.BlockSpec((1, 2, tm), lambda i: (i, 0, 0), memory_space=pltpu.SMEM),
                  pl.BlockSpec((tm, LANES), lambda i: (i, 0)),
                  pl.BlockSpec((tm, D_MODEL), lambda i: (i, 0)),
                  pl.BlockSpec(memory_space=pl.ANY)],
        out_specs=pl.BlockSpec((tm, D_MODEL), lambda i: (i, 0)),
        scratch_shapes=[pltpu.VMEM((2, tm, D_MODEL), F32), pltpu.SemaphoreType.DMA],
        compiler_params=_cparams(("arbitrary",)),
        name="moe_combine",
    )(pos, route, x, y)


def _moe(hp, route, idx, counts, x, wg, wu, wd, tmr=512, tf=1792):
    t = hp.shape[0]
    tm = idx.shape[2]
    n_tiles = 2 * t // tmr + N_EXPERTS
    n_rows = n_tiles * tmr
    cnt = counts[:, 0].astype(jnp.int32)
    padded = (cnt + tmr - 1) // tmr * tmr
    ends = jnp.cumsum(padded)
    offs = ends - padded
    def sorted_rows(e, rank):
        off = jnp.zeros_like(e)
        for k in range(N_EXPERTS):
            off = jnp.where(e == k, offs[k], off)
        return off + rank
    pos = jnp.stack([sorted_rows(idx[:, R_E1, :], idx[:, R_RANK1, :]),
                     sorted_rows(idx[:, R_E2, :], idx[:, R_RANK2, :])], axis=1)
    pos = pos.reshape(-1, COPY_TILES, 2, tm).transpose(0, 2, 1, 3).reshape(-1, 2, COPY_TILES * tm)
    tm = COPY_TILES * tm
    tile_start = jnp.arange(n_tiles, dtype=jnp.int32) * tmr
    tile_valid = (tile_start < ends[-1]).astype(jnp.int32)
    last_tile = ends[-1] // tmr - 1
    clamped = jnp.minimum(tile_start, last_tile * tmr)
    tile_expert = jnp.sum((clamped[:, None] >= ends[None, :]).astype(jnp.int32), axis=1)
    pad_start = jnp.concatenate([offs + cnt, ends[-1:]])
    xs = _dispatch(pad_start, padded - cnt, pos, hp, n_rows, tm, tmr)
    y = _experts(tile_expert, tile_valid, xs, wg, wu, wd, tmr, tf)
    return _combine(pos, route, x, y, tm)


def _inproj1_kernel(x_ref, w_ref, z_ref, xbc_ref, mq_ref, dt_ref):
    h = x_ref[...]
    c1 = TOK_W
    c2 = c1 + SSM_CONV_DIM
    c3 = c2 + MEM_W
    z_ref[...] = jnp.dot(h, w_ref[:, :c1], preferred_element_type=F32).astype(BF16)
    xbc_ref[...] = jnp.dot(h, w_ref[:, c1:c2], preferred_element_type=F32).astype(BF16)
    mq_ref[...] = jnp.dot(h, w_ref[:, c2:c3], preferred_element_type=F32).astype(BF16)
    dt_ref[...] = jnp.dot(h, w_ref[:, c3:], preferred_element_type=F32)


def _inproj1(h, w, tm=512):
    t = h.shape[0]
    row = lambda n: pl.BlockSpec((tm, n), lambda i: (i, 0))
    return pl.pallas_call(
        _inproj1_kernel,
        out_shape=[jax.ShapeDtypeStruct((t, TOK_W), BF16), jax.ShapeDtypeStruct((t, SSM_CONV_DIM), BF16),
                   jax.ShapeDtypeStruct((t, MEM_W), BF16), jax.ShapeDtypeStruct((t, LANES), F32)],
        grid=(t // tm,),
        in_specs=[row(D_MODEL), pl.BlockSpec(w.shape, lambda i: (0, 0))],
        out_specs=[row(TOK_W), row(SSM_CONV_DIM), row(MEM_W), row(LANES)],
        compiler_params=_cparams(("parallel",)),
        name="inproj1",
    )(h, w)


def _split_dot(a, b, terms, split_rhs):
    rem = b if split_rhs else a
    pieces = []
    for _ in range(terms):
        piece = rem.astype(BF16)
        pieces.append(piece)
        rem = rem - piece.astype(F32)
    if split_rhs:
        n = b.shape[1]
        wide = jnp.dot(a, jnp.concatenate(pieces, axis=1), preferred_element_type=F32)
        out = wide[:, :n]
        for t in range(1, terms):
            out = out + wide[:, t * n:(t + 1) * n]
        return out
    return jnp.dot(jnp.concatenate(pieces, axis=1), jnp.concatenate([b] * terms, axis=0),
                   preferred_element_type=F32)


def _ssd_kernel(z_ref, xbc_ref, dt_ref, cw_ref, cb_ref, dtb_ref, alog_ref, dsk_ref, ng_ref, ex_ref,
                o_ref, xe_ref, st_ref, *, tl):
    j = pl.program_id(1)
    pad = 8

    @pl.when(j == 0)
    def _():
        xe_ref[0:pad, :] = jnp.zeros((pad, SSM_CONV_DIM), F32)
        st_ref[...] = jnp.zeros_like(st_ref)

    x = xbc_ref[0].astype(F32)
    xe_ref[pad:pad + tl, :] = x
    acc = cb_ref[...] + cw_ref[SSM_CONV_K - 1:SSM_CONV_K, :] * x
    for k in range(SSM_CONV_K - 1):
        sh = SSM_CONV_K - 1 - k
        acc = acc + cw_ref[k:k + 1, :] * xe_ref[pad - sh:pad - sh + tl, :]
    xe_ref[0:pad, :] = x[tl - pad:tl, :]
    xc = acc * jax.nn.sigmoid(acc)
    xs = xc[:, :TOK_W]

    dt_in = dt_ref[0] + dtb_ref[...]
    dt = jnp.maximum(dt_in, 0.0) + jnp.log1p(jnp.exp(-jnp.abs(dt_in)))
    a = -jnp.exp(alog_ref[...])
    dta = dt * a
    r = lax.broadcasted_iota(jnp.int32, (tl, tl), 0)
    c = lax.broadcasted_iota(jnp.int32, (tl, tl), 1)
    tril = r >= c
    ltri = jnp.where(tril, 1.0, 0.0).astype(BF16)
    cum = _split_dot(ltri, dta, terms=3, split_rhs=True)
    cum_t = cum.T
    dt_t = dt.T
    lane = lax.broadcasted_iota(jnp.int32, (tl, LANES), 1)

    hg = SSM_HEADS // SSM_GROUPS
    y_pairs = []
    cbs = []
    for g in range(SSM_GROUPS):
        bm = xc[:, TOK_W + g * SSM_STATE:TOK_W + (g + 1) * SSM_STATE].astype(BF16)
        cm = xc[:, TOK_W + (SSM_GROUPS + g) * SSM_STATE:TOK_W + (SSM_GROUPS + g + 1) * SSM_STATE].astype(BF16)
        cbs.append(lax.dot_general(cm, bm, (((1,), (1,)), ((), ())), preferred_element_type=F32))
    for pr in range(SSM_HEADS // 2):
        xp = xs[:, pr * LANES:(pr + 1) * LANES].astype(BF16)
        ys = []
        for hh in (2 * pr, 2 * pr + 1):
            g = hh // hg
            seg = cum[:, hh:hh + 1] - cum_t[hh:hh + 1, :]
            decay = jnp.exp(jnp.where(tril, seg, NEG))
            w = cbs[g] * decay * dt_t[hh:hh + 1, :]
            ys.append(jnp.dot(w.astype(BF16), xp, preferred_element_type=F32))
        y_pairs.append(jnp.where(lane < SSM_HD, ys[0], ys[1]))
    y = jnp.concatenate(y_pairs, axis=1)

    expcum = jnp.exp(cum)
    to_end = jnp.exp(cum[tl - 1:tl, :] - cum) * dt
    stacked = jnp.concatenate([expcum, to_end], axis=0)
    exd = _split_dot(stacked, ex_ref[...], terms=2, split_rhs=False)
    expcum_x = exd[:tl]
    xw = (xs * exd[tl:]).astype(BF16)
    gw = TOK_W // SSM_GROUPS
    y_off = []
    for g in range(SSM_GROUPS):
        cs = slice(g * gw, (g + 1) * gw)
        bm_t = xc[:, TOK_W + g * SSM_STATE:TOK_W + (g + 1) * SSM_STATE].T.astype(BF16)
        cm = xc[:, TOK_W + (SSM_GROUPS + g) * SSM_STATE:TOK_W + (SSM_GROUPS + g + 1) * SSM_STATE].astype(BF16)
        sg = st_ref[:, cs]
        y_off.append(jnp.dot(cm, sg.astype(BF16), preferred_element_type=F32) * expcum_x[:, cs])
        st_ref[:, cs] = sg * expcum_x[tl - 1:tl, cs] + jnp.dot(bm_t, xw[:, cs], preferred_element_type=F32)
    y = y + jnp.concatenate(y_off, axis=1) + dsk_ref[...] * xs
    zf = z_ref[0].astype(F32)
    y = y * (zf * jax.nn.sigmoid(zf))
    outs = []
    for g in range(SSM_GROUPS):
        cs = slice(g * gw, (g + 1) * gw)
        outs.append(_rms(y[:, cs], ng_ref[:, cs]))
    o_ref[0] = jnp.concatenate(outs, axis=1).astype(o_ref.dtype)


def _ssd(z3, xbc3, dt3, cw, cb, dtb, alog, dsk, ng, ex, tl=256):
    b, s, _ = z3.shape
    full = lambda a: pl.BlockSpec(a.shape, lambda bi, j: (0, 0))
    blk = lambda n: pl.BlockSpec((1, tl, n), lambda bi, j: (bi, j, 0))
    return pl.pallas_call(
        functools.partial(_ssd_kernel, tl=tl),
        out_shape=jax.ShapeDtypeStruct((b, s, TOK_W), BF16),
        grid=(b, s // tl),
        in_specs=[blk(TOK_W), blk(SSM_CONV_DIM), blk(LANES), full(cw), full(cb), full(dtb), full(alog),
                  full(dsk), full(ng), full(ex)],
        out_specs=blk(TOK_W),
        scratch_shapes=[pltpu.VMEM((tl + 8, SSM_CONV_DIM), F32), pltpu.VMEM((SSM_STATE, TOK_W), F32)],
        compiler_params=_cparams(("parallel", "arbitrary")),
        name="conv_ssd",
    )(z3, xbc3, dt3, cw, cb, dtb, alog, dsk, ng, ex)


def _pad_lanes(v, n=LANES):
    return jnp.pad(v, [(0, 0)] * (v.ndim - 1) + [(0, n - v.shape[-1])])


def kernel(x, mem, ln1_g, ln2_g, mem_norm_g, w_out, mem_w_kv, mem_qn_g, mem_kn_g, da_w_in, da_qn_g, da_kn_g,
           da_lq1, da_lk1, da_lq2, da_lk2, da_sub_g, ssm_w_in, ssm_conv_w, ssm_conv_b, ssm_dt_bias, ssm_a_log,
           ssm_d, ssm_norm_g, ffn_w_gate, ffn_w_up, ffn_w_down, moe_w_router, moe_w_gate, moe_w_up, moe_w_down):
    b, s, d = x.shape
    t = b * s
    row = lambda v: v.reshape(1, -1).astype(F32)
    xt = x.reshape(t, d)

    mem_qg = jnp.tile(mem_qn_g.astype(F32) * (MEM_HD ** -0.5), (1, MEM_HEADS))
    mem_kg = jnp.tile(mem_kn_g.astype(F32), (1, MEM_HEADS))[:, None, :]
    kv = _memkv(mem, row(mem_norm_g), mem_w_kv.astype(BF16), mem_kg)

    lambda_init = 0.8 - 0.6 * math.exp(-0.3 * 0)
    qkg = jnp.concatenate([jnp.tile(da_qn_g[0].astype(F32) * (DA_DH ** -0.5 * LOG2E), 2 * DA_HEADS),
                           jnp.tile(da_kn_g[0].astype(F32), 2 * DA_HEADS)]).reshape(1, -1)
    qt, kn, vt, mq0 = _inproj0(xt, row(ln1_g[0]), da_w_in[0].astype(BF16), qkg, b)
    late_w = ((moe_w_gate[0], CAST_CHUNKS), (moe_w_up[0], CAST_CHUNKS), (moe_w_down[0], CAST_CHUNKS),
              (ffn_w_gate[0], CAST_CHUNKS), (ffn_w_up[0], CAST_CHUNKS), (ffn_w_down[0], CAST_CHUNKS // 4),
              (w_out, CAST_CHUNKS))
    riders = tuple(w.reshape(n, -1, w.shape[-1]) for w, n in late_w)
    tok, w16 = _diff_attention(qt, kn.reshape(b, s, TOK_W), vt, row(da_lq1[0]), row(da_lk1[0]),
                               row(da_lq2[0]), row(da_lk2[0]), row(da_sub_g[0]), lambda_init, riders)
    moe_wg, moe_wu, moe_wd, ffn_wg, ffn_wu, ffn_wd, wo = (a.reshape(w.shape) for a, (w, _) in zip(w16, late_w))
    x1, h2 = _mixout(tok.reshape(t, TOK_W), mq0, 0, kv[0], mem_qg[0:1], wo[0], xt, row(ln2_g[0]))
    x2, h3 = _ffn(h2, x1, ffn_wg, ffn_wu, ffn_wd, row(ln1_g[1]))

    w_in = ssm_w_in[0]
    o2 = TOK_W + SSM_CONV_DIM
    o3 = o2 + SSM_HEADS
    w1 = jnp.concatenate([w_in[:, :o2], w_in[:, o3:], _pad_lanes(w_in[:, o2:o3])], axis=1).astype(BF16)
    z, xbc, mq, dt = _inproj1(h3, w1)
    expand = jnp.repeat(jnp.eye(SSM_HEADS, dtype=F32), SSM_HD, axis=1)
    expand = jnp.pad(expand, ((0, LANES - SSM_HEADS), (0, 0))).astype(BF16)
    dsk = jnp.repeat(ssm_d[0].astype(F32), SSM_HD).reshape(1, -1)
    tok1 = _ssd(z.reshape(b, s, -1), xbc.reshape(b, s, -1), dt.reshape(b, s, -1),
                ssm_conv_w[0].astype(F32), row(ssm_conv_b[0]), _pad_lanes(row(ssm_dt_bias[0])),
                _pad_lanes(row(ssm_a_log[0])), dsk, row(ssm_norm_g[0]), expand)
    x3, hp, route, idx, counts = _mixout(tok1.reshape(t, TOK_W), mq, 0, kv[1], mem_qg[1:2], wo[1], x2,
                                         row(ln2_g[1]), wr=_pad_lanes(moe_w_router[0].astype(F32)))
    x4 = _moe(hp, route, idx, counts, x3, moe_wg, moe_wu, moe_wd)
    return x4.reshape(b, s, d)
```

```python
import functools
import math

import jax
import jax.numpy as jnp
from jax import lax
from jax.experimental import pallas as pl
from jax.experimental.pallas import tpu as pltpu

F32 = jnp.float32
BF16 = jnp.bfloat16

D_MODEL = 1024
CHUNK = 64
MEM_LEN = 256
MEM_W = 256
MEM_HEADS = 4
MEM_HD = 64
TOK_W = 768
DA_DH = 64
DA_HEADS = 6
SSM_HD = 64
SSM_HEADS = 12
SSM_GROUPS = 2
SSM_STATE = 128
SSM_CONV_K = 4
SSM_CONV_DIM = 1280
N_EXPERTS = 8
EPS = 1e-6
LANES = 128
NEG = -1e30
SUBLANES = 8
ONES_ROWS = SUBLANES
LOG2E = math.log2(math.e)
COPY_TILES = 2
CAST_CHUNKS = 64
ISSUE_UNROLL = True
VMEM_LIMIT = 56 * 1024 * 1024


def _cparams(sem):
    return pltpu.CompilerParams(dimension_semantics=sem, vmem_limit_bytes=VMEM_LIMIT)


def _rms(xf, g):
    ms = jnp.mean(xf * xf, axis=-1, keepdims=True)
    return xf * lax.rsqrt(ms + EPS) * g


def _seg_mean_matrix(n, seg_shift):
    r = lax.broadcasted_iota(jnp.int32, (n, n), 0) >> seg_shift
    c = lax.broadcasted_iota(jnp.int32, (n, n), 1) >> seg_shift
    return jnp.where(r == c, 1.0 / (1 << seg_shift), 0.0).astype(BF16)


def _seg_mean_sq(y, seg_shift=6):
    bd = _seg_mean_matrix(256, seg_shift)
    sq = (y * y).astype(BF16)
    parts = [jnp.dot(sq[:, c:c + 256], bd, preferred_element_type=F32)
             for c in range(0, y.shape[1], 256)]
    return parts[0] if len(parts) == 1 else jnp.concatenate(parts, axis=1)


def _inproj0_kernel(x_ref, g_ref, w_ref, qkg_ref, qt_ref, k_ref, vt_ref, mq_ref):
    h = _rms(x_ref[...], g_ref[...]).astype(BF16)
    u = jnp.dot(h, w_ref[...], preferred_element_type=F32)
    tm = u.shape[0]
    hd = 2 * DA_DH
    nqk = 2 * TOK_W
    qk = u[:, :nqk]
    qkn = qk * lax.rsqrt(_seg_mean_sq(qk) + EPS) * qkg_ref[...]
    qt_ref[0] = qkn[:, :TOK_W].T.reshape(DA_HEADS, hd, tm).astype(BF16)
    k_ref[...] = qkn[:, TOK_W:].astype(BF16)
    vt_ref[0, :, 0, :hd, :] = u[:, nqk:nqk + TOK_W].T.reshape(DA_HEADS, hd, tm).astype(BF16)
    vt_ref[0, :, 0, hd:, :] = jnp.ones((DA_HEADS, ONES_ROWS, tm), BF16)
    mq_ref[...] = u[:, nqk + TOK_W:].astype(BF16)


def _inproj0(x, g, w, qkg, b, tm=512):
    t = x.shape[0]
    n = w.shape[1]
    s = t // b
    nt = s // tm
    hd = 2 * DA_DH
    return pl.pallas_call(
        _inproj0_kernel,
        out_shape=[jax.ShapeDtypeStruct((b, DA_HEADS, hd, s), BF16),
                   jax.ShapeDtypeStruct((t, TOK_W), BF16),
                   jax.ShapeDtypeStruct((b, DA_HEADS, nt, hd + ONES_ROWS, tm), BF16),
                   jax.ShapeDtypeStruct((t, MEM_W), BF16)],
        grid=(t // tm,),
        in_specs=[pl.BlockSpec((tm, D_MODEL), lambda i: (i, 0)),
                  pl.BlockSpec((1, D_MODEL), lambda i: (0, 0)),
                  pl.BlockSpec((D_MODEL, n), lambda i: (0, 0)),
                  pl.BlockSpec((1, 2 * TOK_W), lambda i: (0, 0))],
        out_specs=[pl.BlockSpec((1, DA_HEADS, hd, tm), lambda i: (i // nt, 0, 0, i % nt)),
                   pl.BlockSpec((tm, TOK_W), lambda i: (i, 0)),
                   pl.BlockSpec((1, DA_HEADS, 1, hd + ONES_ROWS, tm), lambda i: (i // nt, 0, i % nt, 0, 0)),
                   pl.BlockSpec((tm, MEM_W), lambda i: (i, 0))],
        compiler_params=_cparams(("parallel",)),
        name="inproj0",
    )(x, g, w, qkg)


def _attn_kernel(*refs, tq, tk, nh, rider_chunks, lambda_init):
    n_riders = len(rider_chunks)
    lq1_ref, lk1_ref, lq2_ref, lk2_ref, subg_ref, qt_ref, k_ref, vt_ref = refs[:8]
    rider_in = refs[8:8 + n_riders]
    o_ref = refs[8 + n_riders]
    rider_out = refs[9 + n_riders:9 + 2 * n_riders]
    acc_ref, m_ref, alpha_ref, s_ref, p_ref = refs[9 + 2 * n_riders:]
    step = (pl.program_id(0) * pl.num_programs(1) + pl.program_id(1)) * pl.num_programs(2) + pl.program_id(2)
    for src, dst, chunks in zip(rider_in, rider_out, rider_chunks):
        @pl.when(step < chunks)
        def _(src=src, dst=dst):
            dst[...] = src[...].astype(dst.dtype)

    qi = pl.program_id(2)
    hd = 2 * DA_DH
    qs = []
    for hs in range(nh):
        qt = qt_ref[0, hs]
        row = lax.broadcasted_iota(jnp.int32, qt.shape, 0)
        zero = jnp.zeros_like(qt)
        qs += [jnp.where(row < DA_DH, qt, zero), jnp.where(row >= DA_DH, qt, zero)]
    acc_ref[...] = jnp.zeros_like(acc_ref)
    m_ref[...] = jnp.full_like(m_ref, NEG)
    p_ref[1] = jnp.zeros(p_ref.shape[1:], p_ref.dtype)
    alpha_ref[1] = jnp.ones(alpha_ref.shape[1:], alpha_ref.dtype)

    def scores(ki, slot, c):
        start = pl.multiple_of(ki * tk, tk)
        hs = c // 2
        k = k_ref[0, pl.ds(start, tk), hs * hd:(hs + 1) * hd]
        s_ref[slot, c] = jnp.dot(k, qs[c], preferred_element_type=F32)

    def softmax(slot, masked, c):
        s = s_ref[slot, c]
        if masked:
            kc = lax.broadcasted_iota(jnp.int32, (tk, tq), 0) >> 6
            qc = lax.broadcasted_iota(jnp.int32, (tk, tq), 1) >> 6
            s = jnp.where(kc <= qc, s, NEG)
        m_old = m_ref[c]
        m_new = jnp.maximum(m_old, jnp.max(s, axis=0, keepdims=True))
        alpha_ref[slot, c] = jnp.exp2(m_old - m_new)
        p_ref[slot, c] = jnp.exp2(s - m_new).astype(BF16)
        m_ref[c] = m_new

    def values(ki, slot, c):
        vt = vt_ref[0, c // 2, ki]
        acc_ref[c] = alpha_ref[slot, c] * acc_ref[c] + jnp.dot(vt, p_ref[slot, c], preferred_element_type=F32)

    groups = [(2 * hs, 2 * hs + 1) for hs in range(nh)]
    odd = (qi & 1) == 1

    def stage(fn, g, *args):
        for c in g:
            fn(*args, c)

    @pl.when(odd)
    def _():
        for g in groups:
            stage(scores, g, 0, 1)
            stage(scores, g, 1, 0)
            stage(softmax, g, 1, False)

    @pl.when(jnp.logical_not(odd))
    def _():
        for g in groups:
            stage(scores, g, 0, 0)

    def body(j, carry):
        ki = (qi & 1) + 2 * j
        for g in groups:
            stage(scores, g, ki + 1, 1)
            stage(softmax, g, 0, False)
            stage(values, g, jnp.maximum(ki - 1, 0), 1)
            stage(scores, g, ki + 2, 0)
            stage(softmax, g, 1, False)
            stage(values, g, ki, 0)
        return carry

    lax.fori_loop(0, qi >> 1, body, 0)
    for g in groups:
        stage(softmax, g, 0, True)
        stage(values, g, jnp.maximum(qi - 1, 0), 1)
        stage(values, g, qi, 0)

    lam = (jnp.exp(jnp.sum(lq1_ref[...] * lk1_ref[...])) - jnp.exp(jnp.sum(lq2_ref[...] * lk2_ref[...]))
           + lambda_init)
    for hs in range(nh):
        o1 = acc_ref[2 * hs, :hd, :] / acc_ref[2 * hs, hd:hd + 1, :]
        o2 = acc_ref[2 * hs + 1, :hd, :] / acc_ref[2 * hs + 1, hd:hd + 1, :]
        ot = o1 - lam * o2
        ms = jnp.mean(ot * ot, axis=0, keepdims=True)
        ot = ot * lax.rsqrt(ms + EPS)
        o_ref[0, :, hs * hd:(hs + 1) * hd] = (ot.T * (subg_ref[...] * (1.0 - lambda_init))).astype(o_ref.dtype)


def _diff_attention(qt, k3, vt, lq1, lk1, lq2, lk2, subg, lambda_init, riders=(), nh=2):
    b, s, _ = k3.shape
    tk = vt.shape[-1]
    tq = tk
    nq = s // tq
    ng = DA_HEADS // nh
    hd = 2 * DA_DH
    hde = hd + ONES_ROWS
    nc = 2 * nh
    vec = lambda n: pl.BlockSpec((1, n), lambda bi, h, qi: (0, 0))

    def rider_spec(a):
        assert a.shape[0] <= b * ng * nq
        last = a.shape[0] - 1
        return pl.BlockSpec((1,) + a.shape[1:], lambda bi, h, qi: (jnp.minimum((bi * ng + h) * nq + qi, last), 0, 0))

    kern = functools.partial(_attn_kernel, tq=tq, tk=tk, nh=nh, rider_chunks=tuple(a.shape[0] for a in riders),
                             lambda_init=lambda_init)
    res = pl.pallas_call(
        kern,
        out_shape=[jax.ShapeDtypeStruct((b, s, TOK_W), BF16)] + [jax.ShapeDtypeStruct(a.shape, BF16) for a in riders],
        grid=(b, ng, nq),
        in_specs=[vec(DA_DH), vec(DA_DH), vec(DA_DH), vec(DA_DH), vec(hd),
                  pl.BlockSpec((1, nh, hd, tq), lambda bi, h, qi: (bi, h, 0, qi)),
                  pl.BlockSpec((1, s, nh * hd), lambda bi, h, qi: (bi, 0, h)),
                  pl.BlockSpec((1, nh, s // tk, hde, tk), lambda bi, h, qi: (bi, h, 0, 0, 0))]
                 + [rider_spec(a) for a in riders],
        out_specs=[pl.BlockSpec((1, tq, nh * hd), lambda bi, h, qi: (bi, qi, h))] + [rider_spec(a) for a in riders],
        scratch_shapes=[pltpu.VMEM((nc, hde, tq), F32), pltpu.VMEM((nc, 1, tq), F32),
                        pltpu.VMEM((2, nc, 1, tq), F32), pltpu.VMEM((2, nc, tk, tq), F32),
                        pltpu.VMEM((2, nc, tk, tq), BF16)],
        compiler_params=_cparams(("arbitrary", "arbitrary", "arbitrary")),
        name="diff_attn",
    )(lq1, lk1, lq2, lk2, subg, qt, k3, vt, *riders)
    return res[0], res[1:]


def _memkv_kernel(mem_ref, g_ref, wkv_ref, kng_ref, kv_ref):
    mn = _rms(mem_ref[0], g_ref[...]).astype(BF16)
    kv = jnp.dot(mn, wkv_ref[0], preferred_element_type=F32)
    k = kv[:, :MEM_W]
    kn = k * lax.rsqrt(_seg_mean_sq(k) + EPS) * kng_ref[0]
    kv_ref[0, 0, :, :MEM_W] = kn.astype(BF16)
    kv_ref[0, 0, :, MEM_W:] = kv[:, MEM_W:].astype(BF16)


def _memkv(mem, g, wkv, kng):
    depth = wkv.shape[0]
    b = mem.shape[0]
    return pl.pallas_call(
        _memkv_kernel,
        out_shape=jax.ShapeDtypeStruct((depth, b, MEM_LEN, 2 * MEM_W), BF16),
        grid=(depth, b),
        in_specs=[pl.BlockSpec((1, MEM_LEN, D_MODEL), lambda d, bi: (bi, 0, 0)),
                  pl.BlockSpec((1, D_MODEL), lambda d, bi: (0, 0)),
                  pl.BlockSpec((1, D_MODEL, 2 * MEM_W), lambda d, bi: (d, 0, 0)),
                  pl.BlockSpec((1, 1, MEM_W), lambda d, bi: (d, 0, 0))],
        out_specs=pl.BlockSpec((1, 1, MEM_LEN, 2 * MEM_W), lambda d, bi: (d, bi, 0, 0)),
        compiler_params=_cparams(("parallel", "parallel")),
        name="memkv",
    )(mem, g, wkv, kng)


def _mem_attention(mq, qg, kv):
    qn = (mq * lax.rsqrt(_seg_mean_sq(mq) + EPS) * qg).astype(BF16)
    k = kv[:, :MEM_W]
    v = kv[:, MEM_W:]
    lane = lax.broadcasted_iota(jnp.int32, qn.shape, 1) >> 6
    out = jnp.zeros(qn.shape, F32)
    for h in range(MEM_HEADS):
        sel = lane == h
        qh = jnp.where(sel, qn, jnp.zeros_like(qn))
        s = lax.dot_general(qh, k, (((1,), (1,)), ((), ())), preferred_element_type=F32)
        p = jnp.exp(s - jnp.max(s, axis=1, keepdims=True))
        l = jnp.sum(p, axis=1, keepdims=True)
        o = jnp.dot(p.astype(BF16), v, preferred_element_type=F32)
        out = jnp.where(sel, o / l, out)
    return out


R_E1, R_E2, R_G1, R_G2, R_RANK1, R_RANK2 = range(6)


def _top2_route(logits, count_ref):
    assert N_EXPERTS == SUBLANES
    tm = logits.shape[0]
    lt = logits.T[:SUBLANES, :]
    row = lax.broadcasted_iota(jnp.int32, lt.shape, 0)
    m1 = jnp.max(lt, axis=0, keepdims=True)
    i1 = jnp.min(jnp.where(lt == m1, row, N_EXPERTS), axis=0, keepdims=True)
    lt2 = jnp.where(row == i1, NEG, lt)
    m2 = jnp.max(lt2, axis=0, keepdims=True)
    i2 = jnp.min(jnp.where(lt2 == m2, row, N_EXPERTS), axis=0, keepdims=True)
    g1 = 1.0 / (1.0 + jnp.exp(m2 - m1))
    g2 = 1.0 - g1
    chosen = jnp.logical_or(row == i1, row == i2)
    onehot = jnp.where(chosen, 1.0, 0.0)
    r = lax.broadcasted_iota(jnp.int32, (tm, tm), 0)
    c = lax.broadcasted_iota(jnp.int32, (tm, tm), 1)
    earlier = jnp.where(r < c, 1.0, 0.0).astype(BF16)
    onehot16 = jnp.concatenate([onehot, jnp.zeros_like(onehot)], axis=0).astype(BF16)
    prefix = jnp.dot(onehot16, earlier, preferred_element_type=F32)[:SUBLANES] + count_ref[:, :1]
    count_ref[...] += jnp.sum(onehot, axis=1, keepdims=True)
    rank1 = jnp.sum(jnp.where(row == i1, prefix, 0.0), axis=0, keepdims=True)
    rank2 = jnp.sum(jnp.where(row == i2, prefix, 0.0), axis=0, keepdims=True)
    rec = jnp.zeros(lt.shape, F32)
    for idx, val in ((R_E1, i1.astype(F32)), (R_E2, i2.astype(F32)), (R_G1, g1), (R_G2, g2),
                     (R_RANK1, rank1), (R_RANK2, rank2)):
        rec = jnp.where(row == idx, val, rec)
    return rec


def _mixout_kernel(*refs, with_router):
    if with_router:
        (tok_ref, mq_ref, kv_ref, qg_ref, wo_ref, x_ref, g2_ref, wr_ref,
         x1_ref, hp_ref, route_ref, idx_ref, count_ref) = refs
    else:
        tok_ref, mq_ref, kv_ref, qg_ref, wo_ref, x_ref, g2_ref, x1_ref, h2_ref = refs
    mo = _mem_attention(mq_ref[...].astype(F32), qg_ref[...], kv_ref[0])
    y = jnp.dot(tok_ref[...], wo_ref[:TOK_W, :], preferred_element_type=F32)
    y = y + jnp.dot(mo.astype(BF16), wo_ref[TOK_W:, :], preferred_element_type=F32)
    x1 = x_ref[...] + y
    x1_ref[...] = x1
    h2 = _rms(x1, g2_ref[...])
    if with_router:
        @pl.when(pl.program_id(0) == 0)
        def _():
            count_ref[...] = jnp.zeros_like(count_ref)

        hp_ref[...] = h2
        wr = wr_ref[...]
        w_hi = wr.astype(BF16)
        w_lo = (wr - w_hi.astype(F32)).astype(BF16)
        h_hi = h2.astype(BF16)
        h_lo = (h2 - h_hi.astype(F32)).astype(BF16)
        both = jnp.dot(h_hi, jnp.concatenate([w_hi, w_lo], axis=1), preferred_element_type=F32)
        logits = both[:, :LANES] + both[:, LANES:] + jnp.dot(h_lo, w_hi, preferred_element_type=F32)
        rec = _top2_route(logits, count_ref)
        idx_ref[0] = rec.astype(jnp.int32)
        route_ref[...] = jnp.concatenate([rec, jnp.zeros((LANES - SUBLANES, rec.shape[1]), F32)], axis=0).T
    else:
        h2_ref[...] = h2.astype(BF16)


def _mixout(tok, mq_src, mq_block, kv, qg, wo, x, g2, wr=None, tm=512):
    t = x.shape[0]
    tiles_per_batch = t // kv.shape[0] // tm
    with_router = wr is not None
    in_specs = [pl.BlockSpec((tm, TOK_W), lambda i: (i, 0)),
                pl.BlockSpec((tm, MEM_W), lambda i: (i, mq_block)),
                pl.BlockSpec((1, MEM_LEN, 2 * MEM_W), lambda i: (i // tiles_per_batch, 0, 0)),
                pl.BlockSpec((1, MEM_W), lambda i: (0, 0)),
                pl.BlockSpec((D_MODEL, D_MODEL), lambda i: (0, 0)),
                pl.BlockSpec((tm, D_MODEL), lambda i: (i, 0)),
                pl.BlockSpec((1, D_MODEL), lambda i: (0, 0))]
    args = [tok, mq_src, kv, qg, wo, x, g2]
    row_spec = lambda n: pl.BlockSpec((tm, n), lambda i: (i, 0))
    if with_router:
        in_specs.append(pl.BlockSpec((D_MODEL, LANES), lambda i: (0, 0)))
        args.append(wr)
        out_shape = [jax.ShapeDtypeStruct((t, D_MODEL), F32), jax.ShapeDtypeStruct((t, D_MODEL), F32),
                     jax.ShapeDtypeStruct((t, LANES), F32), jax.ShapeDtypeStruct((t // tm, SUBLANES, tm), jnp.int32),
                     jax.ShapeDtypeStruct((SUBLANES, LANES), F32)]
        out_specs = [row_spec(D_MODEL), row_spec(D_MODEL), row_spec(LANES),
                     pl.BlockSpec((1, SUBLANES, tm), lambda i: (i, 0, 0)),
                     pl.BlockSpec((SUBLANES, LANES), lambda i: (0, 0))]
    else:
        out_shape = [jax.ShapeDtypeStruct((t, D_MODEL), F32), jax.ShapeDtypeStruct((t, D_MODEL), BF16)]
        out_specs = [row_spec(D_MODEL), row_spec(D_MODEL)]
    return pl.pallas_call(
        functools.partial(_mixout_kernel, with_router=with_router),
        out_shape=out_shape,
        grid=(t // tm,),
        in_specs=in_specs,
        out_specs=out_specs,
        compiler_params=_cparams(("arbitrary",) if with_router else ("parallel",)),
        name="mixout_router" if with_router else "mixout",
    )(*args)


def _ffn_kernel(h_ref, x_ref, wg_ref, wu_ref, wd_ref, gn_ref, xo_ref, ho_ref, acc_ref):
    f = pl.program_id(1)
    h = h_ref[...]
    a = jnp.dot(h, wg_ref[...], preferred_element_type=F32)
    u = jnp.dot(h, wu_ref[...], preferred_element_type=F32)
    act = a * jax.nn.sigmoid(a) * u
    contrib = jnp.dot(act.astype(BF16), wd_ref[...], preferred_element_type=F32)
    acc_ref[...] = jnp.where(f == 0, x_ref[...], acc_ref[...]) + contrib

    @pl.when(f == pl.num_programs(1) - 1)
    def _():
        xn = acc_ref[...]
        xo_ref[...] = xn
        ho_ref[...] = _rms(xn, gn_ref[...]).astype(BF16)


def _ffn(h, x, wg, wu, wd, gn, tm=512, tf=1408):
    t = x.shape[0]
    fdim = wg.shape[1]
    row = pl.BlockSpec((tm, D_MODEL), lambda i, f: (i, 0))
    return pl.pallas_call(
        _ffn_kernel,
        out_shape=[jax.ShapeDtypeStruct((t, D_MODEL), F32), jax.ShapeDtypeStruct((t, D_MODEL), BF16)],
        grid=(t // tm, fdim // tf),
        in_specs=[row, row,
                  pl.BlockSpec((D_MODEL, tf), lambda i, f: (0, f)),
                  pl.BlockSpec((D_MODEL, tf), lambda i, f: (0, f)),
                  pl.BlockSpec((tf, D_MODEL), lambda i, f: (f, 0)),
                  pl.BlockSpec((1, D_MODEL), lambda i, f: (0, 0))],
        out_specs=[row, row],
        scratch_shapes=[pltpu.VMEM((tm, D_MODEL), F32)],
        compiler_params=_cparams(("parallel", "arbitrary")),
        name="ffn",
    )(h, x, wg, wu, wd, gn)


def _row_copy(src_ref, src_row, dst_ref, dst_row, sem):
    return pltpu.make_async_copy(src_ref.at[pl.ds(src_row, 1), :], dst_ref.at[pl.ds(dst_row, 1), :], sem)


def _dispatch_kernel(pad_start_ref, pad_len_ref, pos_ref, hp_ref, xs_ref, zeros_ref, sem, zsem, *, tm, tmr):
    nbits = tmr.bit_length() - 1

    @pl.when(pl.program_id(0) == 0)
    def _():
        zeros_ref[...] = jnp.zeros_like(zeros_ref)
        for e in range(N_EXPERTS):
            start = pad_start_ref[e]
            length = pad_len_ref[e]
            singles = length & (SUBLANES - 1)
            for j in range(SUBLANES - 1):
                @pl.when(j < singles)
                def _(j=j, start=start):
                    cp = _row_copy(zeros_ref, 0, xs_ref, start + j, zsem)
                    cp.start()
                    cp.wait()

            done = start + singles
            for b in range(SUBLANES.bit_length() - 1, nbits):
                n = 1 << b
                bit = (length >> b) & 1

                @pl.when(bit == 1)
                def _(n=n, done=done):
                    cp = pltpu.make_async_copy(zeros_ref.at[pl.ds(0, n), :],
                                               xs_ref.at[pl.ds(pl.multiple_of(done, SUBLANES), n), :], zsem)
                    cp.start()
                    cp.wait()

                done = done + bit * n

        n_tiles = xs_ref.shape[0] // tmr
        half = tmr // 2
        for j in range(n_tiles - N_EXPERTS, n_tiles):
            @pl.when(j * tmr >= pad_start_ref[N_EXPERTS])
            def _(j=j):
                for c in range(2):
                    cp = pltpu.make_async_copy(zeros_ref, xs_ref.at[pl.ds(j * tmr + c * half, half), :], zsem)
                    cp.start()
                    cp.wait()

    def issue(r, carry):
        _row_copy(hp_ref, r, xs_ref, pos_ref[0, 0, r], sem).start(priority=0)
        _row_copy(hp_ref, r, xs_ref, pos_ref[0, 1, r], sem).start(priority=1)
        return carry

    lax.fori_loop(0, tm, issue, 0, unroll=ISSUE_UNROLL)
    for _ in range(2):
        pltpu.make_async_copy(hp_ref, xs_ref.at[pl.ds(0, tm), :], sem).wait()


def _dispatch(pad_start, pad_len, pos, hp, n_rows, tm, tmr):
    t, w = hp.shape
    return pl.pallas_call(
        functools.partial(_dispatch_kernel, tm=tm, tmr=tmr),
        out_shape=jax.ShapeDtypeStruct((n_rows, w), hp.dtype),
        grid_spec=pltpu.PrefetchScalarGridSpec(
            num_scalar_prefetch=2,
            grid=(t // tm,),
            in_specs=[pl.BlockSpec((1, 2, tm), lambda i, ps, pn: (i, 0, 0), memory_space=pltpu.SMEM),
                      pl.BlockSpec((tm, w), lambda i, ps, pn: (i, 0))],
            out_specs=pl.BlockSpec(memory_space=pl.ANY),
            scratch_shapes=[pltpu.VMEM((tmr // 2, w), hp.dtype), pltpu.SemaphoreType.DMA,
                            pltpu.SemaphoreType.DMA]),
        compiler_params=_cparams(("arbitrary",)),
        name="moe_dispatch",
    )(pad_start, pad_len, pos, hp)


def _experts_kernel(te_ref, tv_ref, xs_ref, wg_ref, wu_ref, wd_ref, y_ref):
    i = pl.program_id(0)
    f = pl.program_id(1)

    @pl.when(tv_ref[i] == 1)
    def _():
        h = xs_ref[...].astype(BF16)
        a = jnp.dot(h, wg_ref[0], preferred_element_type=F32)
        u = jnp.dot(h, wu_ref[0], preferred_element_type=F32)
        act = a * jax.nn.sigmoid(a) * u
        contrib = jnp.dot(act.astype(BF16), wd_ref[0], preferred_element_type=F32)
        y_ref[...] = jnp.where(f == 0, 0.0, y_ref[...]) + contrib

    @pl.when(jnp.logical_and(tv_ref[i] == 0, f == 0))
    def _():
        y_ref[...] = jnp.zeros_like(y_ref)


def _experts(tile_expert, tile_valid, xs, wg, wu, wd, tmr, tf):
    n_rows, w = xs.shape
    fdim = wg.shape[2]
    nf = fdim // tf
    fidx = lambda i, f, te, tv: jnp.where(tv[i] == 1, f, nf - 1)
    return pl.pallas_call(
        _experts_kernel,
        out_shape=jax.ShapeDtypeStruct((n_rows, D_MODEL), F32),
        grid_spec=pltpu.PrefetchScalarGridSpec(
            num_scalar_prefetch=2,
            grid=(n_rows // tmr, nf),
            in_specs=[pl.BlockSpec((tmr, w), lambda i, f, te, tv: (jnp.where(tv[i] == 1, i, 0), 0)),
                      pl.BlockSpec((1, D_MODEL, tf), lambda i, f, te, tv: (te[i], 0, fidx(i, f, te, tv))),
                      pl.BlockSpec((1, D_MODEL, tf), lambda i, f, te, tv: (te[i], 0, fidx(i, f, te, tv))),
                      pl.BlockSpec((1, tf, D_MODEL), lambda i, f, te, tv: (te[i], fidx(i, f, te, tv), 0))],
            out_specs=pl.BlockSpec((tmr, D_MODEL), lambda i, f, te, tv: (i, 0))),
        compiler_params=_cparams(("parallel", "arbitrary")),
        name="moe_experts",
    )(tile_expert, tile_valid, xs, wg, wu, wd)


def _combine_kernel(pos_ref, route_ref, x_ref, y_ref, o_ref, buf_ref, sem, *, tm):
    def issue(r, carry):
        _row_copy(y_ref, pos_ref[0, 0, r], buf_ref.at[0], r, sem).start(priority=0)
        _row_copy(y_ref, pos_ref[0, 1, r], buf_ref.at[1], r, sem).start(priority=1)
        return carry

    lax.fori_loop(0, tm, issue, 0, unroll=ISSUE_UNROLL)
    for k in range(2):
        pltpu.make_async_copy(y_ref.at[pl.ds(0, tm), :], buf_ref.at[k], sem).wait()
    route = route_ref[...]
    g1 = route[:, R_G1:R_G1 + 1]
    g2 = route[:, R_G2:R_G2 + 1]
    o_ref[...] = x_ref[...] + (g1 * buf_ref[0] + g2 * buf_ref[1])


def _combine(pos, route, x, y, tm):
    t = x.shape[0]
    return pl.pallas_call(
        functools.partial(_combine_kernel, tm=tm),
        out_shape=jax.ShapeDtypeStruct((t, D_MODEL), F32),
        grid=(t // tm,),
        in_specs=[pl.BlockSpec((1, 2, tm), lambda i: (i, 0, 0), memory_space=pltpu.SMEM),
                  pl.BlockSpec((tm, LANES), lambda i: (i, 0)),
                  pl.BlockSpec((tm, D_MODEL), lambda i: (i, 0)),
                  pl.BlockSpec(memory_space=pl.ANY)],
        out_specs=pl.BlockSpec((tm, D_MODEL), lambda i: (i, 0)),
        scratch_shapes=[pltpu.VMEM((2, tm, D_MODEL), F32), pltpu.SemaphoreType.DMA],
        compiler_params=_cparams(("arbitrary",)),
        name="moe_combine",
    )(pos, route, x, y)


def _moe(hp, route, idx, counts, x, wg, wu, wd, tmr=512, tf=1792):
    t = hp.shape[0]
    tm = idx.shape[2]
    n_tiles = 2 * t // tmr + N_EXPERTS
    n_rows = n_tiles * tmr
    cnt = counts[:, 0].astype(jnp.int32)
    padded = (cnt + tmr - 1) // tmr * tmr
    ends = jnp.cumsum(padded)
    offs = ends - padded
    def sorted_rows(e, rank):
        off = jnp.zeros_like(e)
        for k in range(N_EXPERTS):
            off = jnp.where(e == k, offs[k], off)
        return off + rank
    pos = jnp.stack([sorted_rows(idx[:, R_E1, :], idx[:, R_RANK1, :]),
                     sorted_rows(idx[:, R_E2, :], idx[:, R_RANK2, :])], axis=1)
    pos = pos.reshape(-1, COPY_TILES, 2, tm).transpose(0, 2, 1, 3).reshape(-1, 2, COPY_TILES * tm)
    tm = COPY_TILES * tm
    tile_start = jnp.arange(n_tiles, dtype=jnp.int32) * tmr
    tile_valid = (tile_start < ends[-1]).astype(jnp.int32)
    last_tile = ends[-1] // tmr - 1
    clamped = jnp.minimum(tile_start, last_tile * tmr)
    tile_expert = jnp.sum((clamped[:, None] >= ends[None, :]).astype(jnp.int32), axis=1)
    pad_start = jnp.concatenate([offs + cnt, ends[-1:]])
    xs = _dispatch(pad_start, padded - cnt, pos, hp, n_rows, tm, tmr)
    y = _experts(tile_expert, tile_valid, xs, wg, wu, wd, tmr, tf)
    return _combine(pos, route, x, y, tm)


def _inproj1_kernel(x_ref, w_ref, z_ref, xbc_ref, mq_ref, dt_ref):
    h = x_ref[...]
    c1 = TOK_W
    c2 = c1 + SSM_CONV_DIM
    c3 = c2 + MEM_W
    z_ref[...] = jnp.dot(h, w_ref[:, :c1], preferred_element_type=F32).astype(BF16)
    xbc_ref[...] = jnp.dot(h, w_ref[:, c1:c2], preferred_element_type=F32).astype(BF16)
    mq_ref[...] = jnp.dot(h, w_ref[:, c2:c3], preferred_element_type=F32).astype(BF16)
    dt_ref[...] = jnp.dot(h, w_ref[:, c3:], preferred_element_type=F32)


def _inproj1(h, w, tm=512):
    t = h.shape[0]
    row = lambda n: pl.BlockSpec((tm, n), lambda i: (i, 0))
    return pl.pallas_call(
        _inproj1_kernel,
        out_shape=[jax.ShapeDtypeStruct((t, TOK_W), BF16), jax.ShapeDtypeStruct((t, SSM_CONV_DIM), BF16),
                   jax.ShapeDtypeStruct((t, MEM_W), BF16), jax.ShapeDtypeStruct((t, LANES), F32)],
        grid=(t // tm,),
        in_specs=[row(D_MODEL), pl.BlockSpec(w.shape, lambda i: (0, 0))],
        out_specs=[row(TOK_W), row(SSM_CONV_DIM), row(MEM_W), row(LANES)],
        compiler_params=_cparams(("parallel",)),
        name="inproj1",
    )(h, w)


def _split_dot(a, b, terms, split_rhs):
    rem = b if split_rhs else a
    pieces = []
    for _ in range(terms):
        piece = rem.astype(BF16)
        pieces.append(piece)
        rem = rem - piece.astype(F32)
    if split_rhs:
        n = b.shape[1]
        wide = jnp.dot(a, jnp.concatenate(pieces, axis=1), preferred_element_type=F32)
        out = wide[:, :n]
        for t in range(1, terms):
            out = out + wide[:, t * n:(t + 1) * n]
        return out
    return jnp.dot(jnp.concatenate(pieces, axis=1), jnp.concatenate([b] * terms, axis=0),
                   preferred_element_type=F32)


def _ssd_kernel(z_ref, xbc_ref, dt_ref, cw_ref, cb_ref, dtb_ref, alog_ref, dsk_ref, ng_ref, ex_ref,
                o_ref, xe_ref, st_ref, *, tl):
    j = pl.program_id(1)
    pad = 8

    @pl.when(j == 0)
    def _():
        xe_ref[0:pad, :] = jnp.zeros((pad, SSM_CONV_DIM), F32)
        st_ref[...] = jnp.zeros_like(st_ref)

    x = xbc_ref[0].astype(F32)
    xe_ref[pad:pad + tl, :] = x
    acc = cb_ref[...] + cw_ref[SSM_CONV_K - 1:SSM_CONV_K, :] * x
    for k in range(SSM_CONV_K - 1):
        sh = SSM_CONV_K - 1 - k
        acc = acc + cw_ref[k:k + 1, :] * xe_ref[pad - sh:pad - sh + tl, :]
    xe_ref[0:pad, :] = x[tl - pad:tl, :]
    xc = acc * jax.nn.sigmoid(acc)
    xs = xc[:, :TOK_W]

    dt_in = dt_ref[0] + dtb_ref[...]
    dt = jnp.maximum(dt_in, 0.0) + jnp.log1p(jnp.exp(-jnp.abs(dt_in)))
    a = -jnp.exp(alog_ref[...])
    dta = dt * a
    r = lax.broadcasted_iota(jnp.int32, (tl, tl), 0)
    c = lax.broadcasted_iota(jnp.int32, (tl, tl), 1)
    tril = r >= c
    ltri = jnp.where(tril, 1.0, 0.0).astype(BF16)
    cum = _split_dot(ltri, dta, terms=3, split_rhs=True)
    cum_t = cum.T
    dt_t = dt.T
    lane = lax.broadcasted_iota(jnp.int32, (tl, LANES), 1)

    hg = SSM_HEADS // SSM_GROUPS
    y_pairs = []
    cbs = []
    for g in range(SSM_GROUPS):
        bm = xc[:, TOK_W + g * SSM_STATE:TOK_W + (g + 1) * SSM_STATE].astype(BF16)
        cm = xc[:, TOK_W + (SSM_GROUPS + g) * SSM_STATE:TOK_W + (SSM_GROUPS + g + 1) * SSM_STATE].astype(BF16)
        cbs.append(lax.dot_general(cm, bm, (((1,), (1,)), ((), ())), preferred_element_type=F32))
    for pr in range(SSM_HEADS // 2):
        xp = xs[:, pr * LANES:(pr + 1) * LANES].astype(BF16)
        ys = []
        for hh in (2 * pr, 2 * pr + 1):
            g = hh // hg
            seg = cum[:, hh:hh + 1] - cum_t[hh:hh + 1, :]
            decay = jnp.exp(jnp.where(tril, seg, NEG))
            w = cbs[g] * decay * dt_t[hh:hh + 1, :]
            ys.append(jnp.dot(w.astype(BF16), xp, preferred_element_type=F32))
        y_pairs.append(jnp.where(lane < SSM_HD, ys[0], ys[1]))
    y = jnp.concatenate(y_pairs, axis=1)

    expcum = jnp.exp(cum)
    to_end = jnp.exp(cum[tl - 1:tl, :] - cum) * dt
    stacked = jnp.concatenate([expcum, to_end], axis=0)
    exd = _split_dot(stacked, ex_ref[...], terms=2, split_rhs=False)
    expcum_x = exd[:tl]
    xw = (xs * exd[tl:]).astype(BF16)
    gw = TOK_W // SSM_GROUPS
    y_off = []
    for g in range(SSM_GROUPS):
        cs = slice(g * gw, (g + 1) * gw)
        bm_t = xc[:, TOK_W + g * SSM_STATE:TOK_W + (g + 1) * SSM_STATE].T.astype(BF16)
        cm = xc[:, TOK_W + (SSM_GROUPS + g) * SSM_STATE:TOK_W + (SSM_GROUPS + g + 1) * SSM_STATE].astype(BF16)
        sg = st_ref[:, cs]
        y_off.append(jnp.dot(cm, sg.astype(BF16), preferred_element_type=F32) * expcum_x[:, cs])
        st_ref[:, cs] = sg * expcum_x[tl - 1:tl, cs] + jnp.dot(bm_t, xw[:, cs], preferred_element_type=F32)
    y = y + jnp.concatenate(y_off, axis=1) + dsk_ref[...] * xs
    zf = z_ref[0].astype(F32)
    y = y * (zf * jax.nn.sigmoid(zf))
    outs = []
    for g in range(SSM_GROUPS):
        cs = slice(g * gw, (g + 1) * gw)
        outs.append(_rms(y[:, cs], ng_ref[:, cs]))
    o_ref[0] = jnp.concatenate(outs, axis=1).astype(o_ref.dtype)


def _ssd(z3, xbc3, dt3, cw, cb, dtb, alog, dsk, ng, ex, tl=256):
    b, s, _ = z3.shape
    full = lambda a: pl.BlockSpec(a.shape, lambda bi, j: (0, 0))
    blk = lambda n: pl.BlockSpec((1, tl, n), lambda bi, j: (bi, j, 0))
    return pl.pallas_call(
        functools.partial(_ssd_kernel, tl=tl),
        out_shape=jax.ShapeDtypeStruct((b, s, TOK_W), BF16),
        grid=(b, s // tl),
        in_specs=[blk(TOK_W), blk(SSM_CONV_DIM), blk(LANES), full(cw), full(cb), full(dtb), full(alog),
                  full(dsk), full(ng), full(ex)],
        out_specs=blk(TOK_W),
        scratch_shapes=[pltpu.VMEM((tl + 8, SSM_CONV_DIM), F32), pltpu.VMEM((SSM_STATE, TOK_W), F32)],
        compiler_params=_cparams(("parallel", "arbitrary")),
        name="conv_ssd",
    )(z3, xbc3, dt3, cw, cb, dtb, alog, dsk, ng, ex)


def _pad_lanes(v, n=LANES):
    return jnp.pad(v, [(0, 0)] * (v.ndim - 1) + [(0, n - v.shape[-1])])


def kernel(x, mem, ln1_g, ln2_g, mem_norm_g, w_out, mem_w_kv, mem_qn_g, mem_kn_g, da_w_in, da_qn_g, da_kn_g,
           da_lq1, da_lk1, da_lq2, da_lk2, da_sub_g, ssm_w_in, ssm_conv_w, ssm_conv_b, ssm_dt_bias, ssm_a_log,
           ssm_d, ssm_norm_g, ffn_w_gate, ffn_w_up, ffn_w_down, moe_w_router, moe_w_gate, moe_w_up, moe_w_down):
    b, s, d = x.shape
    t = b * s
    row = lambda v: v.reshape(1, -1).astype(F32)
    xt = x.reshape(t, d)

    mem_qg = jnp.tile(mem_qn_g.astype(F32) * (MEM_HD ** -0.5), (1, MEM_HEADS))
    mem_kg = jnp.tile(mem_kn_g.astype(F32), (1, MEM_HEADS))[:, None, :]
    kv = _memkv(mem, row(mem_norm_g), mem_w_kv.astype(BF16), mem_kg)

    lambda_init = 0.8 - 0.6 * math.exp(-0.3 * 0)
    qkg = jnp.concatenate([jnp.tile(da_qn_g[0].astype(F32) * (DA_DH ** -0.5 * LOG2E), 2 * DA_HEADS),
                           jnp.tile(da_kn_g[0].astype(F32), 2 * DA_HEADS)]).reshape(1, -1)
    qt, kn, vt, mq0 = _inproj0(xt, row(ln1_g[0]), da_w_in[0].astype(BF16), qkg, b)
    late_w = ((moe_w_gate[0], CAST_CHUNKS), (moe_w_up[0], CAST_CHUNKS), (moe_w_down[0], CAST_CHUNKS),
              (ffn_w_gate[0], CAST_CHUNKS), (ffn_w_up[0], CAST_CHUNKS), (ffn_w_down[0], CAST_CHUNKS // 4),
              (w_out, CAST_CHUNKS))
    riders = tuple(w.reshape(n, -1, w.shape[-1]) for w, n in late_w)
    tok, w16 = _diff_attention(qt, kn.reshape(b, s, TOK_W), vt, row(da_lq1[0]), row(da_lk1[0]),
                               row(da_lq2[0]), row(da_lk2[0]), row(da_sub_g[0]), lambda_init, riders)
    moe_wg, moe_wu, moe_wd, ffn_wg, ffn_wu, ffn_wd, wo = (a.reshape(w.shape) for a, (w, _) in zip(w16, late_w))
    x1, h2 = _mixout(tok.reshape(t, TOK_W), mq0, 0, kv[0], mem_qg[0:1], wo[0], xt, row(ln2_g[0]))
    x2, h3 = _ffn(h2, x1, ffn_wg, ffn_wu, ffn_wd, row(ln1_g[1]))

    w_in = ssm_w_in[0]
    o2 = TOK_W + SSM_CONV_DIM
    o3 = o2 + SSM_HEADS
    w1 = jnp.concatenate([w_in[:, :o2], w_in[:, o3:], _pad_lanes(w_in[:, o2:o3])], axis=1).astype(BF16)
    z, xbc, mq, dt = _inproj1(h3, w1)
    expand = jnp.repeat(jnp.eye(SSM_HEADS, dtype=F32), SSM_HD, axis=1)
    expand = jnp.pad(expand, ((0, LANES - SSM_HEADS), (0, 0))).astype(BF16)
    dsk = jnp.repeat(ssm_d[0].astype(F32), SSM_HD).reshape(1, -1)
    tok1 = _ssd(z.reshape(b, s, -1), xbc.reshape(b, s, -1), dt.reshape(b, s, -1),
                ssm_conv_w[0].astype(F32), row(ssm_conv_b[0]), _pad_lanes(row(ssm_dt_bias[0])),
                _pad_lanes(row(ssm_a_log[0])), dsk, row(ssm_norm_g[0]), expand)
    x3, hp, route, idx, counts = _mixout(tok1.reshape(t, TOK_W), mq, 0, kv[1], mem_qg[1:2], wo[1], x2,
                                         row(ln2_g[1]), wr=_pad_lanes(moe_w_router[0].astype(F32)))
    x4 = _moe(hp, route, idx, counts, x3, moe_wg, moe_wu, moe_wd)
    return x4.reshape(b, s, d)
```

```python
import functools
import math

import jax
import jax.numpy as jnp
from jax import lax
from jax.experimental import pallas as pl
from jax.experimental.pallas import tpu as pltpu

F32 = jnp.float32
BF16 = jnp.bfloat16

D_MODEL = 1024
CHUNK = 64
MEM_LEN = 256
MEM_W = 256
MEM_HEADS = 4
MEM_HD = 64
TOK_W = 768
DA_DH = 64
DA_HEADS = 6
SSM_HD = 64
SSM_HEADS = 12
SSM_GROUPS = 2
SSM_STATE = 128
SSM_CONV_K = 4
SSM_CONV_DIM = 1280
N_EXPERTS = 8
EPS = 1e-6
LANES = 128
NEG = -1e30
SUBLANES = 8
ONES_ROWS = SUBLANES
LOG2E = math.log2(math.e)
COPY_TILES = 2
CAST_CHUNKS = 64
ISSUE_UNROLL = True
VMEM_LIMIT = 56 * 1024 * 1024


def _cparams(sem):
    return pltpu.CompilerParams(dimension_semantics=sem, vmem_limit_bytes=VMEM_LIMIT)


def _rms(xf, g):
    ms = jnp.mean(xf * xf, axis=-1, keepdims=True)
    return xf * lax.rsqrt(ms + EPS) * g


def _seg_mean_matrix(n, seg_shift):
    r = lax.broadcasted_iota(jnp.int32, (n, n), 0) >> seg_shift
    c = lax.broadcasted_iota(jnp.int32, (n, n), 1) >> seg_shift
    return jnp.where(r == c, 1.0 / (1 << seg_shift), 0.0).astype(BF16)


def _seg_mean_sq(y, seg_shift=6):
    bd = _seg_mean_matrix(256, seg_shift)
    sq = (y * y).astype(BF16)
    parts = [jnp.dot(sq[:, c:c + 256], bd, preferred_element_type=F32)
             for c in range(0, y.shape[1], 256)]
    return parts[0] if len(parts) == 1 else jnp.concatenate(parts, axis=1)


def _inproj0_kernel(x_ref, g_ref, w_ref, qkg_ref, qt_ref, k_ref, vt_ref, mq_ref):
    h = _rms(x_ref[...], g_ref[...]).astype(BF16)
    u = jnp.dot(h, w_ref[...], preferred_element_type=F32)
    tm = u.shape[0]
    hd = 2 * DA_DH
    nqk = 2 * TOK_W
    qk = u[:, :nqk]
    qkn = qk * lax.rsqrt(_seg_mean_sq(qk) + EPS) * qkg_ref[...]
    qt_ref[0] = qkn[:, :TOK_W].T.reshape(DA_HEADS, hd, tm).astype(BF16)
    k_ref[...] = qkn[:, TOK_W:].astype(BF16)
    vt_ref[0, :, 0, :hd, :] = u[:, nqk:nqk + TOK_W].T.reshape(DA_HEADS, hd, tm).astype(BF16)
    vt_ref[0, :, 0, hd:, :] = jnp.ones((DA_HEADS, ONES_ROWS, tm), BF16)
    mq_ref[...] = u[:, nqk + TOK_W:].astype(BF16)


def _inproj0(x, g, w, qkg, b, tm=512):
    t = x.shape[0]
    n = w.shape[1]
    s = t // b
    nt = s // tm
    hd = 2 * DA_DH
    return pl.pallas_call(
        _inproj0_kernel,
        out_shape=[jax.ShapeDtypeStruct((b, DA_HEADS, hd, s), BF16),
                   jax.ShapeDtypeStruct((t, TOK_W), BF16),
                   jax.ShapeDtypeStruct((b, DA_HEADS, nt, hd + ONES_ROWS, tm), BF16),
                   jax.ShapeDtypeStruct((t, MEM_W), BF16)],
        grid=(t // tm,),
        in_specs=[pl.BlockSpec((tm, D_MODEL), lambda i: (i, 0)),
                  pl.BlockSpec((1, D_MODEL), lambda i: (0, 0)),
                  pl.BlockSpec((D_MODEL, n), lambda i: (0, 0)),
                  pl.BlockSpec((1, 2 * TOK_W), lambda i: (0, 0))],
        out_specs=[pl.BlockSpec((1, DA_HEADS, hd, tm), lambda i: (i // nt, 0, 0, i % nt)),
                   pl.BlockSpec((tm, TOK_W), lambda i: (i, 0)),
                   pl.BlockSpec((1, DA_HEADS, 1, hd + ONES_ROWS, tm), lambda i: (i // nt, 0, i % nt, 0, 0)),
                   pl.BlockSpec((tm, MEM_W), lambda i: (i, 0))],
        compiler_params=_cparams(("parallel",)),
        name="inproj0",
    )(x, g, w, qkg)


def _attn_kernel(*refs, tq, tk, nh, rider_chunks, lambda_init):
    n_riders = len(rider_chunks)
    lq1_ref, lk1_ref, lq2_ref, lk2_ref, subg_ref, qt_ref, k_ref, vt_ref = refs[:8]
    rider_in = refs[8:8 + n_riders]
    o_ref = refs[8 + n_riders]
    rider_out = refs[9 + n_riders:9 + 2 * n_riders]
    acc_ref, m_ref, alpha_ref, s_ref, p_ref = refs[9 + 2 * n_riders:]
    step = (pl.program_id(0) * pl.num_programs(1) + pl.program_id(1)) * pl.num_programs(2) + pl.program_id(2)
    for src, dst, chunks in zip(rider_in, rider_out, rider_chunks):
        @pl.when(step < chunks)
        def _(src=src, dst=dst):
            dst[...] = src[...].astype(dst.dtype)

    qi = pl.program_id(2)
    hd = 2 * DA_DH
    qs = []
    for hs in range(nh):
        qt = qt_ref[0, hs]
        row = lax.broadcasted_iota(jnp.int32, qt.shape, 0)
        zero = jnp.zeros_like(qt)
        qs += [jnp.where(row < DA_DH, qt, zero), jnp.where(row >= DA_DH, qt, zero)]
    acc_ref[...] = jnp.zeros_like(acc_ref)
    m_ref[...] = jnp.full_like(m_ref, NEG)
    p_ref[1] = jnp.zeros(p_ref.shape[1:], p_ref.dtype)
    alpha_ref[1] = jnp.ones(alpha_ref.shape[1:], alpha_ref.dtype)

    def scores(ki, slot, c):
        start = pl.multiple_of(ki * tk, tk)
        hs = c // 2
        k = k_ref[0, pl.ds(start, tk), hs * hd:(hs + 1) * hd]
        s_ref[slot, c] = jnp.dot(k, qs[c], preferred_element_type=F32)

    def softmax(slot, masked, c):
        s = s_ref[slot, c]
        if masked:
            kc = lax.broadcasted_iota(jnp.int32, (tk, tq), 0) >> 6
            qc = lax.broadcasted_iota(jnp.int32, (tk, tq), 1) >> 6
            s = jnp.where(kc <= qc, s, NEG)
        m_old = m_ref[c]
        m_new = jnp.maximum(m_old, jnp.max(s, axis=0, keepdims=True))
        alpha_ref[slot, c] = jnp.exp2(m_old - m_new)
        p_ref[slot, c] = jnp.exp2(s - m_new).astype(BF16)
        m_ref[c] = m_new

    def values(ki, slot, c):
        vt = vt_ref[0, c // 2, ki]
        acc_ref[c] = alpha_ref[slot, c] * acc_ref[c] + jnp.dot(vt, p_ref[slot, c], preferred_element_type=F32)

    groups = [(2 * hs, 2 * hs + 1) for hs in range(nh)]
    odd = (qi & 1) == 1

    def stage(fn, g, *args):
        for c in g:
            fn(*args, c)

    @pl.when(odd)
    def _():
        for g in groups:
            stage(scores, g, 0, 1)
            stage(scores, g, 1, 0)
            stage(softmax, g, 1, False)

    @pl.when(jnp.logical_not(odd))
    def _():
        for g in groups:
            stage(scores, g, 0, 0)

    def body(j, carry):
        ki = (qi & 1) + 2 * j
        for g in groups:
            stage(scores, g, ki + 1, 1)
            stage(softmax, g, 0, False)
            stage(values, g, jnp.maximum(ki - 1, 0), 1)
            stage(scores, g, ki + 2, 0)
            stage(softmax, g, 1, False)
            stage(values, g, ki, 0)
        return carry

    lax.fori_loop(0, qi >> 1, body, 0)
    for g in groups:
        stage(softmax, g, 0, True)
        stage(values, g, jnp.maximum(qi - 1, 0), 1)
        stage(values, g, qi, 0)

    lam = (jnp.exp(jnp.sum(lq1_ref[...] * lk1_ref[...])) - jnp.exp(jnp.sum(lq2_ref[...] * lk2_ref[...]))
           + lambda_init)
    for hs in range(nh):
        o1 = acc_ref[2 * hs, :hd, :] / acc_ref[2 * hs, hd:hd + 1, :]
        o2 = acc_ref[2 * hs + 1, :hd, :] / acc_ref[2 * hs + 1, hd:hd + 1, :]
        ot = o1 - lam * o2
        ms = jnp.mean(ot * ot, axis=0, keepdims=True)
        ot = ot * lax.rsqrt(ms + EPS)
        o_ref[0, :, hs * hd:(hs + 1) * hd] = (ot.T * (subg_ref[...] * (1.0 - lambda_init))).astype(o_ref.dtype)


def _diff_attention(qt, k3, vt, lq1, lk1, lq2, lk2, subg, lambda_init, riders=(), nh=2):
    b, s, _ = k3.shape
    tk = vt.shape[-1]
    tq = tk
    nq = s // tq
    ng = DA_HEADS // nh
    hd = 2 * DA_DH
    hde = hd + ONES_ROWS
    nc = 2 * nh
    vec = lambda n: pl.BlockSpec((1, n), lambda bi, h, qi: (0, 0))

    def rider_spec(a):
        assert a.shape[0] <= b * ng * nq
        last = a.shape[0] - 1
        return pl.BlockSpec((1,) + a.shape[1:], lambda bi, h, qi: (jnp.minimum((bi * ng + h) * nq + qi, last), 0, 0))

    kern = functools.partial(_attn_kernel, tq=tq, tk=tk, nh=nh, rider_chunks=tuple(a.shape[0] for a in riders),
                             lambda_init=lambda_init)
    res = pl.pallas_call(
        kern,
        out_shape=[jax.ShapeDtypeStruct((b, s, TOK_W), BF16)] + [jax.ShapeDtypeStruct(a.shape, BF16) for a in riders],
        grid=(b, ng, nq),
        in_specs=[vec(DA_DH), vec(DA_DH), vec(DA_DH), vec(DA_DH), vec(hd),
                  pl.BlockSpec((1, nh, hd, tq), lambda bi, h, qi: (bi, h, 0, qi)),
                  pl.BlockSpec((1, s, nh * hd), lambda bi, h, qi: (bi, 0, h)),
                  pl.BlockSpec((1, nh, s // tk, hde, tk), lambda bi, h, qi: (bi, h, 0, 0, 0))]
                 + [rider_spec(a) for a in riders],
        out_specs=[pl.BlockSpec((1, tq, nh * hd), lambda bi, h, qi: (bi, qi, h))] + [rider_spec(a) for a in riders],
        scratch_shapes=[pltpu.VMEM((nc, hde, tq), F32), pltpu.VMEM((nc, 1, tq), F32),
                        pltpu.VMEM((2, nc, 1, tq), F32), pltpu.VMEM((2, nc, tk, tq), F32),
                        pltpu.VMEM((2, nc, tk, tq), BF16)],
        compiler_params=_cparams(("arbitrary", "arbitrary", "arbitrary")),
        name="diff_attn",
    )(lq1, lk1, lq2, lk2, subg, qt, k3, vt, *riders)
    return res[0], res[1:]


def _memkv_kernel(mem_ref, g_ref, wkv_ref, kng_ref, kv_ref):
    mn = _rms(mem_ref[0], g_ref[...]).astype(BF16)
    kv = jnp.dot(mn, wkv_ref[0], preferred_element_type=F32)
    k = kv[:, :MEM_W]
    kn = k * lax.rsqrt(_seg_mean_sq(k) + EPS) * kng_ref[0]
    kv_ref[0, 0, :, :MEM_W] = kn.astype(BF16)
    kv_ref[0, 0, :, MEM_W:] = kv[:, MEM_W:].astype(BF16)


def _memkv(mem, g, wkv, kng):
    depth = wkv.shape[0]
    b = mem.shape[0]
    return pl.pallas_call(
        _memkv_kernel,
        out_shape=jax.ShapeDtypeStruct((depth, b, MEM_LEN, 2 * MEM_W), BF16),
        grid=(depth, b),
        in_specs=[pl.BlockSpec((1, MEM_LEN, D_MODEL), lambda d, bi: (bi, 0, 0)),
                  pl.BlockSpec((1, D_MODEL), lambda d, bi: (0, 0)),
                  pl.BlockSpec((1, D_MODEL, 2 * MEM_W), lambda d, bi: (d, 0, 0)),
                  pl.BlockSpec((1, 1, MEM_W), lambda d, bi: (d, 0, 0))],
        out_specs=pl.BlockSpec((1, 1, MEM_LEN, 2 * MEM_W), lambda d, bi: (d, bi, 0, 0)),
        compiler_params=_cparams(("parallel", "parallel")),
        name="memkv",
    )(mem, g, wkv, kng)


def _mem_attention(mq, qg, kv):
    qn = (mq * lax.rsqrt(_seg_mean_sq(mq) + EPS) * qg).astype(BF16)
    k = kv[:, :MEM_W]
    v = kv[:, MEM_W:]
    lane = lax.broadcasted_iota(jnp.int32, qn.shape, 1) >> 6
    out = jnp.zeros(qn.shape, F32)
    for h in range(MEM_HEADS):
        sel = lane == h
        qh = jnp.where(sel, qn, jnp.zeros_like(qn))
        s = lax.dot_general(qh, k, (((1,), (1,)), ((), ())), preferred_element_type=F32)
        p = jnp.exp(s - jnp.max(s, axis=1, keepdims=True))
        l = jnp.sum(p, axis=1, keepdims=True)
        o = jnp.dot(p.astype(BF16), v, preferred_element_type=F32)
        out = jnp.where(sel, o / l, out)
    return out


R_E1, R_E2, R_G1, R_G2, R_RANK1, R_RANK2 = range(6)


def _top2_route(logits, count_ref):
    assert N_EXPERTS == SUBLANES
    tm = logits.shape[0]
    lt = logits.T[:SUBLANES, :]
    row = lax.broadcasted_iota(jnp.int32, lt.shape, 0)
    m1 = jnp.max(lt, axis=0, keepdims=True)
    i1 = jnp.min(jnp.where(lt == m1, row, N_EXPERTS), axis=0, keepdims=True)
    lt2 = jnp.where(row == i1, NEG, lt)
    m2 = jnp.max(lt2, axis=0, keepdims=True)
    i2 = jnp.min(jnp.where(lt2 == m2, row, N_EXPERTS), axis=0, keepdims=True)
    g1 = 1.0 / (1.0 + jnp.exp(m2 - m1))
    g2 = 1.0 - g1
    chosen = jnp.logical_or(row == i1, row == i2)
    onehot = jnp.where(chosen, 1.0, 0.0)
    r = lax.broadcasted_iota(jnp.int32, (tm, tm), 0)
    c = lax.broadcasted_iota(jnp.int32, (tm, tm), 1)
    earlier = jnp.where(r < c, 1.0, 0.0).astype(BF16)
    onehot16 = jnp.concatenate([onehot, jnp.zeros_like(onehot)], axis=0).astype(BF16)
    prefix = jnp.dot(onehot16, earlier, preferred_element_type=F32)[:SUBLANES] + count_ref[:, :1]
    count_ref[...] += jnp.sum(onehot, axis=1, keepdims=True)
    rank1 = jnp.sum(jnp.where(row == i1, prefix, 0.0), axis=0, keepdims=True)
    rank2 = jnp.sum(jnp.where(row == i2, prefix, 0.0), axis=0, keepdims=True)
    rec = jnp.zeros(lt.shape, F32)
    for idx, val in ((R_E1, i1.astype(F32)), (R_E2, i2.astype(F32)), (R_G1, g1), (R_G2, g2),
                     (R_RANK1, rank1), (R_RANK2, rank2)):
        rec = jnp.where(row == idx, val, rec)
    return rec


def _mixout_kernel(*refs, with_router):
    if with_router:
        (tok_ref, mq_ref, kv_ref, qg_ref, wo_ref, x_ref, g2_ref, wr_ref,
         x1_ref, hp_ref, route_ref, idx_ref, count_ref) = refs
    else:
        tok_ref, mq_ref, kv_ref, qg_ref, wo_ref, x_ref, g2_ref, x1_ref, h2_ref = refs
    mo = _mem_attention(mq_ref[...].astype(F32), qg_ref[...], kv_ref[0])
    y = jnp.dot(tok_ref[...], wo_ref[:TOK_W, :], preferred_element_type=F32)
    y = y + jnp.dot(mo.astype(BF16), wo_ref[TOK_W:, :], preferred_element_type=F32)
    x1 = x_ref[...] + y
    x1_ref[...] = x1
    h2 = _rms(x1, g2_ref[...])
    if with_router:
        @pl.when(pl.program_id(0) == 0)
        def _():
            count_ref[...] = jnp.zeros_like(count_ref)

        hp_ref[...] = h2
        wr = wr_ref[...]
        w_hi = wr.astype(BF16)
        w_lo = (wr - w_hi.astype(F32)).astype(BF16)
        h_hi = h2.astype(BF16)
        h_lo = (h2 - h_hi.astype(F32)).astype(BF16)
        both = jnp.dot(h_hi, jnp.concatenate([w_hi, w_lo], axis=1), preferred_element_type=F32)
        logits = both[:, :LANES] + both[:, LANES:] + jnp.dot(h_lo, w_hi, preferred_element_type=F32)
        rec = _top2_route(logits, count_ref)
        idx_ref[0] = rec.astype(jnp.int32)
        route_ref[...] = jnp.concatenate([rec, jnp.zeros((LANES - SUBLANES, rec.shape[1]), F32)], axis=0).T
    else:
        h2_ref[...] = h2.astype(BF16)


def _mixout(tok, mq_src, mq_block, kv, qg, wo, x, g2, wr=None, tm=512):
    t = x.shape[0]
    tiles_per_batch = t // kv.shape[0] // tm
    with_router = wr is not None
    in_specs = [pl.BlockSpec((tm, TOK_W), lambda i: (i, 0)),
                pl.BlockSpec((tm, MEM_W), lambda i: (i, mq_block)),
                pl.BlockSpec((1, MEM_LEN, 2 * MEM_W), lambda i: (i // tiles_per_batch, 0, 0)),
                pl.BlockSpec((1, MEM_W), lambda i: (0, 0)),
                pl.BlockSpec((D_MODEL, D_MODEL), lambda i: (0, 0)),
                pl.BlockSpec((tm, D_MODEL), lambda i: (i, 0)),
                pl.BlockSpec((1, D_MODEL), lambda i: (0, 0))]
    args = [tok, mq_src, kv, qg, wo, x, g2]
    row_spec = lambda n: pl.BlockSpec((tm, n), lambda i: (i, 0))
    if with_router:
        in_specs.append(pl.BlockSpec((D_MODEL, LANES), lambda i: (0, 0)))
        args.append(wr)
        out_shape = [jax.ShapeDtypeStruct((t, D_MODEL), F32), jax.ShapeDtypeStruct((t, D_MODEL), F32),
                     jax.ShapeDtypeStruct((t, LANES), F32), jax.ShapeDtypeStruct((t // tm, SUBLANES, tm), jnp.int32),
                     jax.ShapeDtypeStruct((SUBLANES, LANES), F32)]
        out_specs = [row_spec(D_MODEL), row_spec(D_MODEL), row_spec(LANES),
                     pl.BlockSpec((1, SUBLANES, tm), lambda i: (i, 0, 0)),
                     pl.BlockSpec((SUBLANES, LANES), lambda i: (0, 0))]
    else:
        out_shape = [jax.ShapeDtypeStruct((t, D_MODEL), F32), jax.ShapeDtypeStruct((t, D_MODEL), BF16)]
        out_specs = [row_spec(D_MODEL), row_spec(D_MODEL)]
    return pl.pallas_call(
        functools.partial(_mixout_kernel, with_router=with_router),
        out_shape=out_shape,
        grid=(t // tm,),
        in_specs=in_specs,
        out_specs=out_specs,
        compiler_params=_cparams(("arbitrary",) if with_router else ("parallel",)),
        name="mixout_router" if with_router else "mixout",
    )(*args)


def _ffn_kernel(h_ref, x_ref, wg_ref, wu_ref, wd_ref, gn_ref, xo_ref, ho_ref, acc_ref):
    f = pl.program_id(1)
    h = h_ref[...]
    a = jnp.dot(h, wg_ref[...], preferred_element_type=F32)
    u = jnp.dot(h, wu_ref[...], preferred_element_type=F32)
    act = a * jax.nn.sigmoid(a) * u
    contrib = jnp.dot(act.astype(BF16), wd_ref[...], preferred_element_type=F32)
    acc_ref[...] = jnp.where(f == 0, x_ref[...], acc_ref[...]) + contrib

    @pl.when(f == pl.num_programs(1) - 1)
    def _():
        xn = acc_ref[...]
        xo_ref[...] = xn
        ho_ref[...] = _rms(xn, gn_ref[...]).astype(BF16)


def _ffn(h, x, wg, wu, wd, gn, tm=512, tf=1408):
    t = x.shape[0]
    fdim = wg.shape[1]
    row = pl.BlockSpec((tm, D_MODEL), lambda i, f: (i, 0))
    return pl.pallas_call(
        _ffn_kernel,
        out_shape=[jax.ShapeDtypeStruct((t, D_MODEL), F32), jax.ShapeDtypeStruct((t, D_MODEL), BF16)],
        grid=(t // tm, fdim // tf),
        in_specs=[row, row,
                  pl.BlockSpec((D_MODEL, tf), lambda i, f: (0, f)),
                  pl.BlockSpec((D_MODEL, tf), lambda i, f: (0, f)),
                  pl.BlockSpec((tf, D_MODEL), lambda i, f: (f, 0)),
                  pl.BlockSpec((1, D_MODEL), lambda i, f: (0, 0))],
        out_specs=[row, row],
        scratch_shapes=[pltpu.VMEM((tm, D_MODEL), F32)],
        compiler_params=_cparams(("parallel", "arbitrary")),
        name="ffn",
    )(h, x, wg, wu, wd, gn)


def _row_copy(src_ref, src_row, dst_ref, dst_row, sem):
    return pltpu.make_async_copy(src_ref.at[pl.ds(src_row, 1), :], dst_ref.at[pl.ds(dst_row, 1), :], sem)


def _dispatch_kernel(pad_start_ref, pad_len_ref, pos_ref, hp_ref, xs_ref, zeros_ref, sem, zsem, *, tm, tmr):
    nbits = tmr.bit_length() - 1

    @pl.when(pl.program_id(0) == 0)
    def _():
        zeros_ref[...] = jnp.zeros_like(zeros_ref)
        for e in range(N_EXPERTS):
            start = pad_start_ref[e]
            length = pad_len_ref[e]
            singles = length & (SUBLANES - 1)
            for j in range(SUBLANES - 1):
                @pl.when(j < singles)
                def _(j=j, start=start):
                    cp = _row_copy(zeros_ref, 0, xs_ref, start + j, zsem)
                    cp.start()
                    cp.wait()

            done = start + singles
            for b in range(SUBLANES.bit_length() - 1, nbits):
                n = 1 << b
                bit = (length >> b) & 1

                @pl.when(bit == 1)
                def _(n=n, done=done):
                    cp = pltpu.make_async_copy(zeros_ref.at[pl.ds(0, n), :],
                                               xs_ref.at[pl.ds(pl.multiple_of(done, SUBLANES), n), :], zsem)
                    cp.start()
                    cp.wait()

                done = done + bit * n

        n_tiles = xs_ref.shape[0] // tmr
        half = tmr // 2
        for j in range(n_tiles - N_EXPERTS, n_tiles):
            @pl.when(j * tmr >= pad_start_ref[N_EXPERTS])
            def _(j=j):
                for c in range(2):
                    cp = pltpu.make_async_copy(zeros_ref, xs_ref.at[pl.ds(j * tmr + c * half, half), :], zsem)
                    cp.start()
                    cp.wait()

    def issue(r, carry):
        _row_copy(hp_ref, r, xs_ref, pos_ref[0, 0, r], sem).start(priority=0)
        _row_copy(hp_ref, r, xs_ref, pos_ref[0, 1, r], sem).start(priority=1)
        return carry

    lax.fori_loop(0, tm, issue, 0, unroll=ISSUE_UNROLL)
    for _ in range(2):
        pltpu.make_async_copy(hp_ref, xs_ref.at[pl.ds(0, tm), :], sem).wait()


def _dispatch(pad_start, pad_len, pos, hp, n_rows, tm, tmr):
    t, w = hp.shape
    return pl.pallas_call(
        functools.partial(_dispatch_kernel, tm=tm, tmr=tmr),
        out_shape=jax.ShapeDtypeStruct((n_rows, w), hp.dtype),
        grid_spec=pltpu.PrefetchScalarGridSpec(
            num_scalar_prefetch=2,
            grid=(t // tm,),
            in_specs=[pl.BlockSpec((1, 2, tm), lambda i, ps, pn: (i, 0, 0), memory_space=pltpu.SMEM),
                      pl.BlockSpec((tm, w), lambda i, ps, pn: (i, 0))],
            out_specs=pl.BlockSpec(memory_space=pl.ANY),
            scratch_shapes=[pltpu.VMEM((tmr // 2, w), hp.dtype), pltpu.SemaphoreType.DMA,
                            pltpu.SemaphoreType.DMA]),
        compiler_params=_cparams(("arbitrary",)),
        name="moe_dispatch",
    )(pad_start, pad_len, pos, hp)


def _experts_kernel(te_ref, tv_ref, xs_ref, wg_ref, wu_ref, wd_ref, y_ref):
    i = pl.program_id(0)
    f = pl.program_id(1)

    @pl.when(tv_ref[i] == 1)
    def _():
        h = xs_ref[...].astype(BF16)
        a = jnp.dot(h, wg_ref[0], preferred_element_type=F32)
        u = jnp.dot(h, wu_ref[0], preferred_element_type=F32)
        act = a * jax.nn.sigmoid(a) * u
        contrib = jnp.dot(act.astype(BF16), wd_ref[0], preferred_element_type=F32)
        y_ref[...] = jnp.where(f == 0, 0.0, y_ref[...]) + contrib

    @pl.when(jnp.logical_and(tv_ref[i] == 0, f == 0))
    def _():
        y_ref[...] = jnp.zeros_like(y_ref)


def _experts(tile_expert, tile_valid, xs, wg, wu, wd, tmr, tf):
    n_rows, w = xs.shape
    fdim = wg.shape[2]
    nf = fdim // tf
    fidx = lambda i, f, te, tv: jnp.where(tv[i] == 1, f, nf - 1)
    return pl.pallas_call(
        _experts_kernel,
        out_shape=jax.ShapeDtypeStruct((n_rows, D_MODEL), F32),
        grid_spec=pltpu.PrefetchScalarGridSpec(
            num_scalar_prefetch=2,
            grid=(n_rows // tmr, nf),
            in_specs=[pl.BlockSpec((tmr, w), lambda i, f, te, tv: (jnp.where(tv[i] == 1, i, 0), 0)),
                      pl.BlockSpec((1, D_MODEL, tf), lambda i, f, te, tv: (te[i], 0, fidx(i, f, te, tv))),
                      pl.BlockSpec((1, D_MODEL, tf), lambda i, f, te, tv: (te[i], 0, fidx(i, f, te, tv))),
                      pl.BlockSpec((1, tf, D_MODEL), lambda i, f, te, tv: (te[i], fidx(i, f, te, tv), 0))],
            out_specs=pl.BlockSpec((tmr, D_MODEL), lambda i, f, te, tv: (i, 0))),
        compiler_params=_cparams(("parallel", "arbitrary")),
        name="moe_experts",
    )(tile_expert, tile_valid, xs, wg, wu, wd)


def _combine_kernel(pos_ref, route_ref, x_ref, y_ref, o_ref, buf_ref, sem, *, tm):
    def issue(r, carry):
        _row_copy(y_ref, pos_ref[0, 0, r], buf_ref.at[0], r, sem).start(priority=0)
        _row_copy(y_ref, pos_ref[0, 1, r], buf_ref.at[1], r, sem).start(priority=1)
        return carry

    lax.fori_loop(0, tm, issue, 0, unroll=ISSUE_UNROLL)
    for k in range(2):
        pltpu.make_async_copy(y_ref.at[pl.ds(0, tm), :], buf_ref.at[k], sem).wait()
    route = route_ref[...]
    g1 = route[:, R_G1:R_G1 + 1]
    g2 = route[:, R_G2:R_G2 + 1]
    o_ref[...] = x_ref[...] + (g1 * buf_ref[0] + g2 * buf_ref[1])


def _combine(pos, route, x, y, tm):
    t = x.shape[0]
    return pl.pallas_call(
        functools.partial(_combine_kernel, tm=tm),
        out_shape=jax.ShapeDtypeStruct((t, D_MODEL), F32),
        grid=(t // tm,),
        in_specs=[pl.BlockSpec((1, 2, tm), lambda i: (i, 0, 0), memory_space=pltpu.SMEM),
                  pl.BlockSpec((tm, LANES), lambda i: (i, 0)),
                  pl.BlockSpec((tm, D_MODEL), lambda i: (i, 0)),
                  pl.BlockSpec(memory_space=pl.ANY)],
        out_specs=pl.BlockSpec((tm, D_MODEL), lambda i: (i, 0)),
        scratch_shapes=[pltpu.VMEM((2, tm, D_MODEL), F32), pltpu.SemaphoreType.DMA],
        compiler_params=_cparams(("arbitrary",)),
        name="moe_combine",
    )(pos, route, x, y)


def _moe(hp, route, idx, counts, x, wg, wu, wd, tmr=512, tf=1792):
    t = hp.shape[0]
    tm = idx.shape[2]
    n_tiles = 2 * t // tmr + N_EXPERTS
    n_rows = n_tiles * tmr
    cnt = counts[:, 0].astype(jnp.int32)
    padded = (cnt + tmr - 1) // tmr * tmr
    ends = jnp.cumsum(padded)
    offs = ends - padded
    def sorted_rows(e, rank):
        off = jnp.zeros_like(e)
        for k in range(N_EXPERTS):
            off = jnp.where(e == k, offs[k], off)
        return off + rank
    pos = jnp.stack([sorted_rows(idx[:, R_E1, :], idx[:, R_RANK1, :]),
                     sorted_rows(idx[:, R_E2, :], idx[:, R_RANK2, :])], axis=1)
    pos = pos.reshape(-1, COPY_TILES, 2, tm).transpose(0, 2, 1, 3).reshape(-1, 2, COPY_TILES * tm)
    tm = COPY_TILES * tm
    tile_start = jnp.arange(n_tiles, dtype=jnp.int32) * tmr
    tile_valid = (tile_start < ends[-1]).astype(jnp.int32)
    last_tile = ends[-1] // tmr - 1
    clamped = jnp.minimum(tile_start, last_tile * tmr)
    tile_expert = jnp.sum((clamped[:, None] >= ends[None, :]).astype(jnp.int32), axis=1)
    pad_start = jnp.concatenate([offs + cnt, ends[-1:]])
    xs = _dispatch(pad_start, padded - cnt, pos, hp, n_rows, tm, tmr)
    y = _experts(tile_expert, tile_valid, xs, wg, wu, wd, tmr, tf)
    return _combine(pos, route, x, y, tm)


def _inproj1_kernel(x_ref, w_ref, z_ref, xbc_ref, mq_ref, dt_ref):
    h = x_ref[...]
    c1 = TOK_W
    c2 = c1 + SSM_CONV_DIM
    c3 = c2 + MEM_W
    z_ref[...] = jnp.dot(h, w_ref[:, :c1], preferred_element_type=F32).astype(BF16)
    xbc_ref[...] = jnp.dot(h, w_ref[:, c1:c2], preferred_element_type=F32).astype(BF16)
    mq_ref[...] = jnp.dot(h, w_ref[:, c2:c3], preferred_element_type=F32).astype(BF16)
    dt_ref[...] = jnp.dot(h, w_ref[:, c3:], preferred_element_type=F32)


def _inproj1(h, w, tm=1024):
    t = h.shape[0]
    row = lambda n: pl.BlockSpec((tm, n), lambda i: (i, 0))
    return pl.pallas_call(
        _inproj1_kernel,
        out_shape=[jax.ShapeDtypeStruct((t, TOK_W), BF16), jax.ShapeDtypeStruct((t, SSM_CONV_DIM), BF16),
                   jax.ShapeDtypeStruct((t, MEM_W), BF16), jax.ShapeDtypeStruct((t, LANES), F32)],
        grid=(t // tm,),
        in_specs=[row(D_MODEL), pl.BlockSpec(w.shape, lambda i: (0, 0))],
        out_specs=[row(TOK_W), row(SSM_CONV_DIM), row(MEM_W), row(LANES)],
        compiler_params=_cparams(("parallel",)),
        name="inproj1",
    )(h, w)


def _split_dot(a, b, terms, split_rhs):
    rem = b if split_rhs else a
    pieces = []
    for _ in range(terms):
        piece = rem.astype(BF16)
        pieces.append(piece)
        rem = rem - piece.astype(F32)
    if split_rhs:
        n = b.shape[1]
        wide = jnp.dot(a, jnp.concatenate(pieces, axis=1), preferred_element_type=F32)
        out = wide[:, :n]
        for t in range(1, terms):
            out = out + wide[:, t * n:(t + 1) * n]
        return out
    return jnp.dot(jnp.concatenate(pieces, axis=1), jnp.concatenate([b] * terms, axis=0),
                   preferred_element_type=F32)


def _ssd_kernel(z_ref, xbc_ref, dt_ref, cw_ref, cb_ref, dtb_ref, alog_ref, dsk_ref, ng_ref, ex_ref,
                o_ref, xe_ref, st_ref, *, tl):
    j = pl.program_id(1)
    pad = 8

    @pl.when(j == 0)
    def _():
        xe_ref[0:pad, :] = jnp.zeros((pad, SSM_CONV_DIM), F32)
        st_ref[...] = jnp.zeros_like(st_ref)

    x = xbc_ref[0].astype(F32)
    xe_ref[pad:pad + tl, :] = x
    acc = cb_ref[...] + cw_ref[SSM_CONV_K - 1:SSM_CONV_K, :] * x
    for k in range(SSM_CONV_K - 1):
        sh = SSM_CONV_K - 1 - k
        acc = acc + cw_ref[k:k + 1, :] * xe_ref[pad - sh:pad - sh + tl, :]
    xe_ref[0:pad, :] = x[tl - pad:tl, :]
    xc = acc * jax.nn.sigmoid(acc)
    xs = xc[:, :TOK_W]

    dt_in = dt_ref[0] + dtb_ref[...]
    dt = jnp.maximum(dt_in, 0.0) + jnp.log1p(jnp.exp(-jnp.abs(dt_in)))
    a = -jnp.exp(alog_ref[...])
    dta = dt * a
    r = lax.broadcasted_iota(jnp.int32, (tl, tl), 0)
    c = lax.broadcasted_iota(jnp.int32, (tl, tl), 1)
    tril = r >= c
    ltri = jnp.where(tril, 1.0, 0.0).astype(BF16)
    cum = _split_dot(ltri, dta, terms=3, split_rhs=True)
    cum_t = cum.T
    dt_t = dt.T
    lane = lax.broadcasted_iota(jnp.int32, (tl, LANES), 1)

    hg = SSM_HEADS // SSM_GROUPS
    y_pairs = []
    cbs = []
    for g in range(SSM_GROUPS):
        bm = xc[:, TOK_W + g * SSM_STATE:TOK_W + (g + 1) * SSM_STATE].astype(BF16)
        cm = xc[:, TOK_W + (SSM_GROUPS + g) * SSM_STATE:TOK_W + (SSM_GROUPS + g + 1) * SSM_STATE].astype(BF16)
        cbs.append(lax.dot_general(cm, bm, (((1,), (1,)), ((), ())), preferred_element_type=F32))
    for pr in range(SSM_HEADS // 2):
        xp = xs[:, pr * LANES:(pr + 1) * LANES].astype(BF16)
        ys = []
        for hh in (2 * pr, 2 * pr + 1):
            g = hh // hg
            seg = cum[:, hh:hh + 1] - cum_t[hh:hh + 1, :]
            decay = jnp.exp(jnp.where(tril, seg, NEG))
            w = cbs[g] * decay * dt_t[hh:hh + 1, :]
            ys.append(jnp.dot(w.astype(BF16), xp, preferred_element_type=F32))
        y_pairs.append(jnp.where(lane < SSM_HD, ys[0], ys[1]))
    y = jnp.concatenate(y_pairs, axis=1)

    expcum = jnp.exp(cum)
    to_end = jnp.exp(cum[tl - 1:tl, :] - cum) * dt
    stacked = jnp.concatenate([expcum, to_end], axis=0)
    exd = _split_dot(stacked, ex_ref[...], terms=2, split_rhs=False)
    expcum_x = exd[:tl]
    xw = (xs * exd[tl:]).astype(BF16)
    gw = TOK_W // SSM_GROUPS
    y_off = []
    for g in range(SSM_GROUPS):
        cs = slice(g * gw, (g + 1) * gw)
        bm_t = xc[:, TOK_W + g * SSM_STATE:TOK_W + (g + 1) * SSM_STATE].T.astype(BF16)
        cm = xc[:, TOK_W + (SSM_GROUPS + g) * SSM_STATE:TOK_W + (SSM_GROUPS + g + 1) * SSM_STATE].astype(BF16)
        sg = st_ref[:, cs]
        y_off.append(jnp.dot(cm, sg.astype(BF16), preferred_element_type=F32) * expcum_x[:, cs])
        st_ref[:, cs] = sg * expcum_x[tl - 1:tl, cs] + jnp.dot(bm_t, xw[:, cs], preferred_element_type=F32)
    y = y + jnp.concatenate(y_off, axis=1) + dsk_ref[...] * xs
    zf = z_ref[0].astype(F32)
    y = y * (zf * jax.nn.sigmoid(zf))
    outs = []
    for g in range(SSM_GROUPS):
        cs = slice(g * gw, (g + 1) * gw)
        outs.append(_rms(y[:, cs], ng_ref[:, cs]))
    o_ref[0] = jnp.concatenate(outs, axis=1).astype(o_ref.dtype)


def _ssd(z3, xbc3, dt3, cw, cb, dtb, alog, dsk, ng, ex, tl=256):
    b, s, _ = z3.shape
    full = lambda a: pl.BlockSpec(a.shape, lambda bi, j: (0, 0))
    blk = lambda n: pl.BlockSpec((1, tl, n), lambda bi, j: (bi, j, 0))
    return pl.pallas_call(
        functools.partial(_ssd_kernel, tl=tl),
        out_shape=jax.ShapeDtypeStruct((b, s, TOK_W), BF16),
        grid=(b, s // tl),
        in_specs=[blk(TOK_W), blk(SSM_CONV_DIM), blk(LANES), full(cw), full(cb), full(dtb), full(alog),
                  full(dsk), full(ng), full(ex)],
        out_specs=blk(TOK_W),
        scratch_shapes=[pltpu.VMEM((tl + 8, SSM_CONV_DIM), F32), pltpu.VMEM((SSM_STATE, TOK_W), F32)],
        compiler_params=_cparams(("parallel", "arbitrary")),
        name="conv_ssd",
    )(z3, xbc3, dt3, cw, cb, dtb, alog, dsk, ng, ex)


def _pad_lanes(v, n=LANES):
    return jnp.pad(v, [(0, 0)] * (v.ndim - 1) + [(0, n - v.shape[-1])])


def kernel(x, mem, ln1_g, ln2_g, mem_norm_g, w_out, mem_w_kv, mem_qn_g, mem_kn_g, da_w_in, da_qn_g, da_kn_g,
           da_lq1, da_lk1, da_lq2, da_lk2, da_sub_g, ssm_w_in, ssm_conv_w, ssm_conv_b, ssm_dt_bias, ssm_a_log,
           ssm_d, ssm_norm_g, ffn_w_gate, ffn_w_up, ffn_w_down, moe_w_router, moe_w_gate, moe_w_up, moe_w_down):
    b, s, d = x.shape
    t = b * s
    row = lambda v: v.reshape(1, -1).astype(F32)
    xt = x.reshape(t, d)

    mem_qg = jnp.tile(mem_qn_g.astype(F32) * (MEM_HD ** -0.5), (1, MEM_HEADS))
    mem_kg = jnp.tile(mem_kn_g.astype(F32), (1, MEM_HEADS))[:, None, :]
    kv = _memkv(mem, row(mem_norm_g), mem_w_kv.astype(BF16), mem_kg)

    lambda_init = 0.8 - 0.6 * math.exp(-0.3 * 0)
    qkg = jnp.concatenate([jnp.tile(da_qn_g[0].astype(F32) * (DA_DH ** -0.5 * LOG2E), 2 * DA_HEADS),
                           jnp.tile(da_kn_g[0].astype(F32), 2 * DA_HEADS)]).reshape(1, -1)
    qt, kn, vt, mq0 = _inproj0(xt, row(ln1_g[0]), da_w_in[0].astype(BF16), qkg, b)
    late_w = ((moe_w_gate[0], CAST_CHUNKS), (moe_w_up[0], CAST_CHUNKS), (moe_w_down[0], CAST_CHUNKS),
              (ffn_w_gate[0], CAST_CHUNKS), (ffn_w_up[0], CAST_CHUNKS), (ffn_w_down[0], CAST_CHUNKS // 4),
              (w_out, CAST_CHUNKS))
    riders = tuple(w.reshape(n, -1, w.shape[-1]) for w, n in late_w)
    tok, w16 = _diff_attention(qt, kn.reshape(b, s, TOK_W), vt, row(da_lq1[0]), row(da_lk1[0]),
                               row(da_lq2[0]), row(da_lk2[0]), row(da_sub_g[0]), lambda_init, riders)
    moe_wg, moe_wu, moe_wd, ffn_wg, ffn_wu, ffn_wd, wo = (a.reshape(w.shape) for a, (w, _) in zip(w16, late_w))
    x1, h2 = _mixout(tok.reshape(t, TOK_W), mq0, 0, kv[0], mem_qg[0:1], wo[0], xt, row(ln2_g[0]), tm=1024)
    x2, h3 = _ffn(h2, x1, ffn_wg, ffn_wu, ffn_wd, row(ln1_g[1]))

    w_in = ssm_w_in[0]
    o2 = TOK_W + SSM_CONV_DIM
    o3 = o2 + SSM_HEADS
    w1 = jnp.concatenate([w_in[:, :o2], w_in[:, o3:], _pad_lanes(w_in[:, o2:o3])], axis=1).astype(BF16)
    z, xbc, mq, dt = _inproj1(h3, w1)
    expand = jnp.repeat(jnp.eye(SSM_HEADS, dtype=F32), SSM_HD, axis=1)
    expand = jnp.pad(expand, ((0, LANES - SSM_HEADS), (0, 0))).astype(BF16)
    dsk = jnp.repeat(ssm_d[0].astype(F32), SSM_HD).reshape(1, -1)
    tok1 = _ssd(z.reshape(b, s, -1), xbc.reshape(b, s, -1), dt.reshape(b, s, -1),
                ssm_conv_w[0].astype(F32), row(ssm_conv_b[0]), _pad_lanes(row(ssm_dt_bias[0])),
                _pad_lanes(row(ssm_a_log[0])), dsk, row(ssm_norm_g[0]), expand)
    x3, hp, route, idx, counts = _mixout(tok1.reshape(t, TOK_W), mq, 0, kv[1], mem_qg[1:2], wo[1], x2,
                                         row(ln2_g[1]), wr=_pad_lanes(moe_w_router[0].astype(F32)))
    x4 = _moe(hp, route, idx, counts, x3, moe_wg, moe_wu, moe_wd)
    return x4.reshape(b, s, d)
```
